```python
import math
import jax, jax.numpy as jnp
from jax import lax
import numpy as np


D_MODEL = 1024
BATCH = 8
SEQ = 4096
DEPTH = 1
DEC_BATCH = 2
DEC_SEQ = 8192
PAST_LEN = 128

HEAD_DIM = 64
N_ATTN_HEADS = 8
ATTN_WIDTH = N_ATTN_HEADS * HEAD_DIM
ROT_DIM = HEAD_DIM // 4
ROPE_THETA = 500000.0
DILATED_PATTERNS = ((128, 1), (512, 4), (2048, 16))
N_SSM_HEADS = 8
SSM_HEAD_DIM = 64
SSM_WIDTH = N_SSM_HEADS * SSM_HEAD_DIM
SSM_GROUPS = 2
D_STATE = 128
CONV_K = 5
CHUNK = 128
CONV_CH = SSM_WIDTH + 2 * SSM_GROUPS * D_STATE
MIX_WIDTH = ATTN_WIDTH + SSM_WIDTH
IN_SIZES = (ATTN_WIDTH, ATTN_WIDTH, ATTN_WIDTH, SSM_WIDTH, SSM_WIDTH,
            SSM_GROUPS * D_STATE, SSM_GROUPS * D_STATE, N_SSM_HEADS, N_SSM_HEADS)
IN_WIDTH = 3 * ATTN_WIDTH + 2 * SSM_WIDTH + 2 * SSM_GROUPS * D_STATE + 2 * N_SSM_HEADS
N_EXPERTS = 16
CAPACITY_FACTOR = 2
D_FF_EXPERT = 2816
ALPHA = (2.0 * DEPTH) ** 0.25
BETA = (8.0 * DEPTH) ** -0.25
LN_EPS = 1e-5
RMS_EPS = 1e-5

kernel_name = 'hybrid_dilated_attn_ssd_ec_moe_encoder'


def layer_norm(x, g, b):
    xf = x.astype(jnp.float32)
    mu = jnp.mean(xf, axis=-1, keepdims=True)
    var = jnp.mean(jnp.square(xf - mu), axis=-1, keepdims=True)
    return ((xf - mu) * lax.rsqrt(var + LN_EPS) * g.astype(jnp.float32) + b.astype(jnp.float32)).astype(x.dtype)


def partial_rotary(t, pos):
    half = ROT_DIM // 2
    inv = ROPE_THETA ** (-jnp.arange(half, dtype=jnp.float32) * 2.0 / ROT_DIM)
    ang = pos.astype(jnp.float32)[:, None] * inv[None, :]
    cos = jnp.cos(ang)[None, :, None, :].astype(t.dtype)
    sin = jnp.sin(ang)[None, :, None, :].astype(t.dtype)
    t1 = t[..., :half]
    t2 = t[..., half:ROT_DIM]
    return jnp.concatenate([t1 * cos - t2 * sin, t2 * cos + t1 * sin, t[..., ROT_DIM:]], axis=-1)


def dilated_window_attention(q, k, v, window, dil):
    b, s, h, e = q.shape
    half = window // (2 * dil)
    L = s // dil
    nb = -(-L // half)
    Lp = nb * half

    def to_sub(t):
        t = t.reshape(b, L, dil, h, e).transpose(0, 2, 1, 3, 4)
        t = jnp.pad(t, ((0, 0), (0, 0), (0, Lp - L), (0, 0), (0, 0)))
        return t.reshape(b, dil, nb, half, h, e)

    def band(t):
        tp = jnp.pad(t, ((0, 0), (0, 0), (1, 1), (0, 0), (0, 0), (0, 0)))
        return jnp.concatenate([tp[:, :, :-2], tp[:, :, 1:-1], tp[:, :, 2:]], axis=3)

    qs = to_sub(q)
    kb = band(to_sub(k))
    vb = band(to_sub(v))
    qpos = jnp.arange(nb)[:, None] * half + jnp.arange(half)[None, :]
    kpos = (jnp.arange(nb)[:, None] - 1) * half + jnp.arange(3 * half)[None, :]
    mask = ((jnp.abs(qpos[:, :, None] - kpos[:, None, :]) <= half)
            & (kpos[:, None, :] >= 0) & (kpos[:, None, :] < L))
    scores = jnp.einsum('bdnqhe,bdnkhe->bdnhqk', qs, kb,
                        preferred_element_type=jnp.float32) * (e ** -0.5)
    scores = jnp.where(mask[None, None, :, None], scores, -jnp.inf)
    lse = jax.nn.logsumexp(scores, axis=-1)
    p = jnp.exp(scores - lse[..., None])
    o = jnp.einsum('bdnhqk,bdnkhe->bdnqhe', p, vb.astype(jnp.float32))
    o = o.reshape(b, dil, Lp, h, e)[:, :, :L].transpose(0, 2, 1, 3, 4).reshape(b, s, h, e)
    lse = lse.transpose(0, 1, 2, 4, 3).reshape(b, dil, Lp, h)[:, :, :L]
    lse = lse.transpose(0, 2, 1, 3).reshape(b, s, h)
    return o, lse


def centred_dwconv(u, w, bias):
    out = lax.conv_general_dilated(u, w[:, None, :].astype(u.dtype), window_strides=(1,),
                                   padding=[(CONV_K // 2, CONV_K // 2)],
                                   dimension_numbers=('NWC', 'WIO', 'NWC'),
                                   feature_group_count=u.shape[-1])
    return out + bias.astype(u.dtype)


def ssd_scan(x, dt, A, Bm, Cm):
    b, s, h, p = x.shape
    g = Bm.shape[2]
    r = h // g
    n = Bm.shape[3]
    nc = s // CHUNK
    xc = x.astype(jnp.float32).reshape(b, nc, CHUNK, g, r, p)
    dtc = dt.reshape(b, nc, CHUNK, g, r)
    Bc = Bm.astype(jnp.float32).reshape(b, nc, CHUNK, g, n)
    Cc = Cm.astype(jnp.float32).reshape(b, nc, CHUNK, g, n)
    a_cs = jnp.cumsum(dtc * A.reshape(g, r), axis=2)
    xdt = xc * dtc[..., None]
    diff = a_cs[:, :, :, None] - a_cs[:, :, None, :]
    tri = jnp.tril(jnp.ones((CHUNK, CHUNK), dtype=bool))[:, :, None, None]
    Lm = jnp.exp(jnp.where(tri, diff, -jnp.inf))
    cb = jnp.einsum('bclgn,bcsgn->bclsg', Cc, Bc)
    y_diag = jnp.einsum('bclsgr,bcsgrp->bclgrp', cb[..., None] * Lm, xdt)
    decay_to_end = jnp.exp(a_cs[:, :, -1:] - a_cs)
    states = jnp.einsum('bclgn,bclgr,bclgrp->bcgrpn', Bc, decay_to_end, xdt)
    chunk_decay = jnp.exp(a_cs[:, :, -1])

    def step(prev, inp):
        st, dec = inp
        return prev * dec[..., None, None] + st, prev

    init = jnp.zeros((b, g, r, p, n), jnp.float32)
    _, prev_states = lax.scan(step, init, (states.swapaxes(0, 1), chunk_decay.swapaxes(0, 1)))
    prev_states = prev_states.swapaxes(0, 1)
    y_off = jnp.einsum('bclgn,bcgrpn,bclgr->bclgrp', Cc, prev_states, jnp.exp(a_cs))
    return (y_diag + y_off).reshape(b, s, h, p)


def hybrid_mixer(x, w_in, conv_w, conv_b, dt_bias, a_log, d_skip, ssm_norm_w, w_out):
    b, s, _ = x.shape
    proj = x @ w_in
    cuts = []
    acc = 0
    for sz in IN_SIZES[:-1]:
        acc += sz
        cuts.append(acc)
    q, k, v, z, xs, Bs, Cs, dtf, dtb = jnp.split(proj, cuts, axis=-1)

    pos = jnp.arange(s)
    q = partial_rotary(q.reshape(b, s, N_ATTN_HEADS, HEAD_DIM), pos)
    k = partial_rotary(k.reshape(b, s, N_ATTN_HEADS, HEAD_DIM), pos)
    v = v.reshape(b, s, N_ATTN_HEADS, HEAD_DIM)
    outs = []
    lses = []
    for window, dil in DILATED_PATTERNS:
        o_i, l_i = dilated_window_attention(q, k, v, window, dil)
        outs.append(o_i)
        lses.append(l_i)
    wts = jax.nn.softmax(jnp.stack(lses, axis=0), axis=0)
    attn = jnp.sum(wts[..., None] * jnp.stack(outs, axis=0), axis=0)
    attn = attn.reshape(b, s, ATTN_WIDTH).astype(x.dtype)

    xBC = jax.nn.silu(centred_dwconv(jnp.concatenate([xs, Bs, Cs], axis=-1), conv_w, conv_b))
    xs, Bs, Cs = jnp.split(xBC, [SSM_WIDTH, SSM_WIDTH + SSM_GROUPS * D_STATE], axis=-1)
    xh = xs.reshape(b, s, N_SSM_HEADS, SSM_HEAD_DIM)
    Bg = Bs.reshape(b, s, SSM_GROUPS, D_STATE)
    Cg = Cs.reshape(b, s, SSM_GROUPS, D_STATE)
    dtb32 = dt_bias.astype(jnp.float32)
    dt_f = jax.nn.softplus(dtf.astype(jnp.float32) + dtb32[0])
    dt_b = jax.nn.softplus(dtb.astype(jnp.float32) + dtb32[1])
    A = -jnp.exp(a_log.astype(jnp.float32))
    rev = lambda t: jnp.flip(t, axis=1)
    y_f = ssd_scan(xh, dt_f, A[0], Bg, Cg)
    y_b = rev(ssd_scan(rev(xh), rev(dt_b), A[1], rev(Bg), rev(Cg)))
    y = y_f + y_b + d_skip.astype(jnp.float32)[:, None] * xh.astype(jnp.float32)
    y = y.reshape(b, s, SSM_WIDTH) * jax.nn.silu(z.astype(jnp.float32))
    y = y * lax.rsqrt(jnp.mean(jnp.square(y), axis=-1, keepdims=True) + RMS_EPS) * ssm_norm_w.astype(jnp.float32)

    return jnp.concatenate([attn, y.astype(x.dtype)], axis=-1) @ w_out


def expert_choice_ffn(x, w_router, w_gate, w_up, w_down):
    b, s, d = x.shape
    T = b * s
    cap = CAPACITY_FACTOR * T // N_EXPERTS
    h = x.reshape(T, d)
    aff = jax.nn.softmax((h @ w_router).astype(jnp.float32), axis=-1)
    gate_vals, tok_idx = lax.top_k(aff.T, cap)
    xg = h[tok_idx]
    hid = jax.nn.silu(jnp.einsum('ecd,edf->ecf', xg, w_gate)) * jnp.einsum('ecd,edf->ecf', xg, w_up)
    out = jnp.einsum('ecf,efd->ecd', hid, w_down) * gate_vals[..., None].astype(x.dtype)
    y = jnp.zeros_like(h).at[tok_idx.reshape(-1)].add(out.reshape(-1, d))
    return y.reshape(b, s, d)


def setup_inputs(seed: int = 0) -> dict:
    key = jax.random.key(seed)
    ks = jax.random.split(key, 20)
    f32 = jnp.float32
    x_prompt = jax.random.normal(ks[0], (BATCH, SEQ, D_MODEL), f32)
    x_sample = jax.random.normal(ks[1], (DEC_BATCH, DEC_SEQ, D_MODEL), f32)
    w_in = jax.random.normal(ks[2], (DEPTH, D_MODEL, IN_WIDTH), f32) * D_MODEL ** -0.5
    conv_w = jax.random.normal(ks[3], (DEPTH, CONV_K, CONV_CH), f32) * CONV_K ** -0.5
    conv_b = 0.02 * jax.random.normal(ks[4], (DEPTH, CONV_CH), f32)
    u = jax.random.uniform(ks[5], (DEPTH, 2, N_SSM_HEADS), f32)
    dt0 = jnp.exp(u * (math.log(0.1) - math.log(1e-3)) + math.log(1e-3))
    dt_bias = dt0 + jnp.log(-jnp.expm1(-dt0))
    a_log = jnp.log(jax.random.uniform(ks[6], (DEPTH, 2, N_SSM_HEADS), f32, minval=1.0, maxval=16.0))
    d_skip = 1.0 + 0.1 * jax.random.normal(ks[7], (DEPTH, N_SSM_HEADS), f32)
    ssm_norm_w = 1.0 + 0.1 * jax.random.normal(ks[8], (DEPTH, SSM_WIDTH), f32)
    w_out = jax.random.normal(ks[9], (DEPTH, MIX_WIDTH, D_MODEL), f32) * (MIX_WIDTH ** -0.5) * BETA
    ln1_g = 1.0 + 0.1 * jax.random.normal(ks[10], (DEPTH, D_MODEL), f32)
    ln1_b = 0.02 * jax.random.normal(ks[11], (DEPTH, D_MODEL), f32)
    w_router = jax.random.normal(ks[12], (DEPTH, D_MODEL, N_EXPERTS), f32) * D_MODEL ** -0.5
    w_gate = jax.random.normal(ks[13], (DEPTH, N_EXPERTS, D_MODEL, D_FF_EXPERT), f32) * D_MODEL ** -0.5
    w_up = jax.random.normal(ks[14], (DEPTH, N_EXPERTS, D_MODEL, D_FF_EXPERT), f32) * D_MODEL ** -0.5
    w_down = jax.random.normal(ks[15], (DEPTH, N_EXPERTS, D_FF_EXPERT, D_MODEL), f32) * (D_FF_EXPERT ** -0.5) * BETA
    ln2_g = 1.0 + 0.1 * jax.random.normal(ks[16], (DEPTH, D_MODEL), f32)
    ln2_b = 0.02 * jax.random.normal(ks[17], (DEPTH, D_MODEL), f32)
    return {'x_prompt': x_prompt, 'x_sample': x_sample, 'w_in': w_in, 'conv_w': conv_w,
            'conv_b': conv_b, 'dt_bias': dt_bias, 'a_log': a_log, 'd_skip': d_skip,
            'ssm_norm_w': ssm_norm_w, 'w_out': w_out, 'ln1_g': ln1_g, 'ln1_b': ln1_b,
            'w_router': w_router, 'w_gate': w_gate, 'w_up': w_up, 'w_down': w_down,
            'ln2_g': ln2_g, 'ln2_b': ln2_b}


def reference(x_prompt, x_sample, w_in, conv_w, conv_b, dt_bias, a_log, d_skip, ssm_norm_w,
              w_out, ln1_g, ln1_b, w_router, w_gate, w_up, w_down, ln2_g, ln2_b):
    def trunk(x):
        for l in range(DEPTH):
            mix = hybrid_mixer(x, w_in[l], conv_w[l], conv_b[l], dt_bias[l], a_log[l],
                               d_skip[l], ssm_norm_w[l], w_out[l])
            x = layer_norm(ALPHA * x + mix, ln1_g[l], ln1_b[l])
            ffn = expert_choice_ffn(x, w_router[l], w_gate[l], w_up[l], w_down[l])
            x = layer_norm(ALPHA * x + ffn, ln2_g[l], ln2_b[l])
        return x

    y_prompt = trunk(x_prompt)
    y_sample = trunk(x_sample)
    return (y_prompt, y_sample)
```

```python
import functools

import jax
import jax.numpy as jnp
from jax import lax
from jax.experimental import pallas as pl
from jax.experimental.pallas import tpu as pltpu

F32 = jnp.float32
BF16 = jnp.bfloat16
I32 = jnp.int32
HIGHEST = lax.Precision.HIGHEST

D_MODEL = 1024
DEPTH = 1
HEAD_DIM = 64
N_ATTN_HEADS = 8
ATTN_WIDTH = N_ATTN_HEADS * HEAD_DIM
ROT_DIM = HEAD_DIM // 4
ROPE_THETA = 500000.0
DILATED_PATTERNS = ((128, 1), (512, 4), (2048, 16))
N_SSM_HEADS = 8
SSM_HEAD_DIM = 64
SSM_WIDTH = N_SSM_HEADS * SSM_HEAD_DIM
SSM_GROUPS = 2
D_STATE = 128
CONV_K = 5
CHUNK = 128
CONV_CH = SSM_WIDTH + 2 * SSM_GROUPS * D_STATE
MIX_WIDTH = ATTN_WIDTH + SSM_WIDTH
N_EXPERTS = 16
CAPACITY_FACTOR = 2
D_FF_EXPERT = 2816
ALPHA = (2.0 * DEPTH) ** 0.25
LN_EPS = 1e-5
RMS_EPS = 1e-5

LANES = 128
VMEM_LIMIT = 56 * 1024 * 1024
NEG = -1e30
HALF_WIN = 64

IN_PAD = 3 * ATTN_WIDTH + SSM_WIDTH + CONV_CH + LANES


def _params(sem):
    return pltpu.CompilerParams(dimension_semantics=sem, vmem_limit_bytes=VMEM_LIMIT)


def _inproj_kernel(x_ref, w_ref, c_ref, sa_ref, sb_ref, q_ref, k_ref, v_ref, z_ref, xbc_ref, dt_ref):
    x = x_ref[...].astype(BF16)

    def seg(lo, hi):
        return jnp.dot(x, w_ref[:, lo:hi], preferred_element_type=F32)

    c = c_ref[...]
    sa = sa_ref[...]
    sb = sb_ref[...]

    def rotary(t, scale):
        outs = []
        for g in range(ATTN_WIDTH // LANES):
            tg = t[:, g * LANES:(g + 1) * LANES]
            up = pltpu.roll(tg, LANES - ROT_DIM // 2, 1)
            dn = pltpu.roll(tg, ROT_DIM // 2, 1)
            outs.append((tg * c + up * sa + dn * sb) * scale)
        return jnp.concatenate(outs, axis=1)

    a = ATTN_WIDTH
    q_ref[...] = rotary(seg(0, a), HEAD_DIM ** -0.5)
    k_ref[...] = rotary(seg(a, 2 * a), 1.0)
    v_ref[...] = seg(2 * a, 3 * a)
    z_ref[...] = seg(3 * a, 3 * a + SSM_WIDTH)
    o = 3 * a + SSM_WIDTH
    xbc_ref[...] = seg(o, o + CONV_CH)
    dt_ref[...] = seg(o + CONV_CH, o + CONV_CH + LANES)


def _rotary_tables(seq):
    half = ROT_DIM // 2
    inv = ROPE_THETA ** (-jnp.arange(half, dtype=F32) * 2.0 / ROT_DIM)
    ang = jnp.arange(seq).astype(F32)[:, None] * inv[None, :]
    cos = jnp.cos(ang)
    sin = jnp.sin(ang)
    m = jnp.arange(LANES) % HEAD_DIM
    c = jnp.where(m[None, :] < ROT_DIM, cos[:, m % half], 1.0)
    sa = jnp.where(m[None, :] < half, -sin[:, m % half], 0.0)
    sb = jnp.where((m[None, :] >= half) & (m[None, :] < ROT_DIM), sin[:, m % half], 0.0)
    return c.astype(F32), sa.astype(F32), sb.astype(F32)


def _inproj(x2, w_pad, seq, tm=512):
    T = x2.shape[0]
    c, sa, sb = _rotary_tables(seq)
    nseq = seq // tm
    row = lambda i: (i, 0)
    tab = lambda i: (i % nseq, 0)
    outs = [jax.ShapeDtypeStruct((T, ATTN_WIDTH), F32)] * 3 + [
        jax.ShapeDtypeStruct((T, SSM_WIDTH), F32),
        jax.ShapeDtypeStruct((T, CONV_CH), F32),
        jax.ShapeDtypeStruct((T, LANES), F32)]
    return pl.pallas_call(
        _inproj_kernel,
        grid=(T // tm,),
        in_specs=[pl.BlockSpec((tm, D_MODEL), row),
                  pl.BlockSpec((D_MODEL, IN_PAD), lambda i: (0, 0)),
                  pl.BlockSpec((tm, LANES), tab),
                  pl.BlockSpec((tm, LANES), tab),
                  pl.BlockSpec((tm, LANES), tab)],
        out_specs=[pl.BlockSpec((tm, ATTN_WIDTH), row)] * 3 + [
            pl.BlockSpec((tm, SSM_WIDTH), row),
            pl.BlockSpec((tm, CONV_CH), row),
            pl.BlockSpec((tm, LANES), row)],
        out_shape=outs,
        compiler_params=_params(("parallel",)),
        name="inproj",
    )(x2, w_pad, c, sa, sb)


QB = 128
KB = 256
ATTN_GROUP = 4


def _attn_kernel(q_ref, k_ref, v_ref, o_ref, m_ref, l_ref, acc_ref, bias_ref, *, seq):
    row = lax.broadcasted_iota(I32, (2 * QB, KB), 0)
    col = lax.broadcasted_iota(I32, (2 * QB, KB), 1)
    iq = jnp.where(row >= QB, row - QB, row)
    for di in range(3):
        bias_ref[di] = jnp.where(jnp.abs(di * HALF_WIN + iq - col) <= HALF_WIN, 0.0, NEG).astype(F32)
    m_ref[...] = jnp.full(m_ref.shape, NEG, F32)
    l_ref[...] = jnp.zeros(l_ref.shape, F32)
    acc_ref[...] = jnp.zeros(acc_ref.shape, F32)

    lane = lax.broadcasted_iota(I32, (QB, LANES), 1)
    head0 = lane < HEAD_DIM

    for window, dil in DILATED_PATTERNS:
        assert window // (2 * dil) == HALF_WIN
        sub_len = seq // dil
        nblk = sub_len // QB
        assert sub_len >= KB and sub_len % QB == 0

        def rows(start, size, dil=dil):
            return pl.ds(start, size) if dil == 1 else pl.ds(start, size, stride=dil)

        def body(gi, carry, dil=dil, sub_len=sub_len, nblk=nblk, rows=rows):
            loaded = []
            for u in range(ATTN_GROUP):
                idx = gi * ATTN_GROUP + u
                r = idx // nblk
                n = idx - r * nblk
                kb = jnp.clip(n * QB - HALF_WIN, 0, sub_len - KB)
                di = (n * QB - kb) // HALF_WIN
                qrows = rows(r + dil * QB * n, QB)
                krows = rows(r + dil * kb, KB)
                loaded.append((qrows, q_ref[0, qrows, :], k_ref[0, krows, :], v_ref[0, krows, :], bias_ref[di],
                               m_ref[0, qrows, :], m_ref[1, qrows, :], l_ref[0, qrows, :], l_ref[1, qrows, :],
                               acc_ref[qrows, :]))
            results = []
            for qrows, q, k, v, bias, m0, m1, l0, l1, acc in loaded:
                qs = jnp.concatenate([jnp.where(head0, q, 0.0), jnp.where(head0, 0.0, q)], axis=0).astype(BF16)
                s = lax.dot_general(qs, k.astype(BF16), (((1,), (1,)), ((), ())), preferred_element_type=F32)
                s = s + bias
                m_prev = jnp.concatenate([m0, m1], axis=0)
                l_prev = jnp.concatenate([l0, l1], axis=0)
                m_new = jnp.maximum(m_prev, jnp.max(s, axis=1, keepdims=True))
                alpha = jnp.exp(m_prev - m_new)
                p = jnp.exp(s - jnp.concatenate([m_new, m_new], axis=1))
                l_new = alpha * l_prev + jnp.sum(p, axis=1, keepdims=True)
                pv = jnp.dot(p.astype(BF16), v.astype(BF16), preferred_element_type=F32)
                a_l = jnp.where(head0, alpha[:QB], alpha[QB:])
                pv_l = jnp.where(head0, pv[:QB], pv[QB:])
                results.append((qrows, a_l * acc + pv_l, m_new, l_new))
            for qrows, acc_new, m_new, l_new in results:
                acc_ref[qrows, :] = acc_new
                m_ref[0, qrows, :] = m_new[:QB]
                m_ref[1, qrows, :] = m_new[QB:]
                l_ref[0, qrows, :] = l_new[:QB]
                l_ref[1, qrows, :] = l_new[QB:]
            return carry

        assert (dil * nblk) % ATTN_GROUP == 0
        lax.fori_loop(0, dil * nblk // ATTN_GROUP, body, 0)

    def fin(i, carry):
        rws = pl.ds(pl.multiple_of(i * QB, QB), QB)
        den = jnp.where(head0, l_ref[0, rws, :], l_ref[1, rws, :])
        o_ref[0, rws, :] = (acc_ref[rws, :] / den).astype(o_ref.dtype)
        return carry

    lax.fori_loop(0, seq // QB, fin, 0)


def _attention(q, k, v):
    B, seq, _ = q.shape
    npair = ATTN_WIDTH // LANES
    spec = pl.BlockSpec((1, seq, LANES), lambda b, h: (b, 0, h))
    return pl.pallas_call(
        functools.partial(_attn_kernel, seq=seq),
        grid=(B, npair),
        in_specs=[spec, spec, spec],
        out_specs=spec,
        out_shape=jax.ShapeDtypeStruct((B, seq, ATTN_WIDTH), BF16),
        scratch_shapes=[pltpu.VMEM((2, seq, LANES), F32),
                        pltpu.VMEM((2, seq, LANES), F32),
                        pltpu.VMEM((seq, LANES), F32),
                        pltpu.VMEM((3, 2 * QB, KB), F32)],
        compiler_params=_params(("parallel", "parallel")),
        name="attn",
    )(q, k, v)


SSD_TILE = 256
HALO = 8


def _ssd_kernel(*refs, reverse):
    if reverse:
        (xbc_ref, hp_ref, hn_ref, dt_ref, cw_ref, cb_ref, dtb_ref, alog_ref,
         yf_ref, z_ref, dsk_ref, nw_ref, y_ref, ext_ref, st_ref) = refs
    else:
        (xbc_ref, hp_ref, hn_ref, dt_ref, cw_ref, cb_ref, dtb_ref, alog_ref,
         y_ref, ext_ref, st_ref) = refs
    lane_off = N_SSM_HEADS if reverse else 0
    nchunk = SSD_TILE // CHUNK

    @pl.when(pl.program_id(1) == 0)
    def _():
        st_ref[...] = jnp.zeros(st_ref.shape, F32)

    ext_ref[0:HALO, :] = hp_ref[0]
    ext_ref[HALO:HALO + SSD_TILE, :] = xbc_ref[...]
    ext_ref[HALO + SSD_TILE:HALO + SSD_TILE + HALO, :] = hn_ref[0]

    li = lax.broadcasted_iota(I32, (CHUNK, CHUNK), 0)
    si = lax.broadcasted_iota(I32, (CHUNK, CHUNK), 1)
    tri = (si >= li) if reverse else (si <= li)
    tri_f = tri.astype(F32)
    ej = lax.broadcasted_iota(I32, (LANES, SSM_WIDTH), 0)
    ec = lax.broadcasted_iota(I32, (LANES, SSM_WIDTH), 1)
    expand = (ej == lane_off + ec // SSM_HEAD_DIM).astype(BF16)
    lane = lax.broadcasted_iota(I32, (CHUNK, LANES), 1)
    lo_half = lane < SSM_HEAD_DIM
    a_row = -jnp.exp(alog_ref[...])
    gw = SSM_WIDTH // SSM_GROUPS
    heads_per_group = N_SSM_HEADS // SSM_GROUPS

    order = range(nchunk - 1, -1, -1) if reverse else range(nchunk)
    for c in order:
        base = HALO + c * CHUNK
        conv = None
        for kk in range(CONV_K):
            tap = ext_ref[pl.ds(base + kk - CONV_K // 2, CHUNK), :] * cw_ref[kk:kk + 1, :]
            conv = tap if conv is None else conv + tap
        u = conv + cb_ref[...]
        u = u / (1.0 + jnp.exp(-u))
        xs = u[:, :SSM_WIDTH]
        bm = u[:, SSM_WIDTH:SSM_WIDTH + SSM_GROUPS * D_STATE]
        cm = u[:, SSM_WIDTH + SSM_GROUPS * D_STATE:]

        dpre = dt_ref[pl.ds(c * CHUNK, CHUNK), :] + dtb_ref[...]
        dtv = jnp.maximum(dpre, 0.0) + jnp.log(1.0 + jnp.exp(-jnp.abs(dpre)))
        a = dtv * a_row
        acs = jnp.dot(tri_f, a, precision=HIGHEST, preferred_element_type=F32)
        acs_t = acs.T
        tot = acs[0:1, :] if reverse else acs[CHUNK - 1:CHUNK, :]
        stacked = jnp.concatenate([dtv, jnp.exp(acs), jnp.exp(tot - acs)], axis=0)
        s_hi = stacked.astype(BF16)
        s_lo = (stacked - s_hi.astype(F32)).astype(BF16)
        ex = (jnp.dot(s_hi, expand, preferred_element_type=F32)
              + jnp.dot(s_lo, expand, preferred_element_type=F32))
        dt_x = ex[:CHUNK]
        eacs_x = ex[CHUNK:2 * CHUNK]
        dend_x = ex[2 * CHUNK:]
        cdec_x = eacs_x[0:1, :] if reverse else eacs_x[CHUNK - 1:CHUNK, :]
        xdt = xs * dt_x
        wst = (dend_x * xdt).astype(BF16)
        st_prev = st_ref[...]
        st_b = st_prev.astype(BF16)

        ygs = []
        st_new = []
        for g in range(SSM_GROUPS):
            ys = []
            bg = bm[:, g * D_STATE:(g + 1) * D_STATE]
            cg = cm[:, g * D_STATE:(g + 1) * D_STATE].astype(BF16)
            cb = lax.dot_general(cg, bg.astype(BF16), (((1,), (1,)), ((), ())), preferred_element_type=F32)
            ms = []
            for hh in range(heads_per_group):
                h = lane_off + g * heads_per_group + hh
                diff = acs[:, h:h + 1] - acs_t[h:h + 1, :]
                lm = jnp.exp(jnp.where(tri, diff, NEG))
                ms.append((cb * lm).astype(BF16))
            for pp in range(heads_per_group // 2):
                hp = g * (heads_per_group // 2) + pp
                lhs = jnp.concatenate([ms[2 * pp], ms[2 * pp + 1]], axis=1)
                xp = xdt[:, hp * LANES:(hp + 1) * LANES]
                rhs = jnp.concatenate([jnp.where(lo_half, xp, 0.0), jnp.where(lo_half, 0.0, xp)],
                                      axis=0).astype(BF16)
                ys.append(jnp.dot(lhs, rhs, preferred_element_type=F32))
            sc = jnp.dot(bg.T.astype(BF16), wst[:, g * gw:(g + 1) * gw], preferred_element_type=F32)
            yoff = jnp.dot(cg, st_b[:, g * gw:(g + 1) * gw], preferred_element_type=F32)
            ygs.append(jnp.concatenate(ys, axis=1) + yoff * eacs_x[:, g * gw:(g + 1) * gw])
            st_new.append(st_prev[:, g * gw:(g + 1) * gw] * cdec_x[:, g * gw:(g + 1) * gw] + sc)
        st_ref[...] = jnp.concatenate(st_new, axis=1)
        y = jnp.concatenate(ygs, axis=1)

        crow = pl.ds(c * CHUNK, CHUNK)
        if reverse:
            ytot = yf_ref[crow, :] + y + dsk_ref[...] * xs
            zz = z_ref[crow, :]
            yz = ytot * (zz / (1.0 + jnp.exp(-zz)))
            yn = yz * lax.rsqrt(jnp.mean(jnp.square(yz), axis=-1, keepdims=True) + RMS_EPS) * nw_ref[...]
            y_ref[crow, :] = yn.astype(y_ref.dtype)
        else:
            y_ref[crow, :] = y


def _ssd(xbc, dt, z, conv_w, conv_b, dt_bias, a_log, d_skip, ssm_norm_w, B, seq):
    T = B * seq
    nt = seq // SSD_TILE
    x4 = xbc.reshape(B, nt, SSD_TILE, CONV_CH)
    zeros = jnp.zeros((B, 1, HALO, CONV_CH), F32)
    hprev = jnp.concatenate([zeros, x4[:, :-1, SSD_TILE - HALO:, :]], axis=1).reshape(B * nt, HALO, CONV_CH)
    hnext = jnp.concatenate([x4[:, 1:, :HALO, :], zeros], axis=1).reshape(B * nt, HALO, CONV_CH)
    cw = jnp.zeros((HALO, CONV_CH), F32).at[:CONV_K].set(conv_w)
    cb = conv_b.reshape(1, CONV_CH)
    pad = LANES - 2 * N_SSM_HEADS
    dtb = jnp.pad(dt_bias.reshape(1, 2 * N_SSM_HEADS), ((0, 0), (0, pad)))
    alog = jnp.pad(a_log.reshape(1, 2 * N_SSM_HEADS), ((0, 0), (0, pad)), constant_values=-1e30)
    dsk = jnp.repeat(d_skip, SSM_HEAD_DIM).reshape(1, SSM_WIDTH)
    nw = ssm_norm_w.reshape(1, SSM_WIDTH)

    def call(reverse, extra_in, extra_specs, out_dtype):
        if reverse:
            tmap = lambda b, i: (b * nt + nt - 1 - i, 0)
            hmap = lambda b, i: (b * nt + nt - 1 - i, 0, 0)
        else:
            tmap = lambda b, i: (b * nt + i, 0)
            hmap = lambda b, i: (b * nt + i, 0, 0)
        const = lambda b, i: (0, 0)
        in_specs = [pl.BlockSpec((SSD_TILE, CONV_CH), tmap),
                    pl.BlockSpec((1, HALO, CONV_CH), hmap),
                    pl.BlockSpec((1, HALO, CONV_CH), hmap),
                    pl.BlockSpec((SSD_TILE, LANES), tmap),
                    pl.BlockSpec((HALO, CONV_CH), const),
                    pl.BlockSpec((1, CONV_CH), const),
                    pl.BlockSpec((1, LANES), const),
                    pl.BlockSpec((1, LANES), const)] + [s(tmap, const) for s in extra_specs]
        return pl.pallas_call(
            functools.partial(_ssd_kernel, reverse=reverse),
            grid=(B, nt),
            in_specs=in_specs,
            out_specs=pl.BlockSpec((SSD_TILE, SSM_WIDTH), tmap),
            out_shape=jax.ShapeDtypeStruct((T, SSM_WIDTH), out_dtype),
            scratch_shapes=[pltpu.VMEM((SSD_TILE + 2 * HALO, CONV_CH), F32),
                            pltpu.VMEM((D_STATE, SSM_WIDTH), F32)],
            compiler_params=_params(("parallel", "arbitrary")),
            name="ssd_bwd" if reverse else "ssd_fwd",
        )(xbc, hprev, hnext, dt, cw, cb, dtb, alog, *extra_in)

    yf = call(False, [], [], F32)
    tile = lambda tmap, const: pl.BlockSpec((SSD_TILE, SSM_WIDTH), tmap)
    rowc = lambda tmap, const: pl.BlockSpec((1, SSM_WIDTH), const)
    return call(True, [yf, z, dsk, nw], [tile, tile, rowc, rowc], BF16)


def _layer_norm(h, g, b):
    mu = jnp.mean(h, axis=-1, keepdims=True)
    var = jnp.mean(jnp.square(h - mu), axis=-1, keepdims=True)
    return (h - mu) * lax.rsqrt(var + LN_EPS) * g + b


def _outproj_kernel(a_ref, y_ref, x_ref, w_ref, g_ref, b_ref, wr_ref, x1_ref, x1b_ref, lg_ref):
    mix = jnp.dot(a_ref[...], w_ref[:ATTN_WIDTH, :], preferred_element_type=F32)
    mix = mix + jnp.dot(y_ref[...], w_ref[ATTN_WIDTH:, :], preferred_element_type=F32)
    x1 = _layer_norm(ALPHA * x_ref[...] + mix, g_ref[...], b_ref[...])
    x1_ref[...] = x1
    x1b_ref[...] = x1.astype(BF16)
    lg_ref[...] = lax.dot_general(wr_ref[...], x1, (((1,), (1,)), ((), ())),
                                  precision=HIGHEST, preferred_element_type=F32)


def _outproj(attn, y, x2, w_out_b, g, b, wr_t, tm=512):
    T = x2.shape[0]
    row = lambda i: (i, 0)
    const = lambda i: (0, 0)
    return pl.pallas_call(
        _outproj_kernel,
        grid=(T // tm,),
        in_specs=[pl.BlockSpec((tm, ATTN_WIDTH), row),
                  pl.BlockSpec((tm, SSM_WIDTH), row),
                  pl.BlockSpec((tm, D_MODEL), row),
                  pl.BlockSpec((MIX_WIDTH, D_MODEL), const),
                  pl.BlockSpec((1, D_MODEL), const),
                  pl.BlockSpec((1, D_MODEL), const),
                  pl.BlockSpec((N_EXPERTS, D_MODEL), const)],
        out_specs=[pl.BlockSpec((tm, D_MODEL), row),
                   pl.BlockSpec((tm, D_MODEL), row),
                   pl.BlockSpec((N_EXPERTS, tm), lambda i: (0, i))],
        out_shape=[jax.ShapeDtypeStruct((T, D_MODEL), F32),
                   jax.ShapeDtypeStruct((T, D_MODEL), BF16),
                   jax.ShapeDtypeStruct((N_EXPERTS, T), F32)],
        compiler_params=_params(("parallel",)),
        name="outproj",
    )(attn, y, x2, w_out_b, g.reshape(1, D_MODEL), b.reshape(1, D_MODEL), wr_t)


SEL_BLK = 256


MIN_NORMAL_BITS = 0x00800000


def _select_kernel(lg_ref, aff_ref, pos_ref, off_ref, res_ref, *, cap, T):
    lg = lg_ref[...]
    ex = jnp.exp(lg - jnp.max(lg, axis=0, keepdims=True))
    aff_ref[...] = ex / jnp.sum(ex, axis=0, keepdims=True)

    def as_f32(bits):
        return lax.bitcast_convert_type(bits, F32)

    def kth_largest(ref):
        def search(i, t):
            cand = t | jnp.left_shift(jnp.int32(1), 30 - i)
            cnt = jnp.sum((ref[...] >= as_f32(cand)).astype(I32), axis=1, keepdims=True)
            return jnp.where((cnt >= cap) & (cand >= MIN_NORMAL_BITS), cand, t)

        return lax.fori_loop(0, 31, search, jnp.zeros((N_EXPERTS, 1), I32))

    thr1 = as_f32(kth_largest(aff_ref))
    res_ref[...] = aff_ref[...] - thr1
    thr2_bits = kth_largest(res_ref)
    thr2 = as_f32(thr2_bits)
    nxt2 = as_f32(jnp.where(thr2_bits == 0, MIN_NORMAL_BITS, thr2_bits + 1))
    n_gt = jnp.sum((res_ref[...] >= nxt2).astype(I32), axis=1, keepdims=True)
    need = (cap - n_gt).astype(F32)

    uj = lax.broadcasted_iota(I32, (SEL_BLK, SEL_BLK), 0)
    ut = lax.broadcasted_iota(I32, (SEL_BLK, SEL_BLK), 1)
    upper = (uj <= ut).astype(BF16)
    nblk = T // SEL_BLK
    olane = lax.broadcasted_iota(I32, off_ref.shape, 1)

    def blk(i, carry):
        c_gt, c_eq, offs = carry
        cols = pl.ds(pl.multiple_of(i * SEL_BLK, SEL_BLK), SEL_BLK)
        res = res_ref[:, cols]
        gt = res >= nxt2
        eq = (res >= thr2) & jnp.logical_not(gt)
        gt_f = gt.astype(F32)
        eq_f = eq.astype(F32)
        st = jnp.concatenate([gt_f, eq_f], axis=0).astype(BF16)
        cs = jnp.dot(st, upper, preferred_element_type=F32)
        gt_ex = c_gt + cs[:N_EXPERTS] - gt_f
        eq_ex = c_eq + cs[N_EXPERTS:] - eq_f
        sel = gt | (eq & (eq_ex < need))
        slot = gt_ex + jnp.minimum(eq_ex, need)
        pos_ref[:, cols] = jnp.where(sel, slot, -1.0).astype(I32)
        start = (c_gt + jnp.minimum(c_eq, need)).astype(I32)
        offs = jnp.where(olane == i, start, offs)
        return (c_gt + cs[:N_EXPERTS, SEL_BLK - 1:SEL_BLK], c_eq + cs[N_EXPERTS:, SEL_BLK - 1:SEL_BLK], offs)

    zero = jnp.zeros((N_EXPERTS, 1), F32)
    _, _, offs = lax.fori_loop(0, nblk, blk, (zero, zero, jnp.zeros(off_ref.shape, I32)))
    off_ref[...] = offs


def _select(lg_t, cap):
    T = lg_t.shape[1]
    nblk = T // SEL_BLK
    owidth = -(-nblk // LANES) * LANES
    return pl.pallas_call(
        functools.partial(_select_kernel, cap=cap, T=T),
        out_shape=[jax.ShapeDtypeStruct((N_EXPERTS, T), F32),
                   jax.ShapeDtypeStruct((N_EXPERTS, T), I32),
                   jax.ShapeDtypeStruct((N_EXPERTS, owidth), I32)],
        scratch_shapes=[pltpu.VMEM((N_EXPERTS, T), F32)],
        compiler_params=pltpu.CompilerParams(vmem_limit_bytes=VMEM_LIMIT),
        name="select",
    )(lg_t)


SLOT_TILE = 256
MOE_TOK = 512
FF_BLK = 256


def _moe_kernel(off_ref, x_ref, pos_ref, wg_ref, wu_ref, wd_ref, out_ref, xg_ref, xb_ref, acc_ref, *, n_tok_tiles):
    e = pl.program_id(0)
    i = pl.program_id(1)

    @pl.when(i == 0)
    def _():
        xg_ref[...] = jnp.zeros(xg_ref.shape, F32)

    n0 = off_ref[e * (n_tok_tiles + 1) + i]
    n1 = off_ref[e * (n_tok_tiles + 1) + i + 1]
    j0 = n0 // SLOT_TILE
    n_tiles = jnp.where(n1 > n0, (n1 - 1) // SLOT_TILE - j0 + 1, 0)
    pos_row = pos_ref[pl.ds(e, 1), :]
    sub = lax.broadcasted_iota(I32, (SLOT_TILE, MOE_TOK), 0)

    def tile(jj, carry):
        j = j0 + jj
        onehot = jnp.where(pos_row - j * SLOT_TILE == sub, 1.0, 0.0).astype(BF16)
        xg_ref[...] += jnp.dot(onehot, x_ref[...], preferred_element_type=F32)

        @pl.when(n1 >= (j + 1) * SLOT_TILE)
        def _():
            xb_ref[...] = xg_ref[...].astype(BF16)
            xg_ref[...] = jnp.zeros(xg_ref.shape, F32)
            acc_ref[...] = jnp.zeros(acc_ref.shape, F32)

            def ff(c, carry2):
                cols = pl.ds(pl.multiple_of(c * FF_BLK, FF_BLK), FF_BLK)
                xb = xb_ref[...]
                g = jnp.dot(xb, wg_ref[0, :, cols], preferred_element_type=F32)
                u = jnp.dot(xb, wu_ref[0, :, cols], preferred_element_type=F32)
                h = ((g / (1.0 + jnp.exp(-g))) * u).astype(BF16)
                acc_ref[...] += jnp.dot(h, wd_ref[0, cols, :], preferred_element_type=F32)
                return carry2

            lax.fori_loop(0, D_FF_EXPERT // FF_BLK, ff, 0)
            out_ref[0, pl.ds(pl.multiple_of(j * SLOT_TILE, SLOT_TILE), SLOT_TILE), :] = acc_ref[...].astype(out_ref.dtype)

        return carry

    lax.fori_loop(0, n_tiles, tile, 0)


def _moe(x1b, pos, off_flat, wg, wu, wd, cap):
    T = x1b.shape[0]
    nt = T // MOE_TOK
    grid_spec = pltpu.PrefetchScalarGridSpec(
        num_scalar_prefetch=1,
        grid=(N_EXPERTS, nt),
        in_specs=[pl.BlockSpec((MOE_TOK, D_MODEL), lambda e, i, off: (i, 0)),
                  pl.BlockSpec((N_EXPERTS, MOE_TOK), lambda e, i, off: (0, i)),
                  pl.BlockSpec((1, D_MODEL, D_FF_EXPERT), lambda e, i, off: (e, 0, 0), pipeline_mode=pl.Buffered(1)),
                  pl.BlockSpec((1, D_MODEL, D_FF_EXPERT), lambda e, i, off: (e, 0, 0), pipeline_mode=pl.Buffered(1)),
                  pl.BlockSpec((1, D_FF_EXPERT, D_MODEL), lambda e, i, off: (e, 0, 0), pipeline_mode=pl.Buffered(1))],
        out_specs=pl.BlockSpec((1, cap, D_MODEL), lambda e, i, off: (e, 0, 0)),
        scratch_shapes=[pltpu.VMEM((SLOT_TILE, D_MODEL), F32),
                        pltpu.VMEM((SLOT_TILE, D_MODEL), BF16),
                        pltpu.VMEM((SLOT_TILE, D_MODEL), F32)])
    return pl.pallas_call(
        functools.partial(_moe_kernel, n_tok_tiles=nt),
        grid_spec=grid_spec,
        out_shape=jax.ShapeDtypeStruct((N_EXPERTS, cap, D_MODEL), BF16),
        compiler_params=_params(("arbitrary", "arbitrary")),
        name="moe",
    )(off_flat, x1b, pos, wg, wu, wd)


CMB_TOK = 256
ROW_CHUNK = 16
STAGE_ROWS = N_EXPERTS * (CMB_TOK + 2 * ROW_CHUNK)
NO_ROW = -(2 ** 30)


def _combine_kernel(off_ref, x1_ref, gid_ref, gate_ref, g_ref, b_ref, eo_ref, o_ref, stage_ref, w_ref, sem,
                    *, cap, n_tok_tiles):
    i = pl.program_id(0)
    cur = i % 2

    def chunk_copy(src_row, dst_row, buf):
        return pltpu.make_async_copy(eo_ref.at[pl.ds(src_row, ROW_CHUNK)],
                                     stage_ref.at[buf, pl.ds(dst_row, ROW_CHUNK)], sem.at[buf])

    def segments(tile):
        base = jnp.int32(0)
        segs = []
        for e in range(N_EXPERTS):
            n0 = off_ref[e * (n_tok_tiles + 1) + tile]
            n1 = off_ref[e * (n_tok_tiles + 1) + tile + 1]
            nch = (n1 + ROW_CHUNK - 1) // ROW_CHUNK - n0 // ROW_CHUNK
            segs.append((base, nch, e * cap + (n0 // ROW_CHUNK) * ROW_CHUNK))
            base = base + nch * ROW_CHUNK
        return segs, base

    def issue(tile, buf):
        segs, _ = segments(tile)
        for seg_base, nch, seg_row in segs:
            def one(c, carry, seg_base=seg_base, seg_row=seg_row):
                src = pl.multiple_of(seg_row + c * ROW_CHUNK, ROW_CHUNK)
                dst = pl.multiple_of(seg_base + c * ROW_CHUNK, ROW_CHUNK)
                chunk_copy(src, dst, buf).start()
                return carry

            lax.fori_loop(0, nch, one, 0)

    @pl.when(i == 0)
    def _():
        stage_ref[...] = jnp.zeros(stage_ref.shape, stage_ref.dtype)
        issue(0, 0)

    @pl.when(i + 1 < n_tok_tiles)
    def _():
        issue(i + 1, 1 - cur)

    segs, total = segments(i)

    def wait(c, carry):
        chunk_copy(0, 0, cur).wait()
        return carry

    lax.fori_loop(0, total // ROW_CHUNK, wait, 0)

    jsub = lax.broadcasted_iota(I32, (SLOT_TILE, CMB_TOK), 0)
    o_ref[...] = ALPHA * x1_ref[...]

    def kchunk(kc, carry):
        j0 = kc * SLOT_TILE
        w_ref[...] = jnp.zeros(w_ref.shape, F32)
        for e, (seg_base, nch, seg_row) in enumerate(segs):
            seg_end = seg_base + nch * ROW_CHUNK

            @pl.when((seg_base < j0 + SLOT_TILE) & (seg_end > j0))
            def _(e=e, seg_base=seg_base, seg_row=seg_row):
                row_id = jsub + (j0 + seg_row - seg_base)
                w_ref[...] = jnp.where(gid_ref[e:e + 1, :] == row_id, gate_ref[e:e + 1, :], w_ref[...])

        rows = stage_ref[cur, pl.ds(pl.multiple_of(j0, SLOT_TILE), SLOT_TILE), :]
        o_ref[...] += lax.dot_general(w_ref[...].astype(BF16), rows, (((0,), (0,)), ((), ())),
                                      preferred_element_type=F32)
        return carry

    lax.fori_loop(0, (total + SLOT_TILE - 1) // SLOT_TILE, kchunk, 0)
    o_ref[...] = _layer_norm(o_ref[...], g_ref[...], b_ref[...])


def _combine(x1, gid_tok, gate_tok, off_flat, eo_flat, g, b, cap):
    T = x1.shape[0]
    nt = T // CMB_TOK
    stage_rows = -(-STAGE_ROWS // SLOT_TILE) * SLOT_TILE
    row = lambda i, off: (i, 0)
    const = lambda i, off: (0, 0)
    grid_spec = pltpu.PrefetchScalarGridSpec(
        num_scalar_prefetch=1,
        grid=(nt,),
        in_specs=[pl.BlockSpec((CMB_TOK, D_MODEL), row),
                  pl.BlockSpec((N_EXPERTS, CMB_TOK), lambda i, off: (0, i)),
                  pl.BlockSpec((N_EXPERTS, CMB_TOK), lambda i, off: (0, i)),
                  pl.BlockSpec((1, D_MODEL), const),
                  pl.BlockSpec((1, D_MODEL), const),
                  pl.BlockSpec(memory_space=pl.ANY)],
        out_specs=pl.BlockSpec((CMB_TOK, D_MODEL), row),
        scratch_shapes=[pltpu.VMEM((2, stage_rows, D_MODEL), BF16),
                        pltpu.VMEM((SLOT_TILE, CMB_TOK), F32),
                        pltpu.SemaphoreType.DMA((2,))])
    return pl.pallas_call(
        functools.partial(_combine_kernel, cap=cap, n_tok_tiles=nt),
        grid_spec=grid_spec,
        out_shape=jax.ShapeDtypeStruct((T, D_MODEL), F32),
        compiler_params=_params(("arbitrary",)),
        name="combine",
    )(off_flat, x1, gid_tok, gate_tok, g.reshape(1, D_MODEL), b.reshape(1, D_MODEL), eo_flat)


def _tile_offsets(off256, cap, T, tok):
    step = tok // SEL_BLK
    o = off256[:, :T // SEL_BLK:step]
    o = jnp.concatenate([o, jnp.full((N_EXPERTS, 1), cap, I32)], axis=1)
    return o.reshape(-1)


def _trunk(x, w_in_p, conv_w, conv_b, dt_bias, a_log, d_skip, ssm_norm_w, w_out_b, ln1_g, ln1_b,
           wr_t, wg, wu, wd, ln2_g, ln2_b):
    B, seq, _ = x.shape
    T = B * seq
    cap = CAPACITY_FACTOR * T // N_EXPERTS
    x2 = x.reshape(T, D_MODEL)
    q, k, v, z, xbc, dt = _inproj(x2, w_in_p, seq)
    shp = (B, seq, ATTN_WIDTH)
    attn = _attention(q.reshape(shp), k.reshape(shp), v.reshape(shp)).reshape(T, ATTN_WIDTH)
    y = _ssd(xbc, dt, z, conv_w, conv_b, dt_bias, a_log, d_skip, ssm_norm_w, B, seq)
    x1, x1b, lg_t = _outproj(attn, y, x2, w_out_b, ln1_g, ln1_b, wr_t)
    aff_t, pos, off256 = _select(lg_t, cap)
    eo = _moe(x1b, pos, _tile_offsets(off256, cap, T, MOE_TOK), wg, wu, wd, cap)
    gid = jnp.where(pos >= 0, pos + jnp.arange(N_EXPERTS, dtype=I32)[:, None] * cap, NO_ROW)
    out = _combine(x1, gid, aff_t, _tile_offsets(off256, cap, T, CMB_TOK),
                   eo.reshape(N_EXPERTS * cap, D_MODEL), ln2_g, ln2_b, cap)
    return out.reshape(B, seq, D_MODEL)


def kernel(x_prompt, x_sample, w_in, conv_w, conv_b, dt_bias, a_log, d_skip, ssm_norm_w, w_out, ln1_g, ln1_b,
           w_router, w_gate, w_up, w_down, ln2_g, ln2_b):
    assert DEPTH == 1
    l = 0
    w_in_p = jnp.pad(w_in[l], ((0, 0), (0, IN_PAD - w_in.shape[-1]))).astype(BF16)
    args = (w_in_p, conv_w[l], conv_b[l], dt_bias[l], a_log[l], d_skip[l], ssm_norm_w[l],
            w_out[l].astype(BF16), ln1_g[l], ln1_b[l], w_router[l].T,
            w_gate[l].astype(BF16), w_up[l].astype(BF16), w_down[l].astype(BF16), ln2_g[l], ln2_b[l])
    return (_trunk(x_prompt, *args), _trunk(x_sample, *args))
```

```python
import functools

import jax
import jax.numpy as jnp
from jax import lax
from jax.experimental import pallas as pl
from jax.experimental.pallas import tpu as pltpu

F32 = jnp.float32
BF16 = jnp.bfloat16
I32 = jnp.int32
HIGHEST = lax.Precision.HIGHEST

D_MODEL = 1024
DEPTH = 1
HEAD_DIM = 64
N_ATTN_HEADS = 8
ATTN_WIDTH = N_ATTN_HEADS * HEAD_DIM
ROT_DIM = HEAD_DIM // 4
ROPE_THETA = 500000.0
DILATED_PATTERNS = ((128, 1), (512, 4), (2048, 16))
N_SSM_HEADS = 8
SSM_HEAD_DIM = 64
SSM_WIDTH = N_SSM_HEADS * SSM_HEAD_DIM
SSM_GROUPS = 2
D_STATE = 128
CONV_K = 5
CHUNK = 128
CONV_CH = SSM_WIDTH + 2 * SSM_GROUPS * D_STATE
MIX_WIDTH = ATTN_WIDTH + SSM_WIDTH
N_EXPERTS = 16
CAPACITY_FACTOR = 2
D_FF_EXPERT = 2816
ALPHA = (2.0 * DEPTH) ** 0.25
LN_EPS = 1e-5
RMS_EPS = 1e-5

LANES = 128
VMEM_LIMIT = 56 * 1024 * 1024
NEG = -1e30
HALF_WIN = 64

IN_PAD = 3 * ATTN_WIDTH + SSM_WIDTH + CONV_CH + LANES


def _params(sem):
    return pltpu.CompilerParams(dimension_semantics=sem, vmem_limit_bytes=VMEM_LIMIT)


def _inproj_kernel(x_ref, w_ref, c_ref, sa_ref, sb_ref, q_ref, k_ref, v_ref, z_ref, xbc_ref, dt_ref):
    x = x_ref[...].astype(BF16)

    def seg(lo, hi):
        return jnp.dot(x, w_ref[:, lo:hi], preferred_element_type=F32)

    c = c_ref[...]
    sa = sa_ref[...]
    sb = sb_ref[...]

    def rotary(t, scale):
        outs = []
        for g in range(ATTN_WIDTH // LANES):
            tg = t[:, g * LANES:(g + 1) * LANES]
            up = pltpu.roll(tg, LANES - ROT_DIM // 2, 1)
            dn = pltpu.roll(tg, ROT_DIM // 2, 1)
            outs.append((tg * c + up * sa + dn * sb) * scale)
        return jnp.concatenate(outs, axis=1)

    a = ATTN_WIDTH
    q_ref[...] = rotary(seg(0, a), HEAD_DIM ** -0.5)
    k_ref[...] = rotary(seg(a, 2 * a), 1.0)
    v_ref[...] = seg(2 * a, 3 * a)
    z_ref[...] = seg(3 * a, 3 * a + SSM_WIDTH)
    o = 3 * a + SSM_WIDTH
    xbc_ref[...] = seg(o, o + CONV_CH)
    dt_ref[...] = seg(o + CONV_CH, o + CONV_CH + LANES)


def _rotary_tables(seq):
    half = ROT_DIM // 2
    inv = ROPE_THETA ** (-jnp.arange(half, dtype=F32) * 2.0 / ROT_DIM)
    ang = jnp.arange(seq).astype(F32)[:, None] * inv[None, :]
    cos = jnp.cos(ang)
    sin = jnp.sin(ang)
    m = jnp.arange(LANES) % HEAD_DIM
    c = jnp.where(m[None, :] < ROT_DIM, cos[:, m % half], 1.0)
    sa = jnp.where(m[None, :] < half, -sin[:, m % half], 0.0)
    sb = jnp.where((m[None, :] >= half) & (m[None, :] < ROT_DIM), sin[:, m % half], 0.0)
    return c.astype(F32), sa.astype(F32), sb.astype(F32)


def _inproj(x2, w_pad, seq, tm=512):
    T = x2.shape[0]
    c, sa, sb = _rotary_tables(seq)
    nseq = seq // tm
    row = lambda i: (i, 0)
    tab = lambda i: (i % nseq, 0)
    outs = [jax.ShapeDtypeStruct((T, ATTN_WIDTH), F32)] * 3 + [
        jax.ShapeDtypeStruct((T, SSM_WIDTH), F32),
        jax.ShapeDtypeStruct((T, CONV_CH), F32),
        jax.ShapeDtypeStruct((T, LANES), F32)]
    return pl.pallas_call(
        _inproj_kernel,
        grid=(T // tm,),
        in_specs=[pl.BlockSpec((tm, D_MODEL), row),
                  pl.BlockSpec((D_MODEL, IN_PAD), lambda i: (0, 0)),
                  pl.BlockSpec((tm, LANES), tab),
                  pl.BlockSpec((tm, LANES), tab),
                  pl.BlockSpec((tm, LANES), tab)],
        out_specs=[pl.BlockSpec((tm, ATTN_WIDTH), row)] * 3 + [
            pl.BlockSpec((tm, SSM_WIDTH), row),
            pl.BlockSpec((tm, CONV_CH), row),
            pl.BlockSpec((tm, LANES), row)],
        out_shape=outs,
        compiler_params=_params(("parallel",)),
        name="inproj",
    )(x2, w_pad, c, sa, sb)


QB = 128
KB = 256
ATTN_GROUP = 4


def _attn_kernel(q_ref, k_ref, v_ref, o_ref, m_ref, l_ref, acc_ref, bias_ref, *, seq):
    row = lax.broadcasted_iota(I32, (2 * QB, KB), 0)
    col = lax.broadcasted_iota(I32, (2 * QB, KB), 1)
    iq = jnp.where(row >= QB, row - QB, row)
    for di in range(3):
        bias_ref[di] = jnp.where(jnp.abs(di * HALF_WIN + iq - col) <= HALF_WIN, 0.0, NEG).astype(F32)
    m_ref[...] = jnp.full(m_ref.shape, NEG, F32)
    l_ref[...] = jnp.zeros(l_ref.shape, F32)
    acc_ref[...] = jnp.zeros(acc_ref.shape, F32)

    lane = lax.broadcasted_iota(I32, (QB, LANES), 1)
    head0 = lane < HEAD_DIM

    for window, dil in DILATED_PATTERNS:
        assert window // (2 * dil) == HALF_WIN
        sub_len = seq // dil
        nblk = sub_len // QB
        assert sub_len >= KB and sub_len % QB == 0

        def rows(start, size, dil=dil):
            return pl.ds(start, size) if dil == 1 else pl.ds(start, size, stride=dil)

        def body(gi, carry, dil=dil, sub_len=sub_len, nblk=nblk, rows=rows):
            loaded = []
            for u in range(ATTN_GROUP):
                idx = gi * ATTN_GROUP + u
                r = idx // nblk
                n = idx - r * nblk
                kb = jnp.clip(n * QB - HALF_WIN, 0, sub_len - KB)
                di = (n * QB - kb) // HALF_WIN
                qrows = rows(r + dil * QB * n, QB)
                krows = rows(r + dil * kb, KB)
                loaded.append((qrows, q_ref[0, qrows, :], k_ref[0, krows, :], v_ref[0, krows, :], bias_ref[di],
                               m_ref[0, qrows, :], m_ref[1, qrows, :], l_ref[0, qrows, :], l_ref[1, qrows, :],
                               acc_ref[qrows, :]))
            results = []
            for qrows, q, k, v, bias, m0, m1, l0, l1, acc in loaded:
                qs = jnp.concatenate([jnp.where(head0, q, 0.0), jnp.where(head0, 0.0, q)], axis=0).astype(BF16)
                s = lax.dot_general(qs, k.astype(BF16), (((1,), (1,)), ((), ())), preferred_element_type=F32)
                s = s + bias
                m_prev = jnp.concatenate([m0, m1], axis=0)
                l_prev = jnp.concatenate([l0, l1], axis=0)
                m_new = jnp.maximum(m_prev, jnp.max(s, axis=1, keepdims=True))
                alpha = jnp.exp(m_prev - m_new)
                p = jnp.exp(s - jnp.concatenate([m_new, m_new], axis=1))
                l_new = alpha * l_prev + jnp.sum(p, axis=1, keepdims=True)
                pv = jnp.dot(p.astype(BF16), v.astype(BF16), preferred_element_type=F32)
                a_l = jnp.where(head0, alpha[:QB], alpha[QB:])
                pv_l = jnp.where(head0, pv[:QB], pv[QB:])
                results.append((qrows, a_l * acc + pv_l, m_new, l_new))
            for qrows, acc_new, m_new, l_new in results:
                acc_ref[qrows, :] = acc_new
                m_ref[0, qrows, :] = m_new[:QB]
                m_ref[1, qrows, :] = m_new[QB:]
                l_ref[0, qrows, :] = l_new[:QB]
                l_ref[1, qrows, :] = l_new[QB:]
            return carry

        assert (dil * nblk) % ATTN_GROUP == 0
        lax.fori_loop(0, dil * nblk // ATTN_GROUP, body, 0)

    def fin(i, carry):
        rws = pl.ds(pl.multiple_of(i * QB, QB), QB)
        den = jnp.where(head0, l_ref[0, rws, :], l_ref[1, rws, :])
        o_ref[0, rws, :] = (acc_ref[rws, :] / den).astype(o_ref.dtype)
        return carry

    lax.fori_loop(0, seq // QB, fin, 0)


def _attention(q, k, v):
    B, seq, _ = q.shape
    npair = ATTN_WIDTH // LANES
    spec = pl.BlockSpec((1, seq, LANES), lambda b, h: (b, 0, h))
    return pl.pallas_call(
        functools.partial(_attn_kernel, seq=seq),
        grid=(B, npair),
        in_specs=[spec, spec, spec],
        out_specs=spec,
        out_shape=jax.ShapeDtypeStruct((B, seq, ATTN_WIDTH), BF16),
        scratch_shapes=[pltpu.VMEM((2, seq, LANES), F32),
                        pltpu.VMEM((2, seq, LANES), F32),
                        pltpu.VMEM((seq, LANES), F32),
                        pltpu.VMEM((3, 2 * QB, KB), F32)],
        compiler_params=_params(("parallel", "parallel")),
        name="attn",
    )(q, k, v)


SSD_TILE = 256
HALO = 8


def _ssd_kernel(*refs, reverse):
    if reverse:
        (xbc_ref, hp_ref, hn_ref, dt_ref, cw_ref, cb_ref, dtb_ref, alog_ref,
         yf_ref, z_ref, dsk_ref, nw_ref, y_ref, ext_ref, st_ref) = refs
    else:
        (xbc_ref, hp_ref, hn_ref, dt_ref, cw_ref, cb_ref, dtb_ref, alog_ref,
         y_ref, ext_ref, st_ref) = refs
    lane_off = N_SSM_HEADS if reverse else 0
    nchunk = SSD_TILE // CHUNK

    @pl.when(pl.program_id(1) == 0)
    def _():
        st_ref[...] = jnp.zeros(st_ref.shape, F32)

    ext_ref[0:HALO, :] = hp_ref[0]
    ext_ref[HALO:HALO + SSD_TILE, :] = xbc_ref[...]
    ext_ref[HALO + SSD_TILE:HALO + SSD_TILE + HALO, :] = hn_ref[0]

    li = lax.broadcasted_iota(I32, (CHUNK, CHUNK), 0)
    si = lax.broadcasted_iota(I32, (CHUNK, CHUNK), 1)
    tri = (si >= li) if reverse else (si <= li)
    tri_f = tri.astype(F32)
    ej = lax.broadcasted_iota(I32, (LANES, SSM_WIDTH), 0)
    ec = lax.broadcasted_iota(I32, (LANES, SSM_WIDTH), 1)
    expand = (ej == lane_off + ec // SSM_HEAD_DIM).astype(BF16)
    lane = lax.broadcasted_iota(I32, (CHUNK, LANES), 1)
    lo_half = lane < SSM_HEAD_DIM
    a_row = -jnp.exp(alog_ref[...])
    gw = SSM_WIDTH // SSM_GROUPS
    heads_per_group = N_SSM_HEADS // SSM_GROUPS

    order = range(nchunk - 1, -1, -1) if reverse else range(nchunk)
    for c in order:
        base = HALO + c * CHUNK
        conv = None
        for kk in range(CONV_K):
            tap = ext_ref[pl.ds(base + kk - CONV_K // 2, CHUNK), :] * cw_ref[kk:kk + 1, :]
            conv = tap if conv is None else conv + tap
        u = conv + cb_ref[...]
        u = u / (1.0 + jnp.exp(-u))
        xs = u[:, :SSM_WIDTH]
        bm = u[:, SSM_WIDTH:SSM_WIDTH + SSM_GROUPS * D_STATE]
        cm = u[:, SSM_WIDTH + SSM_GROUPS * D_STATE:]

        dpre = dt_ref[pl.ds(c * CHUNK, CHUNK), :] + dtb_ref[...]
        dtv = jnp.maximum(dpre, 0.0) + jnp.log(1.0 + jnp.exp(-jnp.abs(dpre)))
        a = dtv * a_row
        acs = jnp.dot(tri_f, a, precision=HIGHEST, preferred_element_type=F32)
        acs_t = acs.T
        tot = acs[0:1, :] if reverse else acs[CHUNK - 1:CHUNK, :]
        stacked = jnp.concatenate([dtv, jnp.exp(acs), jnp.exp(tot - acs)], axis=0)
        s_hi = stacked.astype(BF16)
        s_lo = (stacked - s_hi.astype(F32)).astype(BF16)
        ex = (jnp.dot(s_hi, expand, preferred_element_type=F32)
              + jnp.dot(s_lo, expand, preferred_element_type=F32))
        dt_x = ex[:CHUNK]
        eacs_x = ex[CHUNK:2 * CHUNK]
        dend_x = ex[2 * CHUNK:]
        cdec_x = eacs_x[0:1, :] if reverse else eacs_x[CHUNK - 1:CHUNK, :]
        xdt = xs * dt_x
        wst = (dend_x * xdt).astype(BF16)
        st_prev = st_ref[...]
        st_b = st_prev.astype(BF16)

        ygs = []
        st_new = []
        for g in range(SSM_GROUPS):
            ys = []
            bg = bm[:, g * D_STATE:(g + 1) * D_STATE]
            cg = cm[:, g * D_STATE:(g + 1) * D_STATE].astype(BF16)
            cb = lax.dot_general(cg, bg.astype(BF16), (((1,), (1,)), ((), ())), preferred_element_type=F32)
            ms = []
            for hh in range(heads_per_group):
                h = lane_off + g * heads_per_group + hh
                diff = acs[:, h:h + 1] - acs_t[h:h + 1, :]
                lm = jnp.exp(jnp.where(tri, diff, NEG))
                ms.append((cb * lm).astype(BF16))
            for pp in range(heads_per_group // 2):
                hp = g * (heads_per_group // 2) + pp
                lhs = jnp.concatenate([ms[2 * pp], ms[2 * pp + 1]], axis=1)
                xp = xdt[:, hp * LANES:(hp + 1) * LANES]
                rhs = jnp.concatenate([jnp.where(lo_half, xp, 0.0), jnp.where(lo_half, 0.0, xp)],
                                      axis=0).astype(BF16)
                ys.append(jnp.dot(lhs, rhs, preferred_element_type=F32))
            sc = jnp.dot(bg.T.astype(BF16), wst[:, g * gw:(g + 1) * gw], preferred_element_type=F32)
            yoff = jnp.dot(cg, st_b[:, g * gw:(g + 1) * gw], preferred_element_type=F32)
            ygs.append(jnp.concatenate(ys, axis=1) + yoff * eacs_x[:, g * gw:(g + 1) * gw])
            st_new.append(st_prev[:, g * gw:(g + 1) * gw] * cdec_x[:, g * gw:(g + 1) * gw] + sc)
        st_ref[...] = jnp.concatenate(st_new, axis=1)
        y = jnp.concatenate(ygs, axis=1)

        crow = pl.ds(c * CHUNK, CHUNK)
        if reverse:
            ytot = yf_ref[crow, :] + y + dsk_ref[...] * xs
            zz = z_ref[crow, :]
            yz = ytot * (zz / (1.0 + jnp.exp(-zz)))
            yn = yz * lax.rsqrt(jnp.mean(jnp.square(yz), axis=-1, keepdims=True) + RMS_EPS) * nw_ref[...]
            y_ref[crow, :] = yn.astype(y_ref.dtype)
        else:
            y_ref[crow, :] = y


def _ssd(xbc, dt, z, conv_w, conv_b, dt_bias, a_log, d_skip, ssm_norm_w, B, seq):
    T = B * seq
    nt = seq // SSD_TILE
    x4 = xbc.reshape(B, nt, SSD_TILE, CONV_CH)
    zeros = jnp.zeros((B, 1, HALO, CONV_CH), F32)
    hprev = jnp.concatenate([zeros, x4[:, :-1, SSD_TILE - HALO:, :]], axis=1).reshape(B * nt, HALO, CONV_CH)
    hnext = jnp.concatenate([x4[:, 1:, :HALO, :], zeros], axis=1).reshape(B * nt, HALO, CONV_CH)
    cw = jnp.zeros((HALO, CONV_CH), F32).at[:CONV_K].set(conv_w)
    cb = conv_b.reshape(1, CONV_CH)
    pad = LANES - 2 * N_SSM_HEADS
    dtb = jnp.pad(dt_bias.reshape(1, 2 * N_SSM_HEADS), ((0, 0), (0, pad)))
    alog = jnp.pad(a_log.reshape(1, 2 * N_SSM_HEADS), ((0, 0), (0, pad)), constant_values=-1e30)
    dsk = jnp.repeat(d_skip, SSM_HEAD_DIM).reshape(1, SSM_WIDTH)
    nw = ssm_norm_w.reshape(1, SSM_WIDTH)

    def call(reverse, extra_in, extra_specs, out_dtype):
        if reverse:
            tmap = lambda b, i: (b * nt + nt - 1 - i, 0)
            hmap = lambda b, i: (b * nt + nt - 1 - i, 0, 0)
        else:
            tmap = lambda b, i: (b * nt + i, 0)
            hmap = lambda b, i: (b * nt + i, 0, 0)
        const = lambda b, i: (0, 0)
        in_specs = [pl.BlockSpec((SSD_TILE, CONV_CH), tmap),
                    pl.BlockSpec((1, HALO, CONV_CH), hmap),
                    pl.BlockSpec((1, HALO, CONV_CH), hmap),
                    pl.BlockSpec((SSD_TILE, LANES), tmap),
                    pl.BlockSpec((HALO, CONV_CH), const),
                    pl.BlockSpec((1, CONV_CH), const),
                    pl.BlockSpec((1, LANES), const),
                    pl.BlockSpec((1, LANES), const)] + [s(tmap, const) for s in extra_specs]
        return pl.pallas_call(
            functools.partial(_ssd_kernel, reverse=reverse),
            grid=(B, nt),
            in_specs=in_specs,
            out_specs=pl.BlockSpec((SSD_TILE, SSM_WIDTH), tmap),
            out_shape=jax.ShapeDtypeStruct((T, SSM_WIDTH), out_dtype),
            scratch_shapes=[pltpu.VMEM((SSD_TILE + 2 * HALO, CONV_CH), F32),
                            pltpu.VMEM((D_STATE, SSM_WIDTH), F32)],
            compiler_params=_params(("parallel", "arbitrary")),
            name="ssd_bwd" if reverse else "ssd_fwd",
        )(xbc, hprev, hnext, dt, cw, cb, dtb, alog, *extra_in)

    yf = call(False, [], [], F32)
    tile = lambda tmap, const: pl.BlockSpec((SSD_TILE, SSM_WIDTH), tmap)
    rowc = lambda tmap, const: pl.BlockSpec((1, SSM_WIDTH), const)
    return call(True, [yf, z, dsk, nw], [tile, tile, rowc, rowc], BF16)


def _layer_norm(h, g, b):
    mu = jnp.mean(h, axis=-1, keepdims=True)
    var = jnp.mean(jnp.square(h - mu), axis=-1, keepdims=True)
    return (h - mu) * lax.rsqrt(var + LN_EPS) * g + b


def _outproj_kernel(a_ref, y_ref, x_ref, w_ref, g_ref, b_ref, wr_ref, x1_ref, lg_ref):
    mix = jnp.dot(a_ref[...], w_ref[:ATTN_WIDTH, :], preferred_element_type=F32)
    mix = mix + jnp.dot(y_ref[...], w_ref[ATTN_WIDTH:, :], preferred_element_type=F32)
    x1 = _layer_norm(ALPHA * x_ref[...] + mix, g_ref[...], b_ref[...])
    x1_ref[...] = x1
    lg_ref[...] = lax.dot_general(wr_ref[...], x1, (((1,), (1,)), ((), ())),
                                  precision=HIGHEST, preferred_element_type=F32)


def _outproj(attn, y, x2, w_out_b, g, b, wr_t, tm=512):
    T = x2.shape[0]
    row = lambda i: (i, 0)
    const = lambda i: (0, 0)
    return pl.pallas_call(
        _outproj_kernel,
        grid=(T // tm,),
        in_specs=[pl.BlockSpec((tm, ATTN_WIDTH), row),
                  pl.BlockSpec((tm, SSM_WIDTH), row),
                  pl.BlockSpec((tm, D_MODEL), row),
                  pl.BlockSpec((MIX_WIDTH, D_MODEL), const),
                  pl.BlockSpec((1, D_MODEL), const),
                  pl.BlockSpec((1, D_MODEL), const),
                  pl.BlockSpec((N_EXPERTS, D_MODEL), const)],
        out_specs=[pl.BlockSpec((tm, D_MODEL), row),
                   pl.BlockSpec((N_EXPERTS, tm), lambda i: (0, i))],
        out_shape=[jax.ShapeDtypeStruct((T, D_MODEL), F32),
                   jax.ShapeDtypeStruct((N_EXPERTS, T), F32)],
        compiler_params=_params(("parallel",)),
        name="outproj",
    )(attn, y, x2, w_out_b, g.reshape(1, D_MODEL), b.reshape(1, D_MODEL), wr_t)


SEL_BLK = 256


MIN_NORMAL_BITS = 0x00800000


def _select_kernel(lg_ref, aff_ref, pos_ref, off_ref, res_ref, *, cap, T):
    lg = lg_ref[...]
    ex = jnp.exp(lg - jnp.max(lg, axis=0, keepdims=True))
    aff_ref[...] = ex / jnp.sum(ex, axis=0, keepdims=True)

    def as_f32(bits):
        return lax.bitcast_convert_type(bits, F32)

    def kth_largest(ref):
        def search(i, t):
            cand = t | jnp.left_shift(jnp.int32(1), 30 - i)
            cnt = jnp.sum((ref[...] >= as_f32(cand)).astype(I32), axis=1, keepdims=True)
            return jnp.where((cnt >= cap) & (cand >= MIN_NORMAL_BITS), cand, t)

        return lax.fori_loop(0, 31, search, jnp.zeros((N_EXPERTS, 1), I32))

    thr1 = as_f32(kth_largest(aff_ref))
    res_ref[...] = aff_ref[...] - thr1
    thr2_bits = kth_largest(res_ref)
    thr2 = as_f32(thr2_bits)
    nxt2 = as_f32(jnp.where(thr2_bits == 0, MIN_NORMAL_BITS, thr2_bits + 1))
    n_gt = jnp.sum((res_ref[...] >= nxt2).astype(I32), axis=1, keepdims=True)
    need = (cap - n_gt).astype(F32)

    uj = lax.broadcasted_iota(I32, (SEL_BLK, SEL_BLK), 0)
    ut = lax.broadcasted_iota(I32, (SEL_BLK, SEL_BLK), 1)
    upper = (uj <= ut).astype(BF16)
    nblk = T // SEL_BLK
    olane = lax.broadcasted_iota(I32, off_ref.shape, 1)

    def blk(i, carry):
        c_gt, c_eq, offs = carry
        cols = pl.ds(pl.multiple_of(i * SEL_BLK, SEL_BLK), SEL_BLK)
        res = res_ref[:, cols]
        gt = res >= nxt2
        eq = (res >= thr2) & jnp.logical_not(gt)
        gt_f = gt.astype(F32)
        eq_f = eq.astype(F32)
        st = jnp.concatenate([gt_f, eq_f], axis=0).astype(BF16)
        cs = jnp.dot(st, upper, preferred_element_type=F32)
        gt_ex = c_gt + cs[:N_EXPERTS] - gt_f
        eq_ex = c_eq + cs[N_EXPERTS:] - eq_f
        sel = gt | (eq & (eq_ex < need))
        slot = gt_ex + jnp.minimum(eq_ex, need)
        pos_ref[:, cols] = jnp.where(sel, slot, -1.0).astype(I32)
        start = (c_gt + jnp.minimum(c_eq, need)).astype(I32)
        offs = jnp.where(olane == i, start, offs)
        return (c_gt + cs[:N_EXPERTS, SEL_BLK - 1:SEL_BLK], c_eq + cs[N_EXPERTS:, SEL_BLK - 1:SEL_BLK], offs)

    zero = jnp.zeros((N_EXPERTS, 1), F32)
    _, _, offs = lax.fori_loop(0, nblk, blk, (zero, zero, jnp.zeros(off_ref.shape, I32)))
    off_ref[...] = offs


def _select(lg_t, cap):
    T = lg_t.shape[1]
    nblk = T // SEL_BLK
    owidth = -(-nblk // LANES) * LANES
    return pl.pallas_call(
        functools.partial(_select_kernel, cap=cap, T=T),
        out_shape=[jax.ShapeDtypeStruct((N_EXPERTS, T), F32),
                   jax.ShapeDtypeStruct((N_EXPERTS, T), I32),
                   jax.ShapeDtypeStruct((N_EXPERTS, owidth), I32)],
        scratch_shapes=[pltpu.VMEM((N_EXPERTS, T), F32)],
        compiler_params=pltpu.CompilerParams(vmem_limit_bytes=VMEM_LIMIT),
        name="select",
    )(lg_t)


SLOT_TILE = 256
MOE_TOK = 512
IDX_BASE = 256


def _slots_kernel(off_ref, pos_ref, idx_ref, *, n_tok_tiles):
    e = pl.program_id(0)
    i = pl.program_id(1)

    @pl.when(i == 0)
    def _():
        idx_ref[...] = jnp.zeros(idx_ref.shape, F32)

    n0 = off_ref[e * (n_tok_tiles + 1) + i]
    n1 = off_ref[e * (n_tok_tiles + 1) + i + 1]
    j0 = n0 // SLOT_TILE
    n_tiles = jnp.where(n1 > n0, (n1 - 1) // SLOT_TILE - j0 + 1, 0)
    pos_row = pos_ref[pl.ds(e, 1), :]
    sub = lax.broadcasted_iota(I32, (SLOT_TILE, MOE_TOK), 0)
    tok = i * MOE_TOK + lax.broadcasted_iota(I32, (MOE_TOK, LANES), 0)
    col = lax.broadcasted_iota(I32, (MOE_TOK, LANES), 1)
    digits = jnp.where(col == 0, tok // IDX_BASE, jnp.where(col == 1, tok % IDX_BASE, 0)).astype(F32).astype(BF16)

    def tile(jj, carry):
        j = j0 + jj
        onehot = jnp.where(pos_row - j * SLOT_TILE == sub, 1.0, 0.0).astype(BF16)
        rows = pl.ds(pl.multiple_of(j * SLOT_TILE, SLOT_TILE), SLOT_TILE)
        idx_ref[0, rows, :] += jnp.dot(onehot, digits, preferred_element_type=F32)
        return carry

    lax.fori_loop(0, n_tiles, tile, 0)


def _slot_tokens(pos, off_flat, cap):
    T = pos.shape[1]
    nt = T // MOE_TOK
    assert T // IDX_BASE <= IDX_BASE
    grid_spec = pltpu.PrefetchScalarGridSpec(
        num_scalar_prefetch=1,
        grid=(N_EXPERTS, nt),
        in_specs=[pl.BlockSpec((N_EXPERTS, MOE_TOK), lambda e, i, off: (0, i))],
        out_specs=pl.BlockSpec((1, cap, LANES), lambda e, i, off: (e, 0, 0)))
    idx = pl.pallas_call(
        functools.partial(_slots_kernel, n_tok_tiles=nt),
        grid_spec=grid_spec,
        out_shape=jax.ShapeDtypeStruct((N_EXPERTS, cap, LANES), F32),
        compiler_params=_params(("arbitrary", "arbitrary")),
        name="slots",
    )(off_flat, pos)
    return (idx[:, :, 0] * IDX_BASE + idx[:, :, 1]).astype(I32).reshape(-1)


MOE_SLOT = 512
FF_BLK = 256
N_FF = D_FF_EXPERT // FF_BLK
ROWS_PER_FF = 48
GATHER_ROWS = ROWS_PER_FF * N_FF
assert D_FF_EXPERT % FF_BLK == 0 and GATHER_ROWS >= MOE_SLOT and GATHER_ROWS % 8 == 0


def _moe_kernel(tok_ref, x_hbm, wg_ref, wu_ref, wd_ref, out_ref, xg_ref, xb_ref, acc_ref, sem, *, n_slot_tiles):
    step = pl.program_id(0) * n_slot_tiles + pl.program_id(1)
    total = N_EXPERTS * n_slot_tiles
    cur = step % 2

    def row_copy(tile_idx, r, buf):
        slot = jnp.minimum(r, MOE_SLOT - 1)
        tok = tok_ref[tile_idx * MOE_SLOT + slot]
        return pltpu.make_async_copy(x_hbm.at[pl.ds(tok, 1)], xg_ref.at[buf, pl.ds(r, 1)], sem.at[buf])

    def wait_tile(buf):
        pltpu.make_async_copy(x_hbm.at[pl.ds(0, GATHER_ROWS)], xg_ref.at[buf], sem.at[buf]).wait()

    @pl.when(step == 0)
    def _():
        def first(r, carry):
            row_copy(0, r, 0).start()
            return carry

        lax.fori_loop(0, GATHER_ROWS, first, 0)

    wait_tile(cur)
    nxt = (step + 1) % total
    xb_ref[...] = xg_ref[cur, :MOE_SLOT, :].astype(BF16)
    acc_ref[...] = jnp.zeros(acc_ref.shape, F32)

    def ff(c, carry):
        for r in range(ROWS_PER_FF):
            row_copy(nxt, c * ROWS_PER_FF + r, 1 - cur).start()
        cols = pl.ds(pl.multiple_of(c * FF_BLK, FF_BLK), FF_BLK)
        xb = xb_ref[...]
        g = jnp.dot(xb, wg_ref[0, :, cols], preferred_element_type=F32)
        u = jnp.dot(xb, wu_ref[0, :, cols], preferred_element_type=F32)
        h = ((g / (1.0 + jnp.exp(-g))) * u).astype(BF16)
        acc_ref[...] += jnp.dot(h, wd_ref[0, cols, :], preferred_element_type=F32)
        return carry

    lax.fori_loop(0, N_FF, ff, 0)
    out_ref[0] = acc_ref[...].astype(out_ref.dtype)

    @pl.when(step == total - 1)
    def _():
        wait_tile(1 - cur)


def _moe(x1, tok_flat, wg, wu, wd, cap):
    assert cap % MOE_SLOT == 0
    nj = cap // MOE_SLOT
    grid_spec = pltpu.PrefetchScalarGridSpec(
        num_scalar_prefetch=1,
        grid=(N_EXPERTS, nj),
        in_specs=[pl.BlockSpec(memory_space=pl.ANY),
                  pl.BlockSpec((1, D_MODEL, D_FF_EXPERT), lambda e, j, tok: (e, 0, 0)),
                  pl.BlockSpec((1, D_MODEL, D_FF_EXPERT), lambda e, j, tok: (e, 0, 0)),
                  pl.BlockSpec((1, D_FF_EXPERT, D_MODEL), lambda e, j, tok: (e, 0, 0))],
        out_specs=pl.BlockSpec((1, MOE_SLOT, D_MODEL), lambda e, j, tok: (e, j, 0)),
        scratch_shapes=[pltpu.VMEM((2, GATHER_ROWS, D_MODEL), F32),
                        pltpu.VMEM((MOE_SLOT, D_MODEL), BF16),
                        pltpu.VMEM((MOE_SLOT, D_MODEL), F32),
                        pltpu.SemaphoreType.DMA((2,))])
    return pl.pallas_call(
        functools.partial(_moe_kernel, n_slot_tiles=nj),
        grid_spec=grid_spec,
        out_shape=jax.ShapeDtypeStruct((N_EXPERTS, cap, D_MODEL), BF16),
        compiler_params=_params(("arbitrary", "arbitrary")),
        name="moe",
    )(tok_flat, x1, wg, wu, wd)


CMB_TOK = 256
ROW_CHUNK = 16
STAGE_ROWS = N_EXPERTS * (CMB_TOK + 2 * ROW_CHUNK)
NO_ROW = -(2 ** 30)


def _combine_kernel(off_ref, x1_ref, gid_ref, gate_ref, g_ref, b_ref, eo_ref, o_ref, stage_ref, w_ref, sem,
                    *, cap, n_tok_tiles):
    i = pl.program_id(0)
    cur = i % 2

    def chunk_copy(src_row, dst_row, buf):
        return pltpu.make_async_copy(eo_ref.at[pl.ds(src_row, ROW_CHUNK)],
                                     stage_ref.at[buf, pl.ds(dst_row, ROW_CHUNK)], sem.at[buf])

    def segments(tile):
        base = jnp.int32(0)
        segs = []
        for e in range(N_EXPERTS):
            n0 = off_ref[e * (n_tok_tiles + 1) + tile]
            n1 = off_ref[e * (n_tok_tiles + 1) + tile + 1]
            nch = (n1 + ROW_CHUNK - 1) // ROW_CHUNK - n0 // ROW_CHUNK
            segs.append((base, nch, e * cap + (n0 // ROW_CHUNK) * ROW_CHUNK))
            base = base + nch * ROW_CHUNK
        return segs, base

    def issue(tile, buf):
        segs, _ = segments(tile)
        for seg_base, nch, seg_row in segs:
            def one(c, carry, seg_base=seg_base, seg_row=seg_row):
                src = pl.multiple_of(seg_row + c * ROW_CHUNK, ROW_CHUNK)
                dst = pl.multiple_of(seg_base + c * ROW_CHUNK, ROW_CHUNK)
                chunk_copy(src, dst, buf).start()
                return carry

            lax.fori_loop(0, nch, one, 0)

    @pl.when(i == 0)
    def _():
        stage_ref[...] = jnp.zeros(stage_ref.shape, stage_ref.dtype)
        issue(0, 0)

    @pl.when(i + 1 < n_tok_tiles)
    def _():
        issue(i + 1, 1 - cur)

    segs, total = segments(i)

    def wait(c, carry):
        chunk_copy(0, 0, cur).wait()
        return carry

    lax.fori_loop(0, total // ROW_CHUNK, wait, 0)

    jsub = lax.broadcasted_iota(I32, (SLOT_TILE, CMB_TOK), 0)
    o_ref[...] = ALPHA * x1_ref[...]

    def kchunk(kc, carry):
        j0 = kc * SLOT_TILE
        w_ref[...] = jnp.zeros(w_ref.shape, F32)
        for e, (seg_base, nch, seg_row) in enumerate(segs):
            seg_end = seg_base + nch * ROW_CHUNK

            @pl.when((seg_base < j0 + SLOT_TILE) & (seg_end > j0))
            def _(e=e, seg_base=seg_base, seg_row=seg_row):
                row_id = jsub + (j0 + seg_row - seg_base)
                w_ref[...] = jnp.where(gid_ref[e:e + 1, :] == row_id, gate_ref[e:e + 1, :], w_ref[...])

        rows = stage_ref[cur, pl.ds(pl.multiple_of(j0, SLOT_TILE), SLOT_TILE), :]
        o_ref[...] += lax.dot_general(w_ref[...].astype(BF16), rows, (((0,), (0,)), ((), ())),
                                      preferred_element_type=F32)
        return carry

    lax.fori_loop(0, (total + SLOT_TILE - 1) // SLOT_TILE, kchunk, 0)
    o_ref[...] = _layer_norm(o_ref[...], g_ref[...], b_ref[...])


def _combine(x1, gid_tok, gate_tok, off_flat, eo_flat, g, b, cap):
    T = x1.shape[0]
    nt = T // CMB_TOK
    stage_rows = -(-STAGE_ROWS // SLOT_TILE) * SLOT_TILE
    row = lambda i, off: (i, 0)
    const = lambda i, off: (0, 0)
    grid_spec = pltpu.PrefetchScalarGridSpec(
        num_scalar_prefetch=1,
        grid=(nt,),
        in_specs=[pl.BlockSpec((CMB_TOK, D_MODEL), row),
                  pl.BlockSpec((N_EXPERTS, CMB_TOK), lambda i, off: (0, i)),
                  pl.BlockSpec((N_EXPERTS, CMB_TOK), lambda i, off: (0, i)),
                  pl.BlockSpec((1, D_MODEL), const),
                  pl.BlockSpec((1, D_MODEL), const),
                  pl.BlockSpec(memory_space=pl.ANY)],
        out_specs=pl.BlockSpec((CMB_TOK, D_MODEL), row),
        scratch_shapes=[pltpu.VMEM((2, stage_rows, D_MODEL), BF16),
                        pltpu.VMEM((SLOT_TILE, CMB_TOK), F32),
                        pltpu.SemaphoreType.DMA((2,))])
    return pl.pallas_call(
        functools.partial(_combine_kernel, cap=cap, n_tok_tiles=nt),
        grid_spec=grid_spec,
        out_shape=jax.ShapeDtypeStruct((T, D_MODEL), F32),
        compiler_params=_params(("arbitrary",)),
        name="combine",
    )(off_flat, x1, gid_tok, gate_tok, g.reshape(1, D_MODEL), b.reshape(1, D_MODEL), eo_flat)


def _tile_offsets(off256, cap, T, tok):
    step = tok // SEL_BLK
    o = off256[:, :T // SEL_BLK:step]
    o = jnp.concatenate([o, jnp.full((N_EXPERTS, 1), cap, I32)], axis=1)
    return o.reshape(-1)


def _trunk(x, w_in_p, conv_w, conv_b, dt_bias, a_log, d_skip, ssm_norm_w, w_out_b, ln1_g, ln1_b,
           wr_t, wg, wu, wd, ln2_g, ln2_b):
    B, seq, _ = x.shape
    T = B * seq
    cap = CAPACITY_FACTOR * T // N_EXPERTS
    x2 = x.reshape(T, D_MODEL)
    q, k, v, z, xbc, dt = _inproj(x2, w_in_p, seq)
    shp = (B, seq, ATTN_WIDTH)
    attn = _attention(q.reshape(shp), k.reshape(shp), v.reshape(shp)).reshape(T, ATTN_WIDTH)
    y = _ssd(xbc, dt, z, conv_w, conv_b, dt_bias, a_log, d_skip, ssm_norm_w, B, seq)
    x1, lg_t = _outproj(attn, y, x2, w_out_b, ln1_g, ln1_b, wr_t)
    aff_t, pos, off256 = _select(lg_t, cap)
    tok_flat = _slot_tokens(pos, _tile_offsets(off256, cap, T, MOE_TOK), cap)
    eo = _moe(x1, tok_flat, wg, wu, wd, cap)
    gid = jnp.where(pos >= 0, pos + jnp.arange(N_EXPERTS, dtype=I32)[:, None] * cap, NO_ROW)
    out = _combine(x1, gid, aff_t, _tile_offsets(off256, cap, T, CMB_TOK),
                   eo.reshape(N_EXPERTS * cap, D_MODEL), ln2_g, ln2_b, cap)
    return out.reshape(B, seq, D_MODEL)


def kernel(x_prompt, x_sample, w_in, conv_w, conv_b, dt_bias, a_log, d_skip, ssm_norm_w, w_out, ln1_g, ln1_b,
           w_router, w_gate, w_up, w_down, ln2_g, ln2_b):
    assert DEPTH == 1
    l = 0
    w_in_p = jnp.pad(w_in[l], ((0, 0), (0, IN_PAD - w_in.shape[-1]))).astype(BF16)
    args = (w_in_p, conv_w[l], conv_b[l], dt_bias[l], a_log[l], d_skip[l], ssm_norm_w[l],
            w_out[l].astype(BF16), ln1_g[l], ln1_b[l], w_router[l].T,
            w_gate[l].astype(BF16), w_up[l].astype(BF16), w_down[l].astype(BF16), ln2_g[l], ln2_b[l])
    return (_trunk(x_prompt, *args), _trunk(x_sample, *args))
```

```python
import functools

import jax
import jax.numpy as jnp
from jax import lax
from jax.experimental import pallas as pl
from jax.experimental.pallas import tpu as pltpu

F32 = jnp.float32
BF16 = jnp.bfloat16
I32 = jnp.int32
HIGHEST = lax.Precision.HIGHEST

D_MODEL = 1024
DEPTH = 1
HEAD_DIM = 64
N_ATTN_HEADS = 8
ATTN_WIDTH = N_ATTN_HEADS * HEAD_DIM
ROT_DIM = HEAD_DIM // 4
ROPE_THETA = 500000.0
DILATED_PATTERNS = ((128, 1), (512, 4), (2048, 16))
N_SSM_HEADS = 8
SSM_HEAD_DIM = 64
SSM_WIDTH = N_SSM_HEADS * SSM_HEAD_DIM
SSM_GROUPS = 2
D_STATE = 128
CONV_K = 5
CHUNK = 128
CONV_CH = SSM_WIDTH + 2 * SSM_GROUPS * D_STATE
MIX_WIDTH = ATTN_WIDTH + SSM_WIDTH
N_EXPERTS = 16
CAPACITY_FACTOR = 2
D_FF_EXPERT = 2816
ALPHA = (2.0 * DEPTH) ** 0.25
LN_EPS = 1e-5
RMS_EPS = 1e-5

LANES = 128
VMEM_LIMIT = 56 * 1024 * 1024
NEG = -1e30
HALF_WIN = 64

IN_PAD = 3 * ATTN_WIDTH + SSM_WIDTH + CONV_CH + LANES


def _params(sem):
    return pltpu.CompilerParams(dimension_semantics=sem, vmem_limit_bytes=VMEM_LIMIT)


def _inproj_kernel(x_ref, w_ref, c_ref, sa_ref, sb_ref, q_ref, k_ref, v_ref, z_ref, xbc_ref, dt_ref):
    x = x_ref[...].astype(BF16)

    def seg(lo, hi):
        return jnp.dot(x, w_ref[:, lo:hi], preferred_element_type=F32)

    c = c_ref[...]
    sa = sa_ref[...]
    sb = sb_ref[...]

    def rotary(t, scale):
        outs = []
        for g in range(ATTN_WIDTH // LANES):
            tg = t[:, g * LANES:(g + 1) * LANES]
            up = pltpu.roll(tg, LANES - ROT_DIM // 2, 1)
            dn = pltpu.roll(tg, ROT_DIM // 2, 1)
            outs.append((tg * c + up * sa + dn * sb) * scale)
        return jnp.concatenate(outs, axis=1)

    a = ATTN_WIDTH
    q_ref[...] = rotary(seg(0, a), HEAD_DIM ** -0.5)
    k_ref[...] = rotary(seg(a, 2 * a), 1.0)
    v_ref[...] = seg(2 * a, 3 * a)
    z_ref[...] = seg(3 * a, 3 * a + SSM_WIDTH)
    o = 3 * a + SSM_WIDTH
    xbc_ref[...] = seg(o, o + CONV_CH)
    dt_ref[...] = seg(o + CONV_CH, o + CONV_CH + LANES)


def _rotary_tables(seq):
    half = ROT_DIM // 2
    inv = ROPE_THETA ** (-jnp.arange(half, dtype=F32) * 2.0 / ROT_DIM)
    ang = jnp.arange(seq).astype(F32)[:, None] * inv[None, :]
    cos = jnp.cos(ang)
    sin = jnp.sin(ang)
    m = jnp.arange(LANES) % HEAD_DIM
    c = jnp.where(m[None, :] < ROT_DIM, cos[:, m % half], 1.0)
    sa = jnp.where(m[None, :] < half, -sin[:, m % half], 0.0)
    sb = jnp.where((m[None, :] >= half) & (m[None, :] < ROT_DIM), sin[:, m % half], 0.0)
    return c.astype(F32), sa.astype(F32), sb.astype(F32)


def _inproj(x2, w_pad, seq, tm=512):
    T = x2.shape[0]
    c, sa, sb = _rotary_tables(seq)
    nseq = seq // tm
    row = lambda i: (i, 0)
    tab = lambda i: (i % nseq, 0)
    outs = [jax.ShapeDtypeStruct((T, ATTN_WIDTH), F32)] * 3 + [
        jax.ShapeDtypeStruct((T, SSM_WIDTH), F32),
        jax.ShapeDtypeStruct((T, CONV_CH), F32),
        jax.ShapeDtypeStruct((T, LANES), F32)]
    return pl.pallas_call(
        _inproj_kernel,
        grid=(T // tm,),
        in_specs=[pl.BlockSpec((tm, D_MODEL), row),
                  pl.BlockSpec((D_MODEL, IN_PAD), lambda i: (0, 0)),
                  pl.BlockSpec((tm, LANES), tab),
                  pl.BlockSpec((tm, LANES), tab),
                  pl.BlockSpec((tm, LANES), tab)],
        out_specs=[pl.BlockSpec((tm, ATTN_WIDTH), row)] * 3 + [
            pl.BlockSpec((tm, SSM_WIDTH), row),
            pl.BlockSpec((tm, CONV_CH), row),
            pl.BlockSpec((tm, LANES), row)],
        out_shape=outs,
        compiler_params=_params(("parallel",)),
        name="inproj",
    )(x2, w_pad, c, sa, sb)


QB = 128
KB = 256
ATTN_GROUP = 4


RES = 4


def _attn_kernel(q_hbm, k_hbm, v_hbm, *rest, seq):
    o_refs = rest[:RES]
    qa, ka, va, m_ref, l_ref, acc_ref, bias_ref, sem = rest[RES:]
    b = pl.program_id(0)
    h = pl.program_id(1)
    sub = seq // RES
    npair = ATTN_WIDTH // LANES

    copies = []
    for src, dst in ((q_hbm, qa), (k_hbm, ka), (v_hbm, va)):
        for r in range(RES):
            lanes = pl.ds(pl.multiple_of((r * npair + h) * LANES, LANES), LANES)
            copies.append(pltpu.make_async_copy(src.at[b, :, lanes], dst.at[pl.ds(r * sub, sub), :], sem))
    for c in copies:
        c.start()

    row = lax.broadcasted_iota(I32, (2 * QB, KB), 0)
    col = lax.broadcasted_iota(I32, (2 * QB, KB), 1)
    iq = jnp.where(row >= QB, row - QB, row)
    iq_runs = RES * (iq % (QB // RES)) + iq // (QB // RES)
    ik_runs = RES * (col % (KB // RES)) + col // (KB // RES)
    for di in range(3):
        bias_ref[0, di] = jnp.where(jnp.abs(di * HALF_WIN + iq - col) <= HALF_WIN, 0.0, NEG).astype(F32)
        bias_ref[1, di] = jnp.where(jnp.abs(di * HALF_WIN + iq_runs - ik_runs) <= HALF_WIN, 0.0, NEG).astype(F32)
    m_ref[...] = jnp.full(m_ref.shape, NEG, F32)
    l_ref[...] = jnp.zeros(l_ref.shape, F32)
    acc_ref[...] = jnp.zeros(acc_ref.shape, F32)
    for c in copies:
        c.wait()

    lane = lax.broadcasted_iota(I32, (QB, LANES), 1)
    head0 = lane < HEAD_DIM
    ones = jnp.ones((KB, LANES), BF16)

    def load(ref, runs, *lead):
        parts = [ref[lead + (rn, slice(None))] for rn in runs]
        return parts[0] if len(parts) == 1 else jnp.concatenate(parts, axis=0)

    def store(ref, runs, val, *lead):
        at = 0
        for rn, size in runs:
            ref[lead + (rn, slice(None))] = val[at:at + size]
            at += size

    for window, dil in DILATED_PATTERNS:
        assert window // (2 * dil) == HALF_WIN and (dil == 1 or dil % RES == 0)
        sub_len = seq // dil
        nblk = sub_len // QB
        assert sub_len >= KB and sub_len % QB == 0 and (dil * nblk) % ATTN_GROUP == 0

        def block_rows(idx, dil=dil, sub_len=sub_len, nblk=nblk):
            r = idx // nblk
            n = idx - r * nblk
            kb = jnp.clip(n * QB - HALF_WIN, 0, sub_len - KB)
            di = (n * QB - kb) // HALF_WIN
            if dil == 1:
                qn, kn = QB // RES, KB // RES
                q_runs = [(pl.ds(pl.multiple_of(c * sub + n * qn, qn), qn), qn) for c in range(RES)]
                k_runs = [pl.ds(pl.multiple_of(c * sub + kb // RES, 8), kn) for c in range(RES)]
                return q_runs, k_runs, bias_ref[1, di]
            step = dil // RES
            base = (r % RES) * sub + r // RES
            if step == 1:
                q_runs = [(pl.ds(pl.multiple_of(base + n * QB, QB), QB), QB)]
                k_runs = [pl.ds(pl.multiple_of(base + kb, HALF_WIN), KB)]
            else:
                q_runs = [(pl.ds(base + step * QB * n, QB, stride=step), QB)]
                k_runs = [pl.ds(base + step * kb, KB, stride=step)]
            return q_runs, k_runs, bias_ref[0, di]

        def body(gi, carry, block_rows=block_rows):
            loaded = []
            for u in range(ATTN_GROUP):
                q_runs, k_runs, bias = block_rows(gi * ATTN_GROUP + u)
                qr = [rn for rn, _ in q_runs]
                loaded.append((q_runs, load(qa, qr), load(ka, k_runs), load(va, k_runs), bias,
                               load(m_ref, qr, 0), load(m_ref, qr, 1), load(l_ref, qr, 0), load(l_ref, qr, 1),
                               load(acc_ref, qr)))
            results = []
            for q_runs, q, k, v, bias, m0, m1, l0, l1, acc in loaded:
                qs = jnp.concatenate([jnp.where(head0, q, 0.0), jnp.where(head0, 0.0, q)], axis=0).astype(BF16)
                s = lax.dot_general(qs, k.astype(BF16), (((1,), (1,)), ((), ())), preferred_element_type=F32)
                s = s + bias
                m_prev = jnp.concatenate([m0, m1], axis=0)
                l_prev = jnp.concatenate([l0, l1], axis=0)
                m_new = jnp.maximum(m_prev, jnp.max(s, axis=1, keepdims=True))
                alpha = jnp.exp(m_prev - m_new)
                p = jnp.exp(s - jnp.concatenate([m_new, m_new], axis=1))
                pv = jnp.dot(p.astype(BF16), jnp.concatenate([v.astype(BF16), ones], axis=1),
                             preferred_element_type=F32)
                l_new = alpha * l_prev + pv[:, LANES:]
                a_l = jnp.where(head0, alpha[:QB], alpha[QB:])
                pv_l = jnp.where(head0, pv[:QB, :LANES], pv[QB:, :LANES])
                results.append((q_runs, a_l * acc + pv_l, m_new, l_new))
            for q_runs, acc_new, m_new, l_new in results:
                store(acc_ref, q_runs, acc_new)
                store(m_ref, q_runs, m_new[:QB], 0)
                store(m_ref, q_runs, m_new[QB:], 1)
                store(l_ref, q_runs, l_new[:QB], 0)
                store(l_ref, q_runs, l_new[QB:], 1)
            return carry

        lax.fori_loop(0, dil * nblk // ATTN_GROUP, body, 0)

    for r in range(RES):
        def fin(i, carry, r=r):
            rws = pl.ds(pl.multiple_of(r * sub + i * QB, QB), QB)
            den = jnp.where(head0, l_ref[0, rws, :], l_ref[1, rws, :])
            o_refs[r][0, pl.ds(pl.multiple_of(i * QB, QB), QB), :] = (acc_ref[rws, :] / den).astype(BF16)
            return carry

        lax.fori_loop(0, sub // QB, fin, 0)


def _attention(q, k, v):
    B, seq, _ = q.shape
    npair = ATTN_WIDTH // LANES
    sub = seq // RES
    view = lambda t: t.reshape(B, sub, RES * ATTN_WIDTH)
    any_spec = pl.BlockSpec(memory_space=pl.ANY)
    out_spec = pl.BlockSpec((1, sub, LANES), lambda b, h: (b, 0, h))
    outs = pl.pallas_call(
        functools.partial(_attn_kernel, seq=seq),
        grid=(B, npair),
        in_specs=[any_spec, any_spec, any_spec],
        out_specs=[out_spec] * RES,
        out_shape=[jax.ShapeDtypeStruct((B, sub, ATTN_WIDTH), BF16)] * RES,
        scratch_shapes=[pltpu.VMEM((seq, LANES), F32),
                        pltpu.VMEM((seq, LANES), F32),
                        pltpu.VMEM((seq, LANES), F32),
                        pltpu.VMEM((2, seq, LANES), F32),
                        pltpu.VMEM((2, seq, LANES), F32),
                        pltpu.VMEM((seq, LANES), F32),
                        pltpu.VMEM((2, 3, 2 * QB, KB), F32),
                        pltpu.SemaphoreType.DMA(())],
        compiler_params=_params(("parallel", "parallel")),
        name="attn",
    )(view(q), view(k), view(v))
    return jnp.stack(outs, axis=2).reshape(B, seq, ATTN_WIDTH)


SSD_TILE = 256
HALO = 8


def _ssd_kernel(*refs, reverse):
    if reverse:
        (xbc_ref, hp_ref, hn_ref, dt_ref, cw_ref, cb_ref, dtb_ref, alog_ref,
         yf_ref, z_ref, dsk_ref, nw_ref, y_ref, ext_ref, st_ref) = refs
    else:
        (xbc_ref, hp_ref, hn_ref, dt_ref, cw_ref, cb_ref, dtb_ref, alog_ref,
         y_ref, ext_ref, st_ref) = refs
    lane_off = N_SSM_HEADS if reverse else 0
    nchunk = SSD_TILE // CHUNK

    @pl.when(pl.program_id(1) == 0)
    def _():
        st_ref[...] = jnp.zeros(st_ref.shape, F32)

    ext_ref[0:HALO, :] = hp_ref[0]
    ext_ref[HALO:HALO + SSD_TILE, :] = xbc_ref[...]
    ext_ref[HALO + SSD_TILE:HALO + SSD_TILE + HALO, :] = hn_ref[0]

    li = lax.broadcasted_iota(I32, (CHUNK, CHUNK), 0)
    si = lax.broadcasted_iota(I32, (CHUNK, CHUNK), 1)
    tri = (si >= li) if reverse else (si <= li)
    tri_f = tri.astype(F32)
    ej = lax.broadcasted_iota(I32, (LANES, SSM_WIDTH), 0)
    ec = lax.broadcasted_iota(I32, (LANES, SSM_WIDTH), 1)
    expand = (ej == lane_off + ec // SSM_HEAD_DIM).astype(BF16)
    lane = lax.broadcasted_iota(I32, (CHUNK, LANES), 1)
    lo_half = lane < SSM_HEAD_DIM
    a_row = -jnp.exp(alog_ref[...])
    gw = SSM_WIDTH // SSM_GROUPS
    heads_per_group = N_SSM_HEADS // SSM_GROUPS

    order = range(nchunk - 1, -1, -1) if reverse else range(nchunk)
    for c in order:
        base = HALO + c * CHUNK
        conv = None
        for kk in range(CONV_K):
            tap = ext_ref[pl.ds(base + kk - CONV_K // 2, CHUNK), :] * cw_ref[kk:kk + 1, :]
            conv = tap if conv is None else conv + tap
        u = conv + cb_ref[...]
        u = u / (1.0 + jnp.exp(-u))
        xs = u[:, :SSM_WIDTH]
        bm = u[:, SSM_WIDTH:SSM_WIDTH + SSM_GROUPS * D_STATE]
        cm = u[:, SSM_WIDTH + SSM_GROUPS * D_STATE:]

        dpre = dt_ref[pl.ds(c * CHUNK, CHUNK), :] + dtb_ref[...]
        dtv = jnp.maximum(dpre, 0.0) + jnp.log(1.0 + jnp.exp(-jnp.abs(dpre)))
        a = dtv * a_row
        acs = jnp.dot(tri_f, a, precision=HIGHEST, preferred_element_type=F32)
        acs_t = acs.T
        tot = acs[0:1, :] if reverse else acs[CHUNK - 1:CHUNK, :]
        stacked = jnp.concatenate([dtv, jnp.exp(acs), jnp.exp(tot - acs)], axis=0)
        s_hi = stacked.astype(BF16)
        s_lo = (stacked - s_hi.astype(F32)).astype(BF16)
        ex = (jnp.dot(s_hi, expand, preferred_element_type=F32)
              + jnp.dot(s_lo, expand, preferred_element_type=F32))
        dt_x = ex[:CHUNK]
        eacs_x = ex[CHUNK:2 * CHUNK]
        dend_x = ex[2 * CHUNK:]
        cdec_x = eacs_x[0:1, :] if reverse else eacs_x[CHUNK - 1:CHUNK, :]
        xdt = xs * dt_x
        wst = (dend_x * xdt).astype(BF16)
        st_prev = st_ref[...]
        st_b = st_prev.astype(BF16)

        ygs = []
        st_new = []
        for g in range(SSM_GROUPS):
            ys = []
            bg = bm[:, g * D_STATE:(g + 1) * D_STATE]
            cg = cm[:, g * D_STATE:(g + 1) * D_STATE].astype(BF16)
            cb = lax.dot_general(cg, bg.astype(BF16), (((1,), (1,)), ((), ())), preferred_element_type=F32)
            ms = []
            for hh in range(heads_per_group):
                h = lane_off + g * heads_per_group + hh
                diff = acs[:, h:h + 1] - acs_t[h:h + 1, :]
                lm = jnp.exp(jnp.where(tri, diff, NEG))
                ms.append((cb * lm).astype(BF16))
            for pp in range(heads_per_group // 2):
                hp = g * (heads_per_group // 2) + pp
                lhs = jnp.concatenate([ms[2 * pp], ms[2 * pp + 1]], axis=1)
                xp = xdt[:, hp * LANES:(hp + 1) * LANES]
                rhs = jnp.concatenate([jnp.where(lo_half, xp, 0.0), jnp.where(lo_half, 0.0, xp)],
                                      axis=0).astype(BF16)
                ys.append(jnp.dot(lhs, rhs, preferred_element_type=F32))
            sc = jnp.dot(bg.T.astype(BF16), wst[:, g * gw:(g + 1) * gw], preferred_element_type=F32)
            yoff = jnp.dot(cg, st_b[:, g * gw:(g + 1) * gw], preferred_element_type=F32)
            ygs.append(jnp.concatenate(ys, axis=1) + yoff * eacs_x[:, g * gw:(g + 1) * gw])
            st_new.append(st_prev[:, g * gw:(g + 1) * gw] * cdec_x[:, g * gw:(g + 1) * gw] + sc)
        st_ref[...] = jnp.concatenate(st_new, axis=1)
        y = jnp.concatenate(ygs, axis=1)

        crow = pl.ds(c * CHUNK, CHUNK)
        if reverse:
            ytot = yf_ref[crow, :] + y + dsk_ref[...] * xs
            zz = z_ref[crow, :]
            yz = ytot * (zz / (1.0 + jnp.exp(-zz)))
            yn = yz * lax.rsqrt(jnp.mean(jnp.square(yz), axis=-1, keepdims=True) + RMS_EPS) * nw_ref[...]
            y_ref[crow, :] = yn.astype(y_ref.dtype)
        else:
            y_ref[crow, :] = y


def _ssd(xbc, dt, z, conv_w, conv_b, dt_bias, a_log, d_skip, ssm_norm_w, B, seq):
    T = B * seq
    nt = seq // SSD_TILE
    x4 = xbc.reshape(B, nt, SSD_TILE, CONV_CH)
    zeros = jnp.zeros((B, 1, HALO, CONV_CH), F32)
    hprev = jnp.concatenate([zeros, x4[:, :-1, SSD_TILE - HALO:, :]], axis=1).reshape(B * nt, HALO, CONV_CH)
    hnext = jnp.concatenate([x4[:, 1:, :HALO, :], zeros], axis=1).reshape(B * nt, HALO, CONV_CH)
    cw = jnp.zeros((HALO, CONV_CH), F32).at[:CONV_K].set(conv_w)
    cb = conv_b.reshape(1, CONV_CH)
    pad = LANES - 2 * N_SSM_HEADS
    dtb = jnp.pad(dt_bias.reshape(1, 2 * N_SSM_HEADS), ((0, 0), (0, pad)))
    alog = jnp.pad(a_log.reshape(1, 2 * N_SSM_HEADS), ((0, 0), (0, pad)), constant_values=-1e30)
    dsk = jnp.repeat(d_skip, SSM_HEAD_DIM).reshape(1, SSM_WIDTH)
    nw = ssm_norm_w.reshape(1, SSM_WIDTH)

    def call(reverse, extra_in, extra_specs, out_dtype):
        if reverse:
            tmap = lambda b, i: (b * nt + nt - 1 - i, 0)
            hmap = lambda b, i: (b * nt + nt - 1 - i, 0, 0)
        else:
            tmap = lambda b, i: (b * nt + i, 0)
            hmap = lambda b, i: (b * nt + i, 0, 0)
        const = lambda b, i: (0, 0)
        in_specs = [pl.BlockSpec((SSD_TILE, CONV_CH), tmap),
                    pl.BlockSpec((1, HALO, CONV_CH), hmap),
                    pl.BlockSpec((1, HALO, CONV_CH), hmap),
                    pl.BlockSpec((SSD_TILE, LANES), tmap),
                    pl.BlockSpec((HALO, CONV_CH), const),
                    pl.BlockSpec((1, CONV_CH), const),
                    pl.BlockSpec((1, LANES), const),
                    pl.BlockSpec((1, LANES), const)] + [s(tmap, const) for s in extra_specs]
        return pl.pallas_call(
            functools.partial(_ssd_kernel, reverse=reverse),
            grid=(B, nt),
            in_specs=in_specs,
            out_specs=pl.BlockSpec((SSD_TILE, SSM_WIDTH), tmap),
            out_shape=jax.ShapeDtypeStruct((T, SSM_WIDTH), out_dtype),
            scratch_shapes=[pltpu.VMEM((SSD_TILE + 2 * HALO, CONV_CH), F32),
                            pltpu.VMEM((D_STATE, SSM_WIDTH), F32)],
            compiler_params=_params(("parallel", "arbitrary")),
            name="ssd_bwd" if reverse else "ssd_fwd",
        )(xbc, hprev, hnext, dt, cw, cb, dtb, alog, *extra_in)

    yf = call(False, [], [], F32)
    tile = lambda tmap, const: pl.BlockSpec((SSD_TILE, SSM_WIDTH), tmap)
    rowc = lambda tmap, const: pl.BlockSpec((1, SSM_WIDTH), const)
    return call(True, [yf, z, dsk, nw], [tile, tile, rowc, rowc], BF16)


def _layer_norm(h, g, b):
    mu = jnp.mean(h, axis=-1, keepdims=True)
    var = jnp.mean(jnp.square(h - mu), axis=-1, keepdims=True)
    return (h - mu) * lax.rsqrt(var + LN_EPS) * g + b


def _outproj_kernel(a_ref, y_ref, x_ref, w_ref, g_ref, b_ref, wr_ref, x1_ref, lg_ref):
    mix = jnp.dot(a_ref[...], w_ref[:ATTN_WIDTH, :], preferred_element_type=F32)
    mix = mix + jnp.dot(y_ref[...], w_ref[ATTN_WIDTH:, :], preferred_element_type=F32)
    x1 = _layer_norm(ALPHA * x_ref[...] + mix, g_ref[...], b_ref[...])
    x1_ref[...] = x1
    lg_ref[...] = lax.dot_general(wr_ref[...], x1, (((1,), (1,)), ((), ())),
                                  precision=HIGHEST, preferred_element_type=F32)


def _outproj(attn, y, x2, w_out_b, g, b, wr_t, tm=512):
    T = x2.shape[0]
    row = lambda i: (i, 0)
    const = lambda i: (0, 0)
    return pl.pallas_call(
        _outproj_kernel,
        grid=(T // tm,),
        in_specs=[pl.BlockSpec((tm, ATTN_WIDTH), row),
                  pl.BlockSpec((tm, SSM_WIDTH), row),
                  pl.BlockSpec((tm, D_MODEL), row),
                  pl.BlockSpec((MIX_WIDTH, D_MODEL), const),
                  pl.BlockSpec((1, D_MODEL), const),
                  pl.BlockSpec((1, D_MODEL), const),
                  pl.BlockSpec((N_EXPERTS, D_MODEL), const)],
        out_specs=[pl.BlockSpec((tm, D_MODEL), row),
                   pl.BlockSpec((N_EXPERTS, tm), lambda i: (0, i))],
        out_shape=[jax.ShapeDtypeStruct((T, D_MODEL), F32),
                   jax.ShapeDtypeStruct((N_EXPERTS, T), F32)],
        compiler_params=_params(("parallel",)),
        name="outproj",
    )(attn, y, x2, w_out_b, g.reshape(1, D_MODEL), b.reshape(1, D_MODEL), wr_t)


SEL_BLK = 256


MIN_NORMAL_BITS = 0x00800000


def _select_kernel(lg_ref, aff_ref, pos_ref, off_ref, res_ref, *, cap, T):
    lg = lg_ref[...]
    ex = jnp.exp(lg - jnp.max(lg, axis=0, keepdims=True))
    aff_ref[...] = ex / jnp.sum(ex, axis=0, keepdims=True)

    def as_f32(bits):
        return lax.bitcast_convert_type(bits, F32)

    def kth_largest(ref):
        def search(i, t):
            cand = t | jnp.left_shift(jnp.int32(1), 30 - i)
            cnt = jnp.sum((ref[...] >= as_f32(cand)).astype(I32), axis=1, keepdims=True)
            return jnp.where((cnt >= cap) & (cand >= MIN_NORMAL_BITS), cand, t)

        return lax.fori_loop(0, 31, search, jnp.zeros((N_EXPERTS, 1), I32))

    thr1 = as_f32(kth_largest(aff_ref))
    res_ref[...] = aff_ref[...] - thr1
    thr2_bits = kth_largest(res_ref)
    thr2 = as_f32(thr2_bits)
    nxt2 = as_f32(jnp.where(thr2_bits == 0, MIN_NORMAL_BITS, thr2_bits + 1))
    n_gt = jnp.sum((res_ref[...] >= nxt2).astype(I32), axis=1, keepdims=True)
    need = (cap - n_gt).astype(F32)

    uj = lax.broadcasted_iota(I32, (SEL_BLK, SEL_BLK), 0)
    ut = lax.broadcasted_iota(I32, (SEL_BLK, SEL_BLK), 1)
    upper = (uj <= ut).astype(BF16)
    nblk = T // SEL_BLK
    olane = lax.broadcasted_iota(I32, off_ref.shape, 1)

    def blk(i, carry):
        c_gt, c_eq, offs = carry
        cols = pl.ds(pl.multiple_of(i * SEL_BLK, SEL_BLK), SEL_BLK)
        res = res_ref[:, cols]
        gt = res >= nxt2
        eq = (res >= thr2) & jnp.logical_not(gt)
        gt_f = gt.astype(F32)
        eq_f = eq.astype(F32)
        st = jnp.concatenate([gt_f, eq_f], axis=0).astype(BF16)
        cs = jnp.dot(st, upper, preferred_element_type=F32)
        gt_ex = c_gt + cs[:N_EXPERTS] - gt_f
        eq_ex = c_eq + cs[N_EXPERTS:] - eq_f
        sel = gt | (eq & (eq_ex < need))
        slot = gt_ex + jnp.minimum(eq_ex, need)
        pos_ref[:, cols] = jnp.where(sel, slot, -1.0).astype(I32)
        start = (c_gt + jnp.minimum(c_eq, need)).astype(I32)
        offs = jnp.where(olane == i, start, offs)
        return (c_gt + cs[:N_EXPERTS, SEL_BLK - 1:SEL_BLK], c_eq + cs[N_EXPERTS:, SEL_BLK - 1:SEL_BLK], offs)

    zero = jnp.zeros((N_EXPERTS, 1), F32)
    _, _, offs = lax.fori_loop(0, nblk, blk, (zero, zero, jnp.zeros(off_ref.shape, I32)))
    off_ref[...] = offs


def _select(lg_t, cap):
    T = lg_t.shape[1]
    nblk = T // SEL_BLK
    owidth = -(-nblk // LANES) * LANES
    return pl.pallas_call(
        functools.partial(_select_kernel, cap=cap, T=T),
        out_shape=[jax.ShapeDtypeStruct((N_EXPERTS, T), F32),
                   jax.ShapeDtypeStruct((N_EXPERTS, T), I32),
                   jax.ShapeDtypeStruct((N_EXPERTS, owidth), I32)],
        scratch_shapes=[pltpu.VMEM((N_EXPERTS, T), F32)],
        compiler_params=pltpu.CompilerParams(vmem_limit_bytes=VMEM_LIMIT),
        name="select",
    )(lg_t)


SLOT_TILE = 256
MOE_TOK = 512
IDX_BASE = 256


def _slots_kernel(off_ref, pos_ref, idx_ref, digits_ref, *, n_tok_tiles):
    e = pl.program_id(0)
    i = pl.program_id(1)

    @pl.when(i == 0)
    def _():
        idx_ref[...] = jnp.zeros(idx_ref.shape, F32)

    n0 = off_ref[e * (n_tok_tiles + 1) + i]
    n1 = off_ref[e * (n_tok_tiles + 1) + i + 1]
    j0 = n0 // SLOT_TILE
    n_tiles = jnp.where(n1 > n0, (n1 - 1) // SLOT_TILE - j0 + 1, 0)
    pos_row = pos_ref[pl.ds(e, 1), :]
    tok = i * MOE_TOK + lax.broadcasted_iota(I32, (MOE_TOK, LANES), 0)
    col = lax.broadcasted_iota(I32, (MOE_TOK, LANES), 1)
    hi = jnp.right_shift(tok, IDX_BASE.bit_length() - 1)
    lo = jnp.bitwise_and(tok, IDX_BASE - 1)
    digits_ref[...] = jnp.where(col == 0, hi, jnp.where(col == 1, lo, 0)).astype(F32).astype(BF16)

    def tile(jj, carry):
        j = j0 + jj
        sub = lax.broadcasted_iota(I32, (SLOT_TILE, MOE_TOK), 0)
        onehot = jnp.where(pos_row - j * SLOT_TILE == sub, 1.0, 0.0).astype(BF16)
        rows = pl.ds(pl.multiple_of(j * SLOT_TILE, SLOT_TILE), SLOT_TILE)
        idx_ref[0, rows, :] += jnp.dot(onehot, digits_ref[...], preferred_element_type=F32)
        return carry

    lax.fori_loop(0, n_tiles, tile, 0)


def _slot_tokens(pos, off_flat, cap):
    T = pos.shape[1]
    nt = T // MOE_TOK
    assert T // IDX_BASE <= IDX_BASE
    grid_spec = pltpu.PrefetchScalarGridSpec(
        num_scalar_prefetch=1,
        grid=(N_EXPERTS, nt),
        in_specs=[pl.BlockSpec((N_EXPERTS, MOE_TOK), lambda e, i, off: (0, i))],
        out_specs=pl.BlockSpec((1, cap, LANES), lambda e, i, off: (e, 0, 0)),
        scratch_shapes=[pltpu.VMEM((MOE_TOK, LANES), BF16)])
    idx = pl.pallas_call(
        functools.partial(_slots_kernel, n_tok_tiles=nt),
        grid_spec=grid_spec,
        out_shape=jax.ShapeDtypeStruct((N_EXPERTS, cap, LANES), F32),
        compiler_params=_params(("arbitrary", "arbitrary")),
        name="slots",
    )(off_flat, pos)
    return (idx[:, :, 0] * IDX_BASE + idx[:, :, 1]).astype(I32).reshape(-1)


MOE_SLOT = 512
FF_BLK = 256
N_FF = D_FF_EXPERT // FF_BLK
ROWS_PER_FF = 48
GATHER_ROWS = ROWS_PER_FF * N_FF
assert D_FF_EXPERT % FF_BLK == 0 and GATHER_ROWS >= MOE_SLOT and GATHER_ROWS % 8 == 0


def _moe_kernel(tok_ref, x_hbm, wg_ref, wu_ref, wd_ref, out_ref, xg_ref, xb_ref, acc_ref, sem, *, n_slot_tiles):
    step = pl.program_id(0) * n_slot_tiles + pl.program_id(1)
    total = N_EXPERTS * n_slot_tiles
    cur = step % 2

    def row_copy(tile_idx, r, buf):
        slot = jnp.minimum(r, MOE_SLOT - 1)
        tok = tok_ref[tile_idx * MOE_SLOT + slot]
        return pltpu.make_async_copy(x_hbm.at[pl.ds(tok, 1)], xg_ref.at[buf, pl.ds(r, 1)], sem.at[buf])

    def wait_tile(buf):
        pltpu.make_async_copy(x_hbm.at[pl.ds(0, GATHER_ROWS)], xg_ref.at[buf], sem.at[buf]).wait()

    @pl.when(step == 0)
    def _():
        def first(r, carry):
            row_copy(0, r, 0).start()
            return carry

        lax.fori_loop(0, GATHER_ROWS, first, 0)

    wait_tile(cur)
    nxt = (step + 1) % total
    xb_ref[...] = xg_ref[cur, :MOE_SLOT, :].astype(BF16)
    acc_ref[...] = jnp.zeros(acc_ref.shape, F32)

    def ff(c, carry):
        for r in range(ROWS_PER_FF):
            row_copy(nxt, c * ROWS_PER_FF + r, 1 - cur).start()
        cols = pl.ds(pl.multiple_of(c * FF_BLK, FF_BLK), FF_BLK)
        xb = xb_ref[...]
        g = jnp.dot(xb, wg_ref[0, :, cols], preferred_element_type=F32)
        u = jnp.dot(xb, wu_ref[0, :, cols], preferred_element_type=F32)
        h = ((g / (1.0 + jnp.exp(-g))) * u).astype(BF16)
        acc_ref[...] += jnp.dot(h, wd_ref[0, cols, :], preferred_element_type=F32)
        return carry

    lax.fori_loop(0, N_FF, ff, 0)
    out_ref[0] = acc_ref[...].astype(out_ref.dtype)

    @pl.when(step == total - 1)
    def _():
        wait_tile(1 - cur)


def _moe(x1, tok_flat, wg, wu, wd, cap):
    assert cap % MOE_SLOT == 0
    nj = cap // MOE_SLOT
    grid_spec = pltpu.PrefetchScalarGridSpec(
        num_scalar_prefetch=1,
        grid=(N_EXPERTS, nj),
        in_specs=[pl.BlockSpec(memory_space=pl.ANY),
                  pl.BlockSpec((1, D_MODEL, D_FF_EXPERT), lambda e, j, tok: (e, 0, 0)),
                  pl.BlockSpec((1, D_MODEL, D_FF_EXPERT), lambda e, j, tok: (e, 0, 0)),
                  pl.BlockSpec((1, D_FF_EXPERT, D_MODEL), lambda e, j, tok: (e, 0, 0))],
        out_specs=pl.BlockSpec((1, MOE_SLOT, D_MODEL), lambda e, j, tok: (e, j, 0)),
        scratch_shapes=[pltpu.VMEM((2, GATHER_ROWS, D_MODEL), F32),
                        pltpu.VMEM((MOE_SLOT, D_MODEL), BF16),
                        pltpu.VMEM((MOE_SLOT, D_MODEL), F32),
                        pltpu.SemaphoreType.DMA((2,))])
    return pl.pallas_call(
        functools.partial(_moe_kernel, n_slot_tiles=nj),
        grid_spec=grid_spec,
        out_shape=jax.ShapeDtypeStruct((N_EXPERTS, cap, D_MODEL), BF16),
        compiler_params=_params(("arbitrary", "arbitrary")),
        name="moe",
    )(tok_flat, x1, wg, wu, wd)


CMB_TOK = 256
ROW_CHUNK = 16
STAGE_ROWS = N_EXPERTS * (CMB_TOK + 2 * ROW_CHUNK)
NO_ROW = -(2 ** 30)


def _combine_kernel(off_ref, x1_ref, gid_ref, gate_ref, g_ref, b_ref, eo_ref, o_ref, stage_ref, w_ref, sem,
                    *, cap, n_tok_tiles):
    i = pl.program_id(0)
    cur = i % 2

    def chunk_copy(src_row, dst_row, buf):
        return pltpu.make_async_copy(eo_ref.at[pl.ds(src_row, ROW_CHUNK)],
                                     stage_ref.at[buf, pl.ds(dst_row, ROW_CHUNK)], sem.at[buf])

    def segments(tile):
        base = jnp.int32(0)
        segs = []
        for e in range(N_EXPERTS):
            n0 = off_ref[e * (n_tok_tiles + 1) + tile]
            n1 = off_ref[e * (n_tok_tiles + 1) + tile + 1]
            nch = (n1 + ROW_CHUNK - 1) // ROW_CHUNK - n0 // ROW_CHUNK
            segs.append((base, nch, e * cap + (n0 // ROW_CHUNK) * ROW_CHUNK))
            base = base + nch * ROW_CHUNK
        return segs, base

    def issue(tile, buf):
        segs, _ = segments(tile)
        for seg_base, nch, seg_row in segs:
            def one(c, carry, seg_base=seg_base, seg_row=seg_row):
                src = pl.multiple_of(seg_row + c * ROW_CHUNK, ROW_CHUNK)
                dst = pl.multiple_of(seg_base + c * ROW_CHUNK, ROW_CHUNK)
                chunk_copy(src, dst, buf).start()
                return carry

            lax.fori_loop(0, nch, one, 0)

    @pl.when(i == 0)
    def _():
        stage_ref[...] = jnp.zeros(stage_ref.shape, stage_ref.dtype)
        issue(0, 0)

    @pl.when(i + 1 < n_tok_tiles)
    def _():
        issue(i + 1, 1 - cur)

    segs, total = segments(i)

    def wait(c, carry):
        chunk_copy(0, 0, cur).wait()
        return carry

    lax.fori_loop(0, total // ROW_CHUNK, wait, 0)

    jsub = lax.broadcasted_iota(I32, (SLOT_TILE, CMB_TOK), 0)
    o_ref[...] = ALPHA * x1_ref[...]

    def kchunk(kc, carry):
        j0 = kc * SLOT_TILE
        w_ref[...] = jnp.zeros(w_ref.shape, F32)
        for e, (seg_base, nch, seg_row) in enumerate(segs):
            seg_end = seg_base + nch * ROW_CHUNK

            @pl.when((seg_base < j0 + SLOT_TILE) & (seg_end > j0))
            def _(e=e, seg_base=seg_base, seg_row=seg_row):
                row_id = jsub + (j0 + seg_row - seg_base)
                w_ref[...] = jnp.where(gid_ref[e:e + 1, :] == row_id, gate_ref[e:e + 1, :], w_ref[...])

        rows = stage_ref[cur, pl.ds(pl.multiple_of(j0, SLOT_TILE), SLOT_TILE), :]
        o_ref[...] += lax.dot_general(w_ref[...].astype(BF16), rows, (((0,), (0,)), ((), ())),
                                      preferred_element_type=F32)
        return carry

    lax.fori_loop(0, (total + SLOT_TILE - 1) // SLOT_TILE, kchunk, 0)
    o_ref[...] = _layer_norm(o_ref[...], g_ref[...], b_ref[...])


def _combine(x1, gid_tok, gate_tok, off_flat, eo_flat, g, b, cap):
    T = x1.shape[0]
    nt = T // CMB_TOK
    stage_rows = -(-STAGE_ROWS // SLOT_TILE) * SLOT_TILE
    row = lambda i, off: (i, 0)
    const = lambda i, off: (0, 0)
    grid_spec = pltpu.PrefetchScalarGridSpec(
        num_scalar_prefetch=1,
        grid=(nt,),
        in_specs=[pl.BlockSpec((CMB_TOK, D_MODEL), row),
                  pl.BlockSpec((N_EXPERTS, CMB_TOK), lambda i, off: (0, i)),
                  pl.BlockSpec((N_EXPERTS, CMB_TOK), lambda i, off: (0, i)),
                  pl.BlockSpec((1, D_MODEL), const),
                  pl.BlockSpec((1, D_MODEL), const),
                  pl.BlockSpec(memory_space=pl.ANY)],
        out_specs=pl.BlockSpec((CMB_TOK, D_MODEL), row),
        scratch_shapes=[pltpu.VMEM((2, stage_rows, D_MODEL), BF16),
                        pltpu.VMEM((SLOT_TILE, CMB_TOK), F32),
                        pltpu.SemaphoreType.DMA((2,))])
    return pl.pallas_call(
        functools.partial(_combine_kernel, cap=cap, n_tok_tiles=nt),
        grid_spec=grid_spec,
        out_shape=jax.ShapeDtypeStruct((T, D_MODEL), F32),
        compiler_params=_params(("arbitrary",)),
        name="combine",
    )(off_flat, x1, gid_tok, gate_tok, g.reshape(1, D_MODEL), b.reshape(1, D_MODEL), eo_flat)


def _tile_offsets(off256, cap, T, tok):
    step = tok // SEL_BLK
    o = off256[:, :T // SEL_BLK:step]
    o = jnp.concatenate([o, jnp.full((N_EXPERTS, 1), cap, I32)], axis=1)
    return o.reshape(-1)


def _trunk(x, w_in_p, conv_w, conv_b, dt_bias, a_log, d_skip, ssm_norm_w, w_out_b, ln1_g, ln1_b,
           wr_t, wg, wu, wd, ln2_g, ln2_b):
    B, seq, _ = x.shape
    T = B * seq
    cap = CAPACITY_FACTOR * T // N_EXPERTS
    x2 = x.reshape(T, D_MODEL)
    q, k, v, z, xbc, dt = _inproj(x2, w_in_p, seq)
    shp = (B, seq, ATTN_WIDTH)
    attn = _attention(q.reshape(shp), k.reshape(shp), v.reshape(shp)).reshape(T, ATTN_WIDTH)
    y = _ssd(xbc, dt, z, conv_w, conv_b, dt_bias, a_log, d_skip, ssm_norm_w, B, seq)
    x1, lg_t = _outproj(attn, y, x2, w_out_b, ln1_g, ln1_b, wr_t)
    aff_t, pos, off256 = _select(lg_t, cap)
    tok_flat = _slot_tokens(pos, _tile_offsets(off256, cap, T, MOE_TOK), cap)
    eo = _moe(x1, tok_flat, wg, wu, wd, cap)
    gid = jnp.where(pos >= 0, pos + jnp.arange(N_EXPERTS, dtype=I32)[:, None] * cap, NO_ROW)
    out = _combine(x1, gid, aff_t, _tile_offsets(off256, cap, T, CMB_TOK),
                   eo.reshape(N_EXPERTS * cap, D_MODEL), ln2_g, ln2_b, cap)
    return out.reshape(B, seq, D_MODEL)


def kernel(x_prompt, x_sample, w_in, conv_w, conv_b, dt_bias, a_log, d_skip, ssm_norm_w, w_out, ln1_g, ln1_b,
           w_router, w_gate, w_up, w_down, ln2_g, ln2_b):
    assert DEPTH == 1
    l = 0
    w_in_p = jnp.pad(w_in[l], ((0, 0), (0, IN_PAD - w_in.shape[-1]))).astype(BF16)
    args = (w_in_p, conv_w[l], conv_b[l], dt_bias[l], a_log[l], d_skip[l], ssm_norm_w[l],
            w_out[l].astype(BF16), ln1_g[l], ln1_b[l], w_router[l].T,
            w_gate[l].astype(BF16), w_up[l].astype(BF16), w_down[l].astype(BF16), ln2_g[l], ln2_b[l])
    return (_trunk(x_prompt, *args), _trunk(x_sample, *args))
```

```python
import functools

import jax
import jax.numpy as jnp
from jax import lax
from jax.experimental import pallas as pl
from jax.experimental.pallas import tpu as pltpu

F32 = jnp.float32
BF16 = jnp.bfloat16
I32 = jnp.int32
HIGHEST = lax.Precision.HIGHEST

D_MODEL = 1024
DEPTH = 1
HEAD_DIM = 64
N_ATTN_HEADS = 8
ATTN_WIDTH = N_ATTN_HEADS * HEAD_DIM
ROT_DIM = HEAD_DIM // 4
ROPE_THETA = 500000.0
DILATED_PATTERNS = ((128, 1), (512, 4), (2048, 16))
N_SSM_HEADS = 8
SSM_HEAD_DIM = 64
SSM_WIDTH = N_SSM_HEADS * SSM_HEAD_DIM
SSM_GROUPS = 2
D_STATE = 128
CONV_K = 5
CHUNK = 128
CONV_CH = SSM_WIDTH + 2 * SSM_GROUPS * D_STATE
MIX_WIDTH = ATTN_WIDTH + SSM_WIDTH
N_EXPERTS = 16
CAPACITY_FACTOR = 2
D_FF_EXPERT = 2816
ALPHA = (2.0 * DEPTH) ** 0.25
LN_EPS = 1e-5
RMS_EPS = 1e-5

LANES = 128
VMEM_LIMIT = 56 * 1024 * 1024
NEG = -1e30
HALF_WIN = 64
RES = 4

IN_PAD = 3 * ATTN_WIDTH + SSM_WIDTH + CONV_CH + LANES


def _params(sem):
    return pltpu.CompilerParams(dimension_semantics=sem, vmem_limit_bytes=VMEM_LIMIT)


def _inproj_kernel(x_ref, w_ref, c_ref, sa_ref, sb_ref, q_ref, k_ref, v_ref, z_ref, xbc_ref, dt_ref, tmp_ref):
    x = x_ref[...].astype(BF16)
    tm = x_ref.shape[0]

    def put_grouped(out_ref, val):
        for g in range(ATTN_WIDTH // LANES):
            tmp_ref[g] = val[:, g * LANES:(g + 1) * LANES]
        for r in range(RES):
            for g in range(ATTN_WIDTH // LANES):
                lo = r * ATTN_WIDTH + g * LANES
                out_ref[:, lo:lo + LANES] = tmp_ref[g, pl.ds(r, tm // RES, stride=RES), :]

    def seg(lo, hi):
        return jnp.dot(x, w_ref[:, lo:hi], preferred_element_type=F32)

    c = c_ref[...]
    sa = sa_ref[...]
    sb = sb_ref[...]

    def rotary(t, scale):
        outs = []
        for g in range(ATTN_WIDTH // LANES):
            tg = t[:, g * LANES:(g + 1) * LANES]
            up = pltpu.roll(tg, LANES - ROT_DIM // 2, 1)
            dn = pltpu.roll(tg, ROT_DIM // 2, 1)
            outs.append((tg * c + up * sa + dn * sb) * scale)
        return jnp.concatenate(outs, axis=1)

    a = ATTN_WIDTH
    put_grouped(q_ref, rotary(seg(0, a), HEAD_DIM ** -0.5))
    put_grouped(k_ref, rotary(seg(a, 2 * a), 1.0))
    put_grouped(v_ref, seg(2 * a, 3 * a))
    z_ref[...] = seg(3 * a, 3 * a + SSM_WIDTH)
    o = 3 * a + SSM_WIDTH
    xbc_ref[...] = seg(o, o + CONV_CH)
    dt_ref[...] = seg(o + CONV_CH, o + CONV_CH + LANES)


def _rotary_tables(seq):
    half = ROT_DIM // 2
    inv = ROPE_THETA ** (-jnp.arange(half, dtype=F32) * 2.0 / ROT_DIM)
    ang = jnp.arange(seq).astype(F32)[:, None] * inv[None, :]
    cos = jnp.cos(ang)
    sin = jnp.sin(ang)
    m = jnp.arange(LANES) % HEAD_DIM
    c = jnp.where(m[None, :] < ROT_DIM, cos[:, m % half], 1.0)
    sa = jnp.where(m[None, :] < half, -sin[:, m % half], 0.0)
    sb = jnp.where((m[None, :] >= half) & (m[None, :] < ROT_DIM), sin[:, m % half], 0.0)
    return c.astype(F32), sa.astype(F32), sb.astype(F32)


def _inproj(x2, w_pad, seq, tm=512):
    T = x2.shape[0]
    c, sa, sb = _rotary_tables(seq)
    nseq = seq // tm
    row = lambda i: (i, 0)
    tab = lambda i: (i % nseq, 0)
    outs = [jax.ShapeDtypeStruct((T // RES, RES * ATTN_WIDTH), F32)] * 3 + [
        jax.ShapeDtypeStruct((T, SSM_WIDTH), F32),
        jax.ShapeDtypeStruct((T, CONV_CH), F32),
        jax.ShapeDtypeStruct((T, LANES), F32)]
    return pl.pallas_call(
        _inproj_kernel,
        grid=(T // tm,),
        in_specs=[pl.BlockSpec((tm, D_MODEL), row),
                  pl.BlockSpec((D_MODEL, IN_PAD), lambda i: (0, 0)),
                  pl.BlockSpec((tm, LANES), tab),
                  pl.BlockSpec((tm, LANES), tab),
                  pl.BlockSpec((tm, LANES), tab)],
        out_specs=[pl.BlockSpec((tm // RES, RES * ATTN_WIDTH), row)] * 3 + [
            pl.BlockSpec((tm, SSM_WIDTH), row),
            pl.BlockSpec((tm, CONV_CH), row),
            pl.BlockSpec((tm, LANES), row)],
        out_shape=outs,
        scratch_shapes=[pltpu.VMEM((ATTN_WIDTH // LANES, tm, LANES), F32)],
        compiler_params=_params(("parallel",)),
        name="inproj",
    )(x2, w_pad, c, sa, sb)


QB = 128
KB = 256
ATTN_GROUP = 4


def _attn_kernel(q_hbm, k_hbm, v_hbm, o_ref, qa, ka, va, m_ref, l_ref, acc_ref, bias_ref, sem, *, seq):
    b = pl.program_id(0)
    h = pl.program_id(1)
    sub = seq // RES
    npair = ATTN_WIDTH // LANES

    copies = []
    for src, dst in ((q_hbm, qa), (k_hbm, ka), (v_hbm, va)):
        for r in range(RES):
            lanes = pl.ds(pl.multiple_of((r * npair + h) * LANES, LANES), LANES)
            copies.append(pltpu.make_async_copy(src.at[b, :, lanes], dst.at[pl.ds(r * sub, sub), :], sem))
    for c in copies:
        c.start()

    row = lax.broadcasted_iota(I32, (2 * QB, KB), 0)
    col = lax.broadcasted_iota(I32, (2 * QB, KB), 1)
    iq = jnp.where(row >= QB, row - QB, row)
    iq_runs = RES * (iq % (QB // RES)) + iq // (QB // RES)
    ik_runs = RES * (col % (KB // RES)) + col // (KB // RES)
    for di in range(3):
        bias_ref[0, di] = jnp.where(jnp.abs(di * HALF_WIN + iq - col) <= HALF_WIN, 0.0, NEG).astype(F32)
        bias_ref[1, di] = jnp.where(jnp.abs(di * HALF_WIN + iq_runs - ik_runs) <= HALF_WIN, 0.0, NEG).astype(F32)
    m_ref[...] = jnp.full(m_ref.shape, NEG, F32)
    l_ref[...] = jnp.zeros(l_ref.shape, F32)
    acc_ref[...] = jnp.zeros(acc_ref.shape, F32)
    for c in copies:
        c.wait()

    lane = lax.broadcasted_iota(I32, (QB, LANES), 1)
    head0 = lane < HEAD_DIM
    ones = jnp.ones((KB, LANES), BF16)

    def load(ref, runs, *lead):
        parts = [ref[lead + (rn, slice(None))] for rn in runs]
        return parts[0] if len(parts) == 1 else jnp.concatenate(parts, axis=0)

    def store(ref, runs, val, *lead):
        at = 0
        for rn, size in runs:
            ref[lead + (rn, slice(None))] = val[at:at + size]
            at += size

    for window, dil in DILATED_PATTERNS:
        assert window // (2 * dil) == HALF_WIN and (dil == 1 or dil % RES == 0)
        sub_len = seq // dil
        nblk = sub_len // QB
        assert sub_len >= KB and sub_len % QB == 0 and (dil * nblk) % ATTN_GROUP == 0

        def block_rows(idx, dil=dil, sub_len=sub_len, nblk=nblk):
            r = idx // nblk
            n = idx - r * nblk
            kb = jnp.clip(n * QB - HALF_WIN, 0, sub_len - KB)
            di = (n * QB - kb) // HALF_WIN
            if dil == 1:
                qn, kn = QB // RES, KB // RES
                q_runs = [(pl.ds(pl.multiple_of(c * sub + n * qn, qn), qn), qn) for c in range(RES)]
                k_runs = [pl.ds(pl.multiple_of(c * sub + kb // RES, 8), kn) for c in range(RES)]
                return q_runs, k_runs, bias_ref[1, di]
            step = dil // RES
            base = (r % RES) * sub + r // RES
            if step == 1:
                q_runs = [(pl.ds(pl.multiple_of(base + n * QB, QB), QB), QB)]
                k_runs = [pl.ds(pl.multiple_of(base + kb, HALF_WIN), KB)]
            else:
                q_runs = [(pl.ds(base + step * QB * n, QB, stride=step), QB)]
                k_runs = [pl.ds(base + step * kb, KB, stride=step)]
            return q_runs, k_runs, bias_ref[0, di]

        def body(gi, carry, block_rows=block_rows):
            loaded = []
            for u in range(ATTN_GROUP):
                q_runs, k_runs, bias = block_rows(gi * ATTN_GROUP + u)
                qr = [rn for rn, _ in q_runs]
                loaded.append((q_runs, load(qa, qr), load(ka, k_runs), load(va, k_runs), bias,
                               load(m_ref, qr, 0), load(m_ref, qr, 1), load(l_ref, qr, 0), load(l_ref, qr, 1),
                               load(acc_ref, qr)))
            results = []
            for q_runs, q, k, v, bias, m0, m1, l0, l1, acc in loaded:
                qs = jnp.concatenate([jnp.where(head0, q, 0.0), jnp.where(head0, 0.0, q)], axis=0).astype(BF16)
                s = lax.dot_general(qs, k.astype(BF16), (((1,), (1,)), ((), ())), preferred_element_type=F32)
                s = s + bias
                m_prev = jnp.concatenate([m0, m1], axis=0)
                l_prev = jnp.concatenate([l0, l1], axis=0)
                m_new = jnp.maximum(m_prev, jnp.max(s, axis=1, keepdims=True))
                alpha = jnp.exp(m_prev - m_new)
                p = jnp.exp(s - jnp.concatenate([m_new, m_new], axis=1))
                pv = jnp.dot(p.astype(BF16), jnp.concatenate([v.astype(BF16), ones], axis=1),
                             preferred_element_type=F32)
                l_new = alpha * l_prev + pv[:, LANES:]
                a_l = jnp.where(head0, alpha[:QB], alpha[QB:])
                pv_l = jnp.where(head0, pv[:QB, :LANES], pv[QB:, :LANES])
                results.append((q_runs, a_l * acc + pv_l, m_new, l_new))
            for q_runs, acc_new, m_new, l_new in results:
                store(acc_ref, q_runs, acc_new)
                store(m_ref, q_runs, m_new[:QB], 0)
                store(m_ref, q_runs, m_new[QB:], 1)
                store(l_ref, q_runs, l_new[:QB], 0)
                store(l_ref, q_runs, l_new[QB:], 1)
            return carry

        lax.fori_loop(0, dil * nblk // ATTN_GROUP, body, 0)

    for r in range(RES):
        def fin(i, carry, r=r):
            rws = pl.ds(pl.multiple_of(r * sub + i * QB, QB), QB)
            den = jnp.where(head0, l_ref[0, rws, :], l_ref[1, rws, :])
            o_ref[0, pl.ds(r + RES * QB * i, QB, stride=RES), :] = acc_ref[rws, :] / den
            return carry

        lax.fori_loop(0, sub // QB, fin, 0)


def _attention(q, k, v):
    B, sub, _ = q.shape
    seq = sub * RES
    npair = ATTN_WIDTH // LANES
    any_spec = pl.BlockSpec(memory_space=pl.ANY)
    return pl.pallas_call(
        functools.partial(_attn_kernel, seq=seq),
        grid=(B, npair),
        in_specs=[any_spec, any_spec, any_spec],
        out_specs=pl.BlockSpec((1, seq, LANES), lambda b, h: (b, 0, h)),
        out_shape=jax.ShapeDtypeStruct((B, seq, ATTN_WIDTH), F32),
        scratch_shapes=[pltpu.VMEM((seq, LANES), F32),
                        pltpu.VMEM((seq, LANES), F32),
                        pltpu.VMEM((seq, LANES), F32),
                        pltpu.VMEM((2, seq, LANES), F32),
                        pltpu.VMEM((2, seq, LANES), F32),
                        pltpu.VMEM((seq, LANES), F32),
                        pltpu.VMEM((2, 3, 2 * QB, KB), F32),
                        pltpu.SemaphoreType.DMA(())],
        compiler_params=_params(("parallel", "parallel")),
        name="attn",
    )(q, k, v)


SSD_TILE = 256
HALO = 8


def _ssd_kernel(*refs, reverse):
    if reverse:
        (xbc_ref, hp_ref, hn_ref, dt_ref, cw_ref, cb_ref, dtb_ref, alog_ref,
         yf_ref, z_ref, dsk_ref, nw_ref, y_ref, ext_ref, st_ref) = refs
    else:
        (xbc_ref, hp_ref, hn_ref, dt_ref, cw_ref, cb_ref, dtb_ref, alog_ref,
         y_ref, ext_ref, st_ref) = refs
    lane_off = N_SSM_HEADS if reverse else 0
    nchunk = SSD_TILE // CHUNK

    @pl.when(pl.program_id(1) == 0)
    def _():
        st_ref[...] = jnp.zeros(st_ref.shape, F32)

    ext_ref[0:HALO, :] = hp_ref[0]
    ext_ref[HALO:HALO + SSD_TILE, :] = xbc_ref[...]
    ext_ref[HALO + SSD_TILE:HALO + SSD_TILE + HALO, :] = hn_ref[0]

    li = lax.broadcasted_iota(I32, (CHUNK, CHUNK), 0)
    si = lax.broadcasted_iota(I32, (CHUNK, CHUNK), 1)
    tri = (si >= li) if reverse else (si <= li)
    tri_f = tri.astype(F32)
    ej = lax.broadcasted_iota(I32, (LANES, SSM_WIDTH), 0)
    ec = lax.broadcasted_iota(I32, (LANES, SSM_WIDTH), 1)
    expand = (ej == lane_off + ec // SSM_HEAD_DIM).astype(BF16)
    lane = lax.broadcasted_iota(I32, (CHUNK, LANES), 1)
    lo_half = lane < SSM_HEAD_DIM
    a_row = -jnp.exp(alog_ref[...])
    gw = SSM_WIDTH // SSM_GROUPS
    heads_per_group = N_SSM_HEADS // SSM_GROUPS

    order = range(nchunk - 1, -1, -1) if reverse else range(nchunk)
    for c in order:
        base = HALO + c * CHUNK
        conv = None
        for kk in range(CONV_K):
            tap = ext_ref[pl.ds(base + kk - CONV_K // 2, CHUNK), :] * cw_ref[kk:kk + 1, :]
            conv = tap if conv is None else conv + tap
        u = conv + cb_ref[...]
        u = u / (1.0 + jnp.exp(-u))
        xs = u[:, :SSM_WIDTH]
        bm = u[:, SSM_WIDTH:SSM_WIDTH + SSM_GROUPS * D_STATE]
        cm = u[:, SSM_WIDTH + SSM_GROUPS * D_STATE:]

        dpre = dt_ref[pl.ds(c * CHUNK, CHUNK), :] + dtb_ref[...]
        dtv = jnp.maximum(dpre, 0.0) + jnp.log(1.0 + jnp.exp(-jnp.abs(dpre)))
        a = dtv * a_row
        acs = jnp.dot(tri_f, a, precision=HIGHEST, preferred_element_type=F32)
        acs_t = acs.T
        tot = acs[0:1, :] if reverse else acs[CHUNK - 1:CHUNK, :]
        stacked = jnp.concatenate([dtv, jnp.exp(acs), jnp.exp(tot - acs)], axis=0)
        s_hi = stacked.astype(BF16)
        s_lo = (stacked - s_hi.astype(F32)).astype(BF16)
        ex = (jnp.dot(s_hi, expand, preferred_element_type=F32)
              + jnp.dot(s_lo, expand, preferred_element_type=F32))
        dt_x = ex[:CHUNK]
        eacs_x = ex[CHUNK:2 * CHUNK]
        dend_x = ex[2 * CHUNK:]
        cdec_x = eacs_x[0:1, :] if reverse else eacs_x[CHUNK - 1:CHUNK, :]
        xdt = xs * dt_x
        wst = (dend_x * xdt).astype(BF16)
        st_prev = st_ref[...]
        st_b = st_prev.astype(BF16)

        ygs = []
        st_new = []
        for g in range(SSM_GROUPS):
            ys = []
            bg = bm[:, g * D_STATE:(g + 1) * D_STATE]
            cg = cm[:, g * D_STATE:(g + 1) * D_STATE].astype(BF16)
            cb = lax.dot_general(cg, bg.astype(BF16), (((1,), (1,)), ((), ())), preferred_element_type=F32)
            ms = []
            for hh in range(heads_per_group):
                h = lane_off + g * heads_per_group + hh
                diff = acs[:, h:h + 1] - acs_t[h:h + 1, :]
                lm = jnp.exp(jnp.where(tri, diff, NEG))
                ms.append((cb * lm).astype(BF16))
            for pp in range(heads_per_group // 2):
                hp = g * (heads_per_group // 2) + pp
                lhs = jnp.concatenate([ms[2 * pp], ms[2 * pp + 1]], axis=1)
                xp = xdt[:, hp * LANES:(hp + 1) * LANES]
                rhs = jnp.concatenate([jnp.where(lo_half, xp, 0.0), jnp.where(lo_half, 0.0, xp)],
                                      axis=0).astype(BF16)
                ys.append(jnp.dot(lhs, rhs, preferred_element_type=F32))
            sc = jnp.dot(bg.T.astype(BF16), wst[:, g * gw:(g + 1) * gw], preferred_element_type=F32)
            yoff = jnp.dot(cg, st_b[:, g * gw:(g + 1) * gw], preferred_element_type=F32)
            ygs.append(jnp.concatenate(ys, axis=1) + yoff * eacs_x[:, g * gw:(g + 1) * gw])
            st_new.append(st_prev[:, g * gw:(g + 1) * gw] * cdec_x[:, g * gw:(g + 1) * gw] + sc)
        st_ref[...] = jnp.concatenate(st_new, axis=1)
        y = jnp.concatenate(ygs, axis=1)

        crow = pl.ds(c * CHUNK, CHUNK)
        if reverse:
            ytot = yf_ref[crow, :] + y + dsk_ref[...] * xs
            zz = z_ref[crow, :]
            yz = ytot * (zz / (1.0 + jnp.exp(-zz)))
            yn = yz * lax.rsqrt(jnp.mean(jnp.square(yz), axis=-1, keepdims=True) + RMS_EPS) * nw_ref[...]
            y_ref[crow, :] = yn.astype(y_ref.dtype)
        else:
            y_ref[crow, :] = y


def _ssd(xbc, dt, z, conv_w, conv_b, dt_bias, a_log, d_skip, ssm_norm_w, B, seq):
    T = B * seq
    nt = seq // SSD_TILE
    x4 = xbc.reshape(B, nt, SSD_TILE, CONV_CH)
    zeros = jnp.zeros((B, 1, HALO, CONV_CH), F32)
    hprev = jnp.concatenate([zeros, x4[:, :-1, SSD_TILE - HALO:, :]], axis=1).reshape(B * nt, HALO, CONV_CH)
    hnext = jnp.concatenate([x4[:, 1:, :HALO, :], zeros], axis=1).reshape(B * nt, HALO, CONV_CH)
    cw = jnp.zeros((HALO, CONV_CH), F32).at[:CONV_K].set(conv_w)
    cb = conv_b.reshape(1, CONV_CH)
    pad = LANES - 2 * N_SSM_HEADS
    dtb = jnp.pad(dt_bias.reshape(1, 2 * N_SSM_HEADS), ((0, 0), (0, pad)))
    alog = jnp.pad(a_log.reshape(1, 2 * N_SSM_HEADS), ((0, 0), (0, pad)), constant_values=-1e30)
    dsk = jnp.repeat(d_skip, SSM_HEAD_DIM).reshape(1, SSM_WIDTH)
    nw = ssm_norm_w.reshape(1, SSM_WIDTH)

    def call(reverse, extra_in, extra_specs, out_dtype):
        if reverse:
            tmap = lambda b, i: (b * nt + nt - 1 - i, 0)
            hmap = lambda b, i: (b * nt + nt - 1 - i, 0, 0)
        else:
            tmap = lambda b, i: (b * nt + i, 0)
            hmap = lambda b, i: (b * nt + i, 0, 0)
        const = lambda b, i: (0, 0)
        in_specs = [pl.BlockSpec((SSD_TILE, CONV_CH), tmap),
                    pl.BlockSpec((1, HALO, CONV_CH), hmap),
                    pl.BlockSpec((1, HALO, CONV_CH), hmap),
                    pl.BlockSpec((SSD_TILE, LANES), tmap),
                    pl.BlockSpec((HALO, CONV_CH), const),
                    pl.BlockSpec((1, CONV_CH), const),
                    pl.BlockSpec((1, LANES), const),
                    pl.BlockSpec((1, LANES), const)] + [s(tmap, const) for s in extra_specs]
        return pl.pallas_call(
            functools.partial(_ssd_kernel, reverse=reverse),
            grid=(B, nt),
            in_specs=in_specs,
            out_specs=pl.BlockSpec((SSD_TILE, SSM_WIDTH), tmap),
            out_shape=jax.ShapeDtypeStruct((T, SSM_WIDTH), out_dtype),
            scratch_shapes=[pltpu.VMEM((SSD_TILE + 2 * HALO, CONV_CH), F32),
                            pltpu.VMEM((D_STATE, SSM_WIDTH), F32)],
            compiler_params=_params(("parallel", "arbitrary")),
            name="ssd_bwd" if reverse else "ssd_fwd",
        )(xbc, hprev, hnext, dt, cw, cb, dtb, alog, *extra_in)

    yf = call(False, [], [], F32)
    tile = lambda tmap, const: pl.BlockSpec((SSD_TILE, SSM_WIDTH), tmap)
    rowc = lambda tmap, const: pl.BlockSpec((1, SSM_WIDTH), const)
    return call(True, [yf, z, dsk, nw], [tile, tile, rowc, rowc], BF16)


def _layer_norm(h, g, b):
    mu = jnp.mean(h, axis=-1, keepdims=True)
    var = jnp.mean(jnp.square(h - mu), axis=-1, keepdims=True)
    return (h - mu) * lax.rsqrt(var + LN_EPS) * g + b


def _outproj_kernel(a_ref, y_ref, x_ref, w_ref, g_ref, b_ref, wr_ref, x1_ref, lg_ref):
    mix = jnp.dot(a_ref[...].astype(BF16), w_ref[:ATTN_WIDTH, :], preferred_element_type=F32)
    mix = mix + jnp.dot(y_ref[...], w_ref[ATTN_WIDTH:, :], preferred_element_type=F32)
    x1 = _layer_norm(ALPHA * x_ref[...] + mix, g_ref[...], b_ref[...])
    x1_ref[...] = x1
    lg_ref[...] = lax.dot_general(wr_ref[...], x1, (((1,), (1,)), ((), ())),
                                  precision=HIGHEST, preferred_element_type=F32)


def _outproj(attn, y, x2, w_out_b, g, b, wr_t, tm=512):
    T = x2.shape[0]
    row = lambda i: (i, 0)
    const = lambda i: (0, 0)
    return pl.pallas_call(
        _outproj_kernel,
        grid=(T // tm,),
        in_specs=[pl.BlockSpec((tm, ATTN_WIDTH), row),
                  pl.BlockSpec((tm, SSM_WIDTH), row),
                  pl.BlockSpec((tm, D_MODEL), row),
                  pl.BlockSpec((MIX_WIDTH, D_MODEL), const),
                  pl.BlockSpec((1, D_MODEL), const),
                  pl.BlockSpec((1, D_MODEL), const),
                  pl.BlockSpec((N_EXPERTS, D_MODEL), const)],
        out_specs=[pl.BlockSpec((tm, D_MODEL), row),
                   pl.BlockSpec((N_EXPERTS, tm), lambda i: (0, i))],
        out_shape=[jax.ShapeDtypeStruct((T, D_MODEL), F32),
                   jax.ShapeDtypeStruct((N_EXPERTS, T), F32)],
        compiler_params=_params(("parallel",)),
        name="outproj",
    )(attn, y, x2, w_out_b, g.reshape(1, D_MODEL), b.reshape(1, D_MODEL), wr_t)


SEL_BLK = 256


MIN_NORMAL_BITS = 0x00800000


def _select_kernel(lg_ref, aff_ref, pos_ref, off_ref, res_ref, *, cap, T):
    lg = lg_ref[...]
    ex = jnp.exp(lg - jnp.max(lg, axis=0, keepdims=True))
    aff_ref[...] = ex / jnp.sum(ex, axis=0, keepdims=True)

    def as_f32(bits):
        return lax.bitcast_convert_type(bits, F32)

    def kth_largest(ref):
        def search(i, t):
            cand = t | jnp.left_shift(jnp.int32(1), 30 - i)
            cnt = jnp.sum((ref[...] >= as_f32(cand)).astype(I32), axis=1, keepdims=True)
            return jnp.where((cnt >= cap) & (cand >= MIN_NORMAL_BITS), cand, t)

        return lax.fori_loop(0, 31, search, jnp.zeros((N_EXPERTS, 1), I32))

    thr1 = as_f32(kth_largest(aff_ref))
    res_ref[...] = aff_ref[...] - thr1
    thr2_bits = kth_largest(res_ref)
    thr2 = as_f32(thr2_bits)
    nxt2 = as_f32(jnp.where(thr2_bits == 0, MIN_NORMAL_BITS, thr2_bits + 1))
    n_gt = jnp.sum((res_ref[...] >= nxt2).astype(I32), axis=1, keepdims=True)
    need = (cap - n_gt).astype(F32)

    uj = lax.broadcasted_iota(I32, (SEL_BLK, SEL_BLK), 0)
    ut = lax.broadcasted_iota(I32, (SEL_BLK, SEL_BLK), 1)
    upper = (uj <= ut).astype(BF16)
    nblk = T // SEL_BLK
    olane = lax.broadcasted_iota(I32, off_ref.shape, 1)

    def blk(i, carry):
        c_gt, c_eq, offs = carry
        cols = pl.ds(pl.multiple_of(i * SEL_BLK, SEL_BLK), SEL_BLK)
        res = res_ref[:, cols]
        gt = res >= nxt2
        eq = (res >= thr2) & jnp.logical_not(gt)
        gt_f = gt.astype(F32)
        eq_f = eq.astype(F32)
        st = jnp.concatenate([gt_f, eq_f], axis=0).astype(BF16)
        cs = jnp.dot(st, upper, preferred_element_type=F32)
        gt_ex = c_gt + cs[:N_EXPERTS] - gt_f
        eq_ex = c_eq + cs[N_EXPERTS:] - eq_f
        sel = gt | (eq & (eq_ex < need))
        slot = gt_ex + jnp.minimum(eq_ex, need)
        pos_ref[:, cols] = jnp.where(sel, slot, -1.0).astype(I32)
        start = (c_gt + jnp.minimum(c_eq, need)).astype(I32)
        offs = jnp.where(olane == i, start, offs)
        return (c_gt + cs[:N_EXPERTS, SEL_BLK - 1:SEL_BLK], c_eq + cs[N_EXPERTS:, SEL_BLK - 1:SEL_BLK], offs)

    zero = jnp.zeros((N_EXPERTS, 1), F32)
    _, _, offs = lax.fori_loop(0, nblk, blk, (zero, zero, jnp.zeros(off_ref.shape, I32)))
    off_ref[...] = offs


def _select(lg_t, cap):
    T = lg_t.shape[1]
    nblk = T // SEL_BLK
    owidth = -(-nblk // LANES) * LANES
    return pl.pallas_call(
        functools.partial(_select_kernel, cap=cap, T=T),
        out_shape=[jax.ShapeDtypeStruct((N_EXPERTS, T), F32),
                   jax.ShapeDtypeStruct((N_EXPERTS, T), I32),
                   jax.ShapeDtypeStruct((N_EXPERTS, owidth), I32)],
        scratch_shapes=[pltpu.VMEM((N_EXPERTS, T), F32)],
        compiler_params=pltpu.CompilerParams(vmem_limit_bytes=VMEM_LIMIT),
        name="select",
    )(lg_t)


SLOT_TILE = 256
MOE_TOK = 512
IDX_BASE = 256
SLOT_SUBTILES = 4


def _slots_kernel(off_ref, pos_ref, idx_ref, digits_ref, *, n_tok_tiles):
    e = pl.program_id(0)
    g = pl.program_id(1)

    @pl.when(g == 0)
    def _():
        idx_ref[...] = jnp.zeros(idx_ref.shape, F32)

    for s in range(SLOT_SUBTILES):
        i = g * SLOT_SUBTILES + s
        n0 = off_ref[e * (n_tok_tiles + 1) + i]
        n1 = off_ref[e * (n_tok_tiles + 1) + i + 1]
        j0 = n0 // SLOT_TILE
        n_tiles = jnp.where(n1 > n0, (n1 - 1) // SLOT_TILE - j0 + 1, 0)
        pos_row = pos_ref[pl.ds(e, 1), s * MOE_TOK:(s + 1) * MOE_TOK]
        tok = i * MOE_TOK + lax.broadcasted_iota(I32, (MOE_TOK, LANES), 0)
        col = lax.broadcasted_iota(I32, (MOE_TOK, LANES), 1)
        hi = jnp.right_shift(tok, IDX_BASE.bit_length() - 1)
        lo = jnp.bitwise_and(tok, IDX_BASE - 1)
        digits_ref[...] = jnp.where(col == 0, hi, jnp.where(col == 1, lo, 0)).astype(F32).astype(BF16)

        def tile(jj, carry, j0=j0, pos_row=pos_row):
            j = j0 + jj
            sub = lax.broadcasted_iota(I32, (SLOT_TILE, MOE_TOK), 0)
            onehot = jnp.where(pos_row - j * SLOT_TILE == sub, 1.0, 0.0).astype(BF16)
            rows = pl.ds(pl.multiple_of(j * SLOT_TILE, SLOT_TILE), SLOT_TILE)
            idx_ref[0, rows, :] += jnp.dot(onehot, digits_ref[...], preferred_element_type=F32)
            return carry

        lax.fori_loop(0, n_tiles, tile, 0)


def _slot_tokens(pos, off_flat, cap):
    T = pos.shape[1]
    nt = T // MOE_TOK
    assert T // IDX_BASE <= IDX_BASE and nt % SLOT_SUBTILES == 0
    grid_spec = pltpu.PrefetchScalarGridSpec(
        num_scalar_prefetch=1,
        grid=(N_EXPERTS, nt // SLOT_SUBTILES),
        in_specs=[pl.BlockSpec((N_EXPERTS, SLOT_SUBTILES * MOE_TOK), lambda e, g, off: (0, g))],
        out_specs=pl.BlockSpec((1, cap, LANES), lambda e, i, off: (e, 0, 0)),
        scratch_shapes=[pltpu.VMEM((MOE_TOK, LANES), BF16)])
    idx = pl.pallas_call(
        functools.partial(_slots_kernel, n_tok_tiles=nt),
        grid_spec=grid_spec,
        out_shape=jax.ShapeDtypeStruct((N_EXPERTS, cap, LANES), F32),
        compiler_params=_params(("arbitrary", "arbitrary")),
        name="slots",
    )(off_flat, pos)
    return (idx[:, :, 0] * IDX_BASE + idx[:, :, 1]).astype(I32).reshape(-1)


MOE_SLOT = 512
FF_BLK = 256
N_FF = D_FF_EXPERT // FF_BLK
ROWS_PER_FF = 48
GATHER_ROWS = ROWS_PER_FF * N_FF
assert D_FF_EXPERT % FF_BLK == 0 and GATHER_ROWS >= MOE_SLOT and GATHER_ROWS % 8 == 0


def _moe_kernel(tok_ref, x_hbm, wg_ref, wu_ref, wd_ref, out_ref, xg_ref, xb_ref, acc_ref, sem, *, n_slot_tiles):
    step = pl.program_id(0) * n_slot_tiles + pl.program_id(1)
    total = N_EXPERTS * n_slot_tiles
    cur = step % 2

    def row_copy(tile_idx, r, buf):
        slot = jnp.minimum(r, MOE_SLOT - 1)
        tok = tok_ref[tile_idx * MOE_SLOT + slot]
        return pltpu.make_async_copy(x_hbm.at[pl.ds(tok, 1)], xg_ref.at[buf, pl.ds(r, 1)], sem.at[buf])

    def wait_tile(buf):
        pltpu.make_async_copy(x_hbm.at[pl.ds(0, GATHER_ROWS)], xg_ref.at[buf], sem.at[buf]).wait()

    @pl.when(step == 0)
    def _():
        def first(r, carry):
            row_copy(0, r, 0).start()
            return carry

        lax.fori_loop(0, GATHER_ROWS, first, 0)

    wait_tile(cur)
    nxt = (step + 1) % total
    xb_ref[...] = xg_ref[cur, :MOE_SLOT, :].astype(BF16)
    acc_ref[...] = jnp.zeros(acc_ref.shape, F32)

    def ff(c, carry):
        for r in range(ROWS_PER_FF):
            row_copy(nxt, c * ROWS_PER_FF + r, 1 - cur).start()
        cols = pl.ds(pl.multiple_of(c * FF_BLK, FF_BLK), FF_BLK)
        xb = xb_ref[...]
        g = jnp.dot(xb, wg_ref[0, :, cols], preferred_element_type=F32)
        u = jnp.dot(xb, wu_ref[0, :, cols], preferred_element_type=F32)
        h = ((g / (1.0 + jnp.exp(-g))) * u).astype(BF16)
        acc_ref[...] += jnp.dot(h, wd_ref[0, cols, :], preferred_element_type=F32)
        return carry

    lax.fori_loop(0, N_FF, ff, 0)
    out_ref[0] = acc_ref[...].astype(out_ref.dtype)

    @pl.when(step == total - 1)
    def _():
        wait_tile(1 - cur)


def _moe(x1, tok_flat, wg, wu, wd, cap):
    assert cap % MOE_SLOT == 0
    nj = cap // MOE_SLOT
    grid_spec = pltpu.PrefetchScalarGridSpec(
        num_scalar_prefetch=1,
        grid=(N_EXPERTS, nj),
        in_specs=[pl.BlockSpec(memory_space=pl.ANY),
                  pl.BlockSpec((1, D_MODEL, D_FF_EXPERT), lambda e, j, tok: (e, 0, 0)),
                  pl.BlockSpec((1, D_MODEL, D_FF_EXPERT), lambda e, j, tok: (e, 0, 0)),
                  pl.BlockSpec((1, D_FF_EXPERT, D_MODEL), lambda e, j, tok: (e, 0, 0))],
        out_specs=pl.BlockSpec((1, MOE_SLOT, D_MODEL), lambda e, j, tok: (e, j, 0)),
        scratch_shapes=[pltpu.VMEM((2, GATHER_ROWS, D_MODEL), F32),
                        pltpu.VMEM((MOE_SLOT, D_MODEL), BF16),
                        pltpu.VMEM((MOE_SLOT, D_MODEL), F32),
                        pltpu.SemaphoreType.DMA((2,))])
    return pl.pallas_call(
        functools.partial(_moe_kernel, n_slot_tiles=nj),
        grid_spec=grid_spec,
        out_shape=jax.ShapeDtypeStruct((N_EXPERTS, cap, D_MODEL), BF16),
        compiler_params=_params(("arbitrary", "arbitrary")),
        name="moe",
    )(tok_flat, x1, wg, wu, wd)


CMB_TOK = 256
ROW_CHUNK = 16
STAGE_ROWS = N_EXPERTS * (CMB_TOK + 2 * ROW_CHUNK)
NO_ROW = -(2 ** 30)


def _combine_kernel(off_ref, x1_ref, gid_ref, gate_ref, g_ref, b_ref, eo_ref, o_ref, stage_ref, w_ref, sem,
                    *, cap, n_tok_tiles):
    i = pl.program_id(0)
    cur = i % 2

    def chunk_copy(src_row, dst_row, buf):
        return pltpu.make_async_copy(eo_ref.at[pl.ds(src_row, ROW_CHUNK)],
                                     stage_ref.at[buf, pl.ds(dst_row, ROW_CHUNK)], sem.at[buf])

    def segments(tile):
        base = jnp.int32(0)
        segs = []
        for e in range(N_EXPERTS):
            n0 = off_ref[e * (n_tok_tiles + 1) + tile]
            n1 = off_ref[e * (n_tok_tiles + 1) + tile + 1]
            nch = (n1 + ROW_CHUNK - 1) // ROW_CHUNK - n0 // ROW_CHUNK
            segs.append((base, nch, e * cap + (n0 // ROW_CHUNK) * ROW_CHUNK))
            base = base + nch * ROW_CHUNK
        return segs, base

    def issue(tile, buf):
        segs, _ = segments(tile)
        for seg_base, nch, seg_row in segs:
            def one(c, carry, seg_base=seg_base, seg_row=seg_row):
                src = pl.multiple_of(seg_row + c * ROW_CHUNK, ROW_CHUNK)
                dst = pl.multiple_of(seg_base + c * ROW_CHUNK, ROW_CHUNK)
                chunk_copy(src, dst, buf).start()
                return carry

            lax.fori_loop(0, nch, one, 0)

    @pl.when(i == 0)
    def _():
        stage_ref[...] = jnp.zeros(stage_ref.shape, stage_ref.dtype)
        issue(0, 0)

    @pl.when(i + 1 < n_tok_tiles)
    def _():
        issue(i + 1, 1 - cur)

    segs, total = segments(i)

    def wait(c, carry):
        chunk_copy(0, 0, cur).wait()
        return carry

    lax.fori_loop(0, total // ROW_CHUNK, wait, 0)

    jsub = lax.broadcasted_iota(I32, (SLOT_TILE, CMB_TOK), 0)
    o_ref[...] = ALPHA * x1_ref[...]

    def kchunk(kc, carry):
        j0 = kc * SLOT_TILE
        w_ref[...] = jnp.zeros(w_ref.shape, F32)
        for e, (seg_base, nch, seg_row) in enumerate(segs):
            seg_end = seg_base + nch * ROW_CHUNK

            @pl.when((seg_base < j0 + SLOT_TILE) & (seg_end > j0))
            def _(e=e, seg_base=seg_base, seg_row=seg_row):
                row_id = jsub + (j0 + seg_row - seg_base)
                w_ref[...] = jnp.where(gid_ref[e:e + 1, :] == row_id, gate_ref[e:e + 1, :], w_ref[...])

        rows = stage_ref[cur, pl.ds(pl.multiple_of(j0, SLOT_TILE), SLOT_TILE), :]
        o_ref[...] += lax.dot_general(w_ref[...].astype(BF16), rows, (((0,), (0,)), ((), ())),
                                      preferred_element_type=F32)
        return carry

    lax.fori_loop(0, (total + SLOT_TILE - 1) // SLOT_TILE, kchunk, 0)
    o_ref[...] = _layer_norm(o_ref[...], g_ref[...], b_ref[...])


def _combine(x1, gid_tok, gate_tok, off_flat, eo_flat, g, b, cap):
    T = x1.shape[0]
    nt = T // CMB_TOK
    stage_rows = -(-STAGE_ROWS // SLOT_TILE) * SLOT_TILE
    row = lambda i, off: (i, 0)
    const = lambda i, off: (0, 0)
    grid_spec = pltpu.PrefetchScalarGridSpec(
        num_scalar_prefetch=1,
        grid=(nt,),
        in_specs=[pl.BlockSpec((CMB_TOK, D_MODEL), row),
                  pl.BlockSpec((N_EXPERTS, CMB_TOK), lambda i, off: (0, i)),
                  pl.BlockSpec((N_EXPERTS, CMB_TOK), lambda i, off: (0, i)),
                  pl.BlockSpec((1, D_MODEL), const),
                  pl.BlockSpec((1, D_MODEL), const),
                  pl.BlockSpec(memory_space=pl.ANY)],
        out_specs=pl.BlockSpec((CMB_TOK, D_MODEL), row),
        scratch_shapes=[pltpu.VMEM((2, stage_rows, D_MODEL), BF16),
                        pltpu.VMEM((SLOT_TILE, CMB_TOK), F32),
                        pltpu.SemaphoreType.DMA((2,))])
    return pl.pallas_call(
        functools.partial(_combine_kernel, cap=cap, n_tok_tiles=nt),
        grid_spec=grid_spec,
        out_shape=jax.ShapeDtypeStruct((T, D_MODEL), F32),
        compiler_params=_params(("arbitrary",)),
        name="combine",
    )(off_flat, x1, gid_tok, gate_tok, g.reshape(1, D_MODEL), b.reshape(1, D_MODEL), eo_flat)


def _tile_offsets(off256, cap, T, tok):
    step = tok // SEL_BLK
    o = off256[:, :T // SEL_BLK:step]
    o = jnp.concatenate([o, jnp.full((N_EXPERTS, 1), cap, I32)], axis=1)
    return o.reshape(-1)


def _trunk(x, w_in_p, conv_w, conv_b, dt_bias, a_log, d_skip, ssm_norm_w, w_out_b, ln1_g, ln1_b,
           wr_t, wg, wu, wd, ln2_g, ln2_b):
    B, seq, _ = x.shape
    T = B * seq
    cap = CAPACITY_FACTOR * T // N_EXPERTS
    x2 = x.reshape(T, D_MODEL)
    q, k, v, z, xbc, dt = _inproj(x2, w_in_p, seq)
    shp = (B, seq // RES, RES * ATTN_WIDTH)
    attn = _attention(q.reshape(shp), k.reshape(shp), v.reshape(shp)).reshape(T, ATTN_WIDTH)
    y = _ssd(xbc, dt, z, conv_w, conv_b, dt_bias, a_log, d_skip, ssm_norm_w, B, seq)
    x1, lg_t = _outproj(attn, y, x2, w_out_b, ln1_g, ln1_b, wr_t)
    aff_t, pos, off256 = _select(lg_t, cap)
    tok_flat = _slot_tokens(pos, _tile_offsets(off256, cap, T, MOE_TOK), cap)
    eo = _moe(x1, tok_flat, wg, wu, wd, cap)
    gid = jnp.where(pos >= 0, pos + jnp.arange(N_EXPERTS, dtype=I32)[:, None] * cap, NO_ROW)
    out = _combine(x1, gid, aff_t, _tile_offsets(off256, cap, T, CMB_TOK),
                   eo.reshape(N_EXPERTS * cap, D_MODEL), ln2_g, ln2_b, cap)
    return out.reshape(B, seq, D_MODEL)


def kernel(x_prompt, x_sample, w_in, conv_w, conv_b, dt_bias, a_log, d_skip, ssm_norm_w, w_out, ln1_g, ln1_b,
           w_router, w_gate, w_up, w_down, ln2_g, ln2_b):
    assert DEPTH == 1
    l = 0
    w_in_p = jnp.pad(w_in[l], ((0, 0), (0, IN_PAD - w_in.shape[-1]))).astype(BF16)
    args = (w_in_p, conv_w[l], conv_b[l], dt_bias[l], a_log[l], d_skip[l], ssm_norm_w[l],
            w_out[l].astype(BF16), ln1_g[l], ln1_b[l], w_router[l].T,
            w_gate[l].astype(BF16), w_up[l].astype(BF16), w_down[l].astype(BF16), ln2_g[l], ln2_b[l])
    return (_trunk(x_prompt, *args), _trunk(x_sample, *args))
```

```python
import functools

import jax
import jax.numpy as jnp
from jax import lax
from jax.experimental import pallas as pl
from jax.experimental.pallas import tpu as pltpu

F32 = jnp.float32
BF16 = jnp.bfloat16
I32 = jnp.int32
HIGHEST = lax.Precision.HIGHEST

D_MODEL = 1024
DEPTH = 1
HEAD_DIM = 64
N_ATTN_HEADS = 8
ATTN_WIDTH = N_ATTN_HEADS * HEAD_DIM
ROT_DIM = HEAD_DIM // 4
ROPE_THETA = 500000.0
DILATED_PATTERNS = ((128, 1), (512, 4), (2048, 16))
N_SSM_HEADS = 8
SSM_HEAD_DIM = 64
SSM_WIDTH = N_SSM_HEADS * SSM_HEAD_DIM
SSM_GROUPS = 2
D_STATE = 128
CONV_K = 5
CHUNK = 128
CONV_CH = SSM_WIDTH + 2 * SSM_GROUPS * D_STATE
MIX_WIDTH = ATTN_WIDTH + SSM_WIDTH
N_EXPERTS = 16
CAPACITY_FACTOR = 2
D_FF_EXPERT = 2816
ALPHA = (2.0 * DEPTH) ** 0.25
LN_EPS = 1e-5
RMS_EPS = 1e-5

LANES = 128
VMEM_LIMIT = 56 * 1024 * 1024
NEG = -1e30
HALF_WIN = 64
RES = 4

IN_PAD = 3 * ATTN_WIDTH + SSM_WIDTH + CONV_CH + LANES


def _params(sem):
    return pltpu.CompilerParams(dimension_semantics=sem, vmem_limit_bytes=VMEM_LIMIT)


def _inproj_kernel(x_ref, w_ref, c_ref, sa_ref, sb_ref, q_ref, k_ref, v_ref, z_ref, xbc_ref, dt_ref, tmp_ref):
    x = x_ref[...].astype(BF16)
    tm = x_ref.shape[0]

    def put_grouped(out_ref, val):
        for g in range(ATTN_WIDTH // LANES):
            tmp_ref[g] = val[:, g * LANES:(g + 1) * LANES]
        for r in range(RES):
            for g in range(ATTN_WIDTH // LANES):
                lo = r * ATTN_WIDTH + g * LANES
                out_ref[:, lo:lo + LANES] = tmp_ref[g, pl.ds(r, tm // RES, stride=RES), :]

    def seg(lo, hi):
        return jnp.dot(x, w_ref[:, lo:hi], preferred_element_type=F32)

    c = c_ref[...]
    sa = sa_ref[...]
    sb = sb_ref[...]

    def rotary(t, scale):
        outs = []
        for g in range(ATTN_WIDTH // LANES):
            tg = t[:, g * LANES:(g + 1) * LANES]
            up = pltpu.roll(tg, LANES - ROT_DIM // 2, 1)
            dn = pltpu.roll(tg, ROT_DIM // 2, 1)
            outs.append((tg * c + up * sa + dn * sb) * scale)
        return jnp.concatenate(outs, axis=1)

    a = ATTN_WIDTH
    put_grouped(q_ref, rotary(seg(0, a), HEAD_DIM ** -0.5))
    put_grouped(k_ref, rotary(seg(a, 2 * a), 1.0))
    put_grouped(v_ref, seg(2 * a, 3 * a))
    z_ref[...] = seg(3 * a, 3 * a + SSM_WIDTH)
    o = 3 * a + SSM_WIDTH
    xbc_ref[...] = seg(o, o + CONV_CH)
    dt_ref[...] = seg(o + CONV_CH, o + CONV_CH + LANES)


def _rotary_tables(seq):
    half = ROT_DIM // 2
    inv = ROPE_THETA ** (-jnp.arange(half, dtype=F32) * 2.0 / ROT_DIM)
    ang = jnp.arange(seq).astype(F32)[:, None] * inv[None, :]
    cos = jnp.cos(ang)
    sin = jnp.sin(ang)
    m = jnp.arange(LANES) % HEAD_DIM
    c = jnp.where(m[None, :] < ROT_DIM, cos[:, m % half], 1.0)
    sa = jnp.where(m[None, :] < half, -sin[:, m % half], 0.0)
    sb = jnp.where((m[None, :] >= half) & (m[None, :] < ROT_DIM), sin[:, m % half], 0.0)
    return c.astype(F32), sa.astype(F32), sb.astype(F32)


def _inproj(x2, w_pad, seq, tm=512):
    T = x2.shape[0]
    c, sa, sb = _rotary_tables(seq)
    nseq = seq // tm
    row = lambda i: (i, 0)
    tab = lambda i: (i % nseq, 0)
    outs = [jax.ShapeDtypeStruct((T // RES, RES * ATTN_WIDTH), F32)] * 3 + [
        jax.ShapeDtypeStruct((T, SSM_WIDTH), F32),
        jax.ShapeDtypeStruct((T, CONV_CH), F32),
        jax.ShapeDtypeStruct((T, LANES), F32)]
    return pl.pallas_call(
        _inproj_kernel,
        grid=(T // tm,),
        in_specs=[pl.BlockSpec((tm, D_MODEL), row),
                  pl.BlockSpec((D_MODEL, IN_PAD), lambda i: (0, 0)),
                  pl.BlockSpec((tm, LANES), tab),
                  pl.BlockSpec((tm, LANES), tab),
                  pl.BlockSpec((tm, LANES), tab)],
        out_specs=[pl.BlockSpec((tm // RES, RES * ATTN_WIDTH), row)] * 3 + [
            pl.BlockSpec((tm, SSM_WIDTH), row),
            pl.BlockSpec((tm, CONV_CH), row),
            pl.BlockSpec((tm, LANES), row)],
        out_shape=outs,
        scratch_shapes=[pltpu.VMEM((ATTN_WIDTH // LANES, tm, LANES), F32)],
        compiler_params=_params(("parallel",)),
        name="inproj",
    )(x2, w_pad, c, sa, sb)


QB = 128
KB = 256
ATTN_GROUP = 8


def _attn_kernel(q_hbm, k_hbm, v_hbm, o_ref, qa, ka, va, m_ref, l_ref, acc_ref, bias_ref, sem, *, seq):
    b = pl.program_id(0)
    h = pl.program_id(1)
    sub = seq // RES
    npair = ATTN_WIDTH // LANES

    copies = []
    for src, dst in ((q_hbm, qa), (k_hbm, ka), (v_hbm, va)):
        for r in range(RES):
            lanes = pl.ds(pl.multiple_of((r * npair + h) * LANES, LANES), LANES)
            copies.append(pltpu.make_async_copy(src.at[b, :, lanes], dst.at[pl.ds(r * sub, sub), :], sem))
    for c in copies:
        c.start()

    row = lax.broadcasted_iota(I32, (2 * QB, KB), 0)
    col = lax.broadcasted_iota(I32, (2 * QB, KB), 1)
    iq = jnp.where(row >= QB, row - QB, row)
    iq_runs = RES * (iq % (QB // RES)) + iq // (QB // RES)
    ik_runs = RES * (col % (KB // RES)) + col // (KB // RES)
    for di in range(3):
        bias_ref[0, di] = jnp.where(jnp.abs(di * HALF_WIN + iq - col) <= HALF_WIN, 0.0, NEG).astype(F32)
        bias_ref[1, di] = jnp.where(jnp.abs(di * HALF_WIN + iq_runs - ik_runs) <= HALF_WIN, 0.0, NEG).astype(F32)
    m_ref[...] = jnp.full(m_ref.shape, NEG, F32)
    l_ref[...] = jnp.zeros(l_ref.shape, F32)
    acc_ref[...] = jnp.zeros(acc_ref.shape, F32)
    for c in copies:
        c.wait()

    lane = lax.broadcasted_iota(I32, (QB, LANES), 1)
    head0 = lane < HEAD_DIM
    ones = jnp.ones((KB, LANES), BF16)

    def load(ref, runs, *lead):
        parts = [ref[lead + (rn, slice(None))] for rn in runs]
        return parts[0] if len(parts) == 1 else jnp.concatenate(parts, axis=0)

    def store(ref, runs, val, *lead):
        at = 0
        for rn, size in runs:
            ref[lead + (rn, slice(None))] = val[at:at + size]
            at += size

    for window, dil in DILATED_PATTERNS:
        assert window // (2 * dil) == HALF_WIN and (dil == 1 or dil % RES == 0)
        sub_len = seq // dil
        nblk = sub_len // QB
        assert sub_len >= KB and sub_len % QB == 0 and (dil * nblk) % ATTN_GROUP == 0

        def block_rows(idx, dil=dil, sub_len=sub_len, nblk=nblk):
            r = idx // nblk
            n = idx - r * nblk
            kb = jnp.clip(n * QB - HALF_WIN, 0, sub_len - KB)
            di = (n * QB - kb) // HALF_WIN
            if dil == 1:
                qn, kn = QB // RES, KB // RES
                q_runs = [(pl.ds(pl.multiple_of(c * sub + n * qn, qn), qn), qn) for c in range(RES)]
                k_runs = [pl.ds(pl.multiple_of(c * sub + kb // RES, 8), kn) for c in range(RES)]
                return q_runs, k_runs, bias_ref[1, di]
            step = dil // RES
            base = (r % RES) * sub + r // RES
            if step == 1:
                q_runs = [(pl.ds(pl.multiple_of(base + n * QB, QB), QB), QB)]
                k_runs = [pl.ds(pl.multiple_of(base + kb, HALF_WIN), KB)]
            else:
                q_runs = [(pl.ds(base + step * QB * n, QB, stride=step), QB)]
                k_runs = [pl.ds(base + step * kb, KB, stride=step)]
            return q_runs, k_runs, bias_ref[0, di]

        def body(gi, carry, block_rows=block_rows):
            loaded = []
            for u in range(ATTN_GROUP):
                q_runs, k_runs, bias = block_rows(gi * ATTN_GROUP + u)
                qr = [rn for rn, _ in q_runs]
                loaded.append((q_runs, load(qa, qr), load(ka, k_runs), load(va, k_runs), bias,
                               load(m_ref, qr, 0), load(m_ref, qr, 1), load(l_ref, qr, 0), load(l_ref, qr, 1),
                               load(acc_ref, qr)))
            results = []
            for q_runs, q, k, v, bias, m0, m1, l0, l1, acc in loaded:
                qs = jnp.concatenate([jnp.where(head0, q, 0.0), jnp.where(head0, 0.0, q)], axis=0).astype(BF16)
                s = lax.dot_general(qs, k.astype(BF16), (((1,), (1,)), ((), ())), preferred_element_type=F32)
                s = s + bias
                m_prev = jnp.concatenate([m0, m1], axis=0)
                l_prev = jnp.concatenate([l0, l1], axis=0)
                m_new = jnp.maximum(m_prev, jnp.max(s, axis=1, keepdims=True))
                alpha = jnp.exp(m_prev - m_new)
                p = jnp.exp(s - jnp.concatenate([m_new, m_new], axis=1))
                pv = jnp.dot(p.astype(BF16), jnp.concatenate([v.astype(BF16), ones], axis=1),
                             preferred_element_type=F32)
                l_new = alpha * l_prev + pv[:, LANES:]
                a_l = jnp.where(head0, alpha[:QB], alpha[QB:])
                pv_l = jnp.where(head0, pv[:QB, :LANES], pv[QB:, :LANES])
                results.append((q_runs, a_l * acc + pv_l, m_new, l_new))
            for q_runs, acc_new, m_new, l_new in results:
                store(acc_ref, q_runs, acc_new)
                store(m_ref, q_runs, m_new[:QB], 0)
                store(m_ref, q_runs, m_new[QB:], 1)
                store(l_ref, q_runs, l_new[:QB], 0)
                store(l_ref, q_runs, l_new[QB:], 1)
            return carry

        lax.fori_loop(0, dil * nblk // ATTN_GROUP, body, 0)

    for r in range(RES):
        def fin(i, carry, r=r):
            rws = pl.ds(pl.multiple_of(r * sub + i * QB, QB), QB)
            den = jnp.where(head0, l_ref[0, rws, :], l_ref[1, rws, :])
            o_ref[0, pl.ds(r + RES * QB * i, QB, stride=RES), :] = acc_ref[rws, :] / den
            return carry

        lax.fori_loop(0, sub // QB, fin, 0)


def _attention(q, k, v):
    B, sub, _ = q.shape
    seq = sub * RES
    npair = ATTN_WIDTH // LANES
    any_spec = pl.BlockSpec(memory_space=pl.ANY)
    return pl.pallas_call(
        functools.partial(_attn_kernel, seq=seq),
        grid=(B, npair),
        in_specs=[any_spec, any_spec, any_spec],
        out_specs=pl.BlockSpec((1, seq, LANES), lambda b, h: (b, 0, h)),
        out_shape=jax.ShapeDtypeStruct((B, seq, ATTN_WIDTH), F32),
        scratch_shapes=[pltpu.VMEM((seq, LANES), F32),
                        pltpu.VMEM((seq, LANES), F32),
                        pltpu.VMEM((seq, LANES), F32),
                        pltpu.VMEM((2, seq, LANES), F32),
                        pltpu.VMEM((2, seq, LANES), F32),
                        pltpu.VMEM((seq, LANES), F32),
                        pltpu.VMEM((2, 3, 2 * QB, KB), F32),
                        pltpu.SemaphoreType.DMA(())],
        compiler_params=_params(("parallel", "parallel")),
        name="attn",
    )(q, k, v)


SSD_TILE = 256
HALO = 8


def _ssd_kernel(*refs, reverse):
    if reverse:
        u_ref, dt_ref, dtb_ref, alog_ref, yf_ref, z_ref, dsk_ref, nw_ref, y_ref, st_ref = refs
    else:
        (xbc_ref, hp_ref, hn_ref, dt_ref, cw_ref, cb_ref, dtb_ref, alog_ref,
         y_ref, u_ref, ext_ref, st_ref) = refs
    lane_off = N_SSM_HEADS if reverse else 0
    nchunk = SSD_TILE // CHUNK

    @pl.when(pl.program_id(1) == 0)
    def _():
        st_ref[...] = jnp.zeros(st_ref.shape, F32)

    if not reverse:
        ext_ref[0:HALO, :] = hp_ref[0]
        ext_ref[HALO:HALO + SSD_TILE, :] = xbc_ref[...]
        ext_ref[HALO + SSD_TILE:HALO + SSD_TILE + HALO, :] = hn_ref[0]

    li = lax.broadcasted_iota(I32, (CHUNK, CHUNK), 0)
    si = lax.broadcasted_iota(I32, (CHUNK, CHUNK), 1)
    tri = (si >= li) if reverse else (si <= li)
    tri_f = tri.astype(F32)
    ej = lax.broadcasted_iota(I32, (LANES, SSM_WIDTH), 0)
    ec = lax.broadcasted_iota(I32, (LANES, SSM_WIDTH), 1)
    expand = (ej == lane_off + ec // SSM_HEAD_DIM).astype(BF16)
    lane = lax.broadcasted_iota(I32, (CHUNK, LANES), 1)
    lo_half = lane < SSM_HEAD_DIM
    a_row = -jnp.exp(alog_ref[...])
    gw = SSM_WIDTH // SSM_GROUPS
    heads_per_group = N_SSM_HEADS // SSM_GROUPS

    order = range(nchunk - 1, -1, -1) if reverse else range(nchunk)
    for c in order:
        crow = pl.ds(c * CHUNK, CHUNK)
        if reverse:
            u = u_ref[crow, :]
        else:
            base = HALO + c * CHUNK
            conv = None
            for kk in range(CONV_K):
                tap = ext_ref[pl.ds(base + kk - CONV_K // 2, CHUNK), :] * cw_ref[kk:kk + 1, :]
                conv = tap if conv is None else conv + tap
            u = conv + cb_ref[...]
            u = u / (1.0 + jnp.exp(-u))
            u_ref[crow, :] = u
        xs = u[:, :SSM_WIDTH]
        bm = u[:, SSM_WIDTH:SSM_WIDTH + SSM_GROUPS * D_STATE]
        cm = u[:, SSM_WIDTH + SSM_GROUPS * D_STATE:]

        dpre = dt_ref[pl.ds(c * CHUNK, CHUNK), :] + dtb_ref[...]
        dtv = jnp.maximum(dpre, 0.0) + jnp.log(1.0 + jnp.exp(-jnp.abs(dpre)))
        a = dtv * a_row
        acs = jnp.dot(tri_f, a, precision=HIGHEST, preferred_element_type=F32)
        acs_t = acs.T
        tot = acs[0:1, :] if reverse else acs[CHUNK - 1:CHUNK, :]
        stacked = jnp.concatenate([dtv, jnp.exp(acs), jnp.exp(tot - acs)], axis=0)
        s_hi = stacked.astype(BF16)
        s_lo = (stacked - s_hi.astype(F32)).astype(BF16)
        ex = (jnp.dot(s_hi, expand, preferred_element_type=F32)
              + jnp.dot(s_lo, expand, preferred_element_type=F32))
        dt_x = ex[:CHUNK]
        eacs_x = ex[CHUNK:2 * CHUNK]
        dend_x = ex[2 * CHUNK:]
        cdec_x = eacs_x[0:1, :] if reverse else eacs_x[CHUNK - 1:CHUNK, :]
        xdt = xs * dt_x
        wst = (dend_x * xdt).astype(BF16)
        st_prev = st_ref[...]
        st_b = st_prev.astype(BF16)

        ygs = []
        st_new = []
        for g in range(SSM_GROUPS):
            ys = []
            bg = bm[:, g * D_STATE:(g + 1) * D_STATE]
            cg = cm[:, g * D_STATE:(g + 1) * D_STATE].astype(BF16)
            cb = lax.dot_general(cg, bg.astype(BF16), (((1,), (1,)), ((), ())), preferred_element_type=F32)
            ms = []
            for hh in range(heads_per_group):
                h = lane_off + g * heads_per_group + hh
                diff = acs[:, h:h + 1] - acs_t[h:h + 1, :]
                lm = jnp.exp(jnp.where(tri, diff, NEG))
                ms.append((cb * lm).astype(BF16))
            for pp in range(heads_per_group // 2):
                hp = g * (heads_per_group // 2) + pp
                lhs = jnp.concatenate([ms[2 * pp], ms[2 * pp + 1]], axis=1)
                xp = xdt[:, hp * LANES:(hp + 1) * LANES]
                rhs = jnp.concatenate([jnp.where(lo_half, xp, 0.0), jnp.where(lo_half, 0.0, xp)],
                                      axis=0).astype(BF16)
                ys.append(jnp.dot(lhs, rhs, preferred_element_type=F32))
            sc = jnp.dot(bg.T.astype(BF16), wst[:, g * gw:(g + 1) * gw], preferred_element_type=F32)
            yoff = jnp.dot(cg, st_b[:, g * gw:(g + 1) * gw], preferred_element_type=F32)
            ygs.append(jnp.concatenate(ys, axis=1) + yoff * eacs_x[:, g * gw:(g + 1) * gw])
            st_new.append(st_prev[:, g * gw:(g + 1) * gw] * cdec_x[:, g * gw:(g + 1) * gw] + sc)
        st_ref[...] = jnp.concatenate(st_new, axis=1)
        y = jnp.concatenate(ygs, axis=1)

        crow = pl.ds(c * CHUNK, CHUNK)
        if reverse:
            ytot = yf_ref[crow, :] + y + dsk_ref[...] * xs
            zz = z_ref[crow, :]
            yz = ytot * (zz / (1.0 + jnp.exp(-zz)))
            yn = yz * lax.rsqrt(jnp.mean(jnp.square(yz), axis=-1, keepdims=True) + RMS_EPS) * nw_ref[...]
            y_ref[crow, :] = yn.astype(y_ref.dtype)
        else:
            y_ref[crow, :] = y


def _ssd(xbc, dt, z, conv_w, conv_b, dt_bias, a_log, d_skip, ssm_norm_w, B, seq):
    T = B * seq
    nt = seq // SSD_TILE
    x4 = xbc.reshape(B, nt, SSD_TILE, CONV_CH)
    zeros = jnp.zeros((B, 1, HALO, CONV_CH), F32)
    hprev = jnp.concatenate([zeros, x4[:, :-1, SSD_TILE - HALO:, :]], axis=1).reshape(B * nt, HALO, CONV_CH)
    hnext = jnp.concatenate([x4[:, 1:, :HALO, :], zeros], axis=1).reshape(B * nt, HALO, CONV_CH)
    cw = jnp.zeros((HALO, CONV_CH), F32).at[:CONV_K].set(conv_w)
    cb = conv_b.reshape(1, CONV_CH)
    pad = LANES - 2 * N_SSM_HEADS
    dtb = jnp.pad(dt_bias.reshape(1, 2 * N_SSM_HEADS), ((0, 0), (0, pad)))
    alog = jnp.pad(a_log.reshape(1, 2 * N_SSM_HEADS), ((0, 0), (0, pad)), constant_values=-1e30)
    dsk = jnp.repeat(d_skip, SSM_HEAD_DIM).reshape(1, SSM_WIDTH)
    nw = ssm_norm_w.reshape(1, SSM_WIDTH)

    const = lambda b, i: (0, 0)
    state = pltpu.VMEM((D_STATE, SSM_WIDTH), F32)

    fmap = lambda b, i: (b * nt + i, 0)
    hmap = lambda b, i: (b * nt + i, 0, 0)
    yf, u = pl.pallas_call(
        functools.partial(_ssd_kernel, reverse=False),
        grid=(B, nt),
        in_specs=[pl.BlockSpec((SSD_TILE, CONV_CH), fmap),
                  pl.BlockSpec((1, HALO, CONV_CH), hmap),
                  pl.BlockSpec((1, HALO, CONV_CH), hmap),
                  pl.BlockSpec((SSD_TILE, LANES), fmap),
                  pl.BlockSpec((HALO, CONV_CH), const),
                  pl.BlockSpec((1, CONV_CH), const),
                  pl.BlockSpec((1, LANES), const),
                  pl.BlockSpec((1, LANES), const)],
        out_specs=[pl.BlockSpec((SSD_TILE, SSM_WIDTH), fmap), pl.BlockSpec((SSD_TILE, CONV_CH), fmap)],
        out_shape=[jax.ShapeDtypeStruct((T, SSM_WIDTH), F32), jax.ShapeDtypeStruct((T, CONV_CH), F32)],
        scratch_shapes=[pltpu.VMEM((SSD_TILE + 2 * HALO, CONV_CH), F32), state],
        compiler_params=_params(("parallel", "arbitrary")),
        name="ssd_fwd",
    )(xbc, hprev, hnext, dt, cw, cb, dtb, alog)

    rmap = lambda b, i: (b * nt + nt - 1 - i, 0)
    return pl.pallas_call(
        functools.partial(_ssd_kernel, reverse=True),
        grid=(B, nt),
        in_specs=[pl.BlockSpec((SSD_TILE, CONV_CH), rmap),
                  pl.BlockSpec((SSD_TILE, LANES), rmap),
                  pl.BlockSpec((1, LANES), const),
                  pl.BlockSpec((1, LANES), const),
                  pl.BlockSpec((SSD_TILE, SSM_WIDTH), rmap),
                  pl.BlockSpec((SSD_TILE, SSM_WIDTH), rmap),
                  pl.BlockSpec((1, SSM_WIDTH), const),
                  pl.BlockSpec((1, SSM_WIDTH), const)],
        out_specs=pl.BlockSpec((SSD_TILE, SSM_WIDTH), rmap),
        out_shape=jax.ShapeDtypeStruct((T, SSM_WIDTH), BF16),
        scratch_shapes=[state],
        compiler_params=_params(("parallel", "arbitrary")),
        name="ssd_bwd",
    )(u, dt, dtb, alog, yf, z, dsk, nw)


def _layer_norm(h, g, b):
    mu = jnp.mean(h, axis=-1, keepdims=True)
    var = jnp.mean(jnp.square(h - mu), axis=-1, keepdims=True)
    return (h - mu) * lax.rsqrt(var + LN_EPS) * g + b


def _outproj_kernel(a_ref, y_ref, x_ref, w_ref, g_ref, b_ref, wr_ref, x1_ref, lg_ref):
    mix = jnp.dot(a_ref[...].astype(BF16), w_ref[:ATTN_WIDTH, :], preferred_element_type=F32)
    mix = mix + jnp.dot(y_ref[...], w_ref[ATTN_WIDTH:, :], preferred_element_type=F32)
    x1 = _layer_norm(ALPHA * x_ref[...] + mix, g_ref[...], b_ref[...])
    x1_ref[...] = x1
    x_hi = x1.astype(BF16)
    x_lo = (x1 - x_hi.astype(F32)).astype(BF16)
    w_hi = wr_ref[...].astype(BF16)
    w_lo = (wr_ref[...] - w_hi.astype(F32)).astype(BF16)
    nt_dims = (((1,), (1,)), ((), ()))
    lg_ref[...] = (lax.dot_general(w_hi, x_hi, nt_dims, preferred_element_type=F32)
                   + lax.dot_general(w_lo, x_hi, nt_dims, preferred_element_type=F32)
                   + lax.dot_general(w_hi, x_lo, nt_dims, preferred_element_type=F32))


def _outproj(attn, y, x2, w_out_b, g, b, wr_t, tm=512):
    T = x2.shape[0]
    row = lambda i: (i, 0)
    const = lambda i: (0, 0)
    return pl.pallas_call(
        _outproj_kernel,
        grid=(T // tm,),
        in_specs=[pl.BlockSpec((tm, ATTN_WIDTH), row),
                  pl.BlockSpec((tm, SSM_WIDTH), row),
                  pl.BlockSpec((tm, D_MODEL), row),
                  pl.BlockSpec((MIX_WIDTH, D_MODEL), const),
                  pl.BlockSpec((1, D_MODEL), const),
                  pl.BlockSpec((1, D_MODEL), const),
                  pl.BlockSpec((N_EXPERTS, D_MODEL), const)],
        out_specs=[pl.BlockSpec((tm, D_MODEL), row),
                   pl.BlockSpec((N_EXPERTS, tm), lambda i: (0, i))],
        out_shape=[jax.ShapeDtypeStruct((T, D_MODEL), F32),
                   jax.ShapeDtypeStruct((N_EXPERTS, T), F32)],
        compiler_params=_params(("parallel",)),
        name="outproj",
    )(attn, y, x2, w_out_b, g.reshape(1, D_MODEL), b.reshape(1, D_MODEL), wr_t)


SEL_BLK = 256


MIN_NORMAL_BITS = 0x00800000


def _select_kernel(lg_ref, aff_ref, pos_ref, off_ref, res_ref, *, cap, T):
    lg = lg_ref[...]
    ex = jnp.exp(lg - jnp.max(lg, axis=0, keepdims=True))
    aff_ref[...] = ex / jnp.sum(ex, axis=0, keepdims=True)

    def as_f32(bits):
        return lax.bitcast_convert_type(bits, F32)

    def kth_largest(ref):
        def search(i, t):
            cand = t | jnp.left_shift(jnp.int32(1), 30 - i)
            cnt = jnp.sum((ref[...] >= as_f32(cand)).astype(I32), axis=1, keepdims=True)
            return jnp.where((cnt >= cap) & (cand >= MIN_NORMAL_BITS), cand, t)

        return lax.fori_loop(0, 31, search, jnp.zeros((N_EXPERTS, 1), I32))

    thr1 = as_f32(kth_largest(aff_ref))
    res_ref[...] = aff_ref[...] - thr1
    thr2_bits = kth_largest(res_ref)
    thr2 = as_f32(thr2_bits)
    nxt2 = as_f32(jnp.where(thr2_bits == 0, MIN_NORMAL_BITS, thr2_bits + 1))
    n_gt = jnp.sum((res_ref[...] >= nxt2).astype(I32), axis=1, keepdims=True)
    need = (cap - n_gt).astype(F32)

    uj = lax.broadcasted_iota(I32, (SEL_BLK, SEL_BLK), 0)
    ut = lax.broadcasted_iota(I32, (SEL_BLK, SEL_BLK), 1)
    upper = (uj <= ut).astype(BF16)
    nblk = T // SEL_BLK
    olane = lax.broadcasted_iota(I32, off_ref.shape, 1)

    def blk(i, carry):
        c_gt, c_eq, offs = carry
        cols = pl.ds(pl.multiple_of(i * SEL_BLK, SEL_BLK), SEL_BLK)
        res = res_ref[:, cols]
        gt = res >= nxt2
        eq = (res >= thr2) & jnp.logical_not(gt)
        gt_f = gt.astype(F32)
        eq_f = eq.astype(F32)
        st = jnp.concatenate([gt_f, eq_f], axis=0).astype(BF16)
        cs = jnp.dot(st, upper, preferred_element_type=F32)
        gt_ex = c_gt + cs[:N_EXPERTS] - gt_f
        eq_ex = c_eq + cs[N_EXPERTS:] - eq_f
        sel = gt | (eq & (eq_ex < need))
        slot = gt_ex + jnp.minimum(eq_ex, need)
        pos_ref[:, cols] = jnp.where(sel, slot, -1.0).astype(I32)
        start = (c_gt + jnp.minimum(c_eq, need)).astype(I32)
        offs = jnp.where(olane == i, start, offs)
        return (c_gt + cs[:N_EXPERTS, SEL_BLK - 1:SEL_BLK], c_eq + cs[N_EXPERTS:, SEL_BLK - 1:SEL_BLK], offs)

    zero = jnp.zeros((N_EXPERTS, 1), F32)
    _, _, offs = lax.fori_loop(0, nblk, blk, (zero, zero, jnp.zeros(off_ref.shape, I32)))
    off_ref[...] = offs


def _select(lg_t, cap):
    T = lg_t.shape[1]
    nblk = T // SEL_BLK
    owidth = -(-nblk // LANES) * LANES
    return pl.pallas_call(
        functools.partial(_select_kernel, cap=cap, T=T),
        out_shape=[jax.ShapeDtypeStruct((N_EXPERTS, T), F32),
                   jax.ShapeDtypeStruct((N_EXPERTS, T), I32),
                   jax.ShapeDtypeStruct((N_EXPERTS, owidth), I32)],
        scratch_shapes=[pltpu.VMEM((N_EXPERTS, T), F32)],
        compiler_params=pltpu.CompilerParams(vmem_limit_bytes=VMEM_LIMIT),
        name="select",
    )(lg_t)


SLOT_TILE = 256
MOE_TOK = 512
IDX_BASE = 256
SLOT_SUBTILES = 4


def _slots_kernel(off_ref, pos_ref, idx_ref, digits_ref, *, n_tok_tiles):
    e = pl.program_id(0)
    g = pl.program_id(1)

    @pl.when(g == 0)
    def _():
        idx_ref[...] = jnp.zeros(idx_ref.shape, F32)

    for s in range(SLOT_SUBTILES):
        i = g * SLOT_SUBTILES + s
        n0 = off_ref[e * (n_tok_tiles + 1) + i]
        n1 = off_ref[e * (n_tok_tiles + 1) + i + 1]
        j0 = n0 // SLOT_TILE
        n_tiles = jnp.where(n1 > n0, (n1 - 1) // SLOT_TILE - j0 + 1, 0)
        pos_row = pos_ref[pl.ds(e, 1), s * MOE_TOK:(s + 1) * MOE_TOK]
        tok = i * MOE_TOK + lax.broadcasted_iota(I32, (MOE_TOK, LANES), 0)
        col = lax.broadcasted_iota(I32, (MOE_TOK, LANES), 1)
        hi = jnp.right_shift(tok, IDX_BASE.bit_length() - 1)
        lo = jnp.bitwise_and(tok, IDX_BASE - 1)
        digits_ref[...] = jnp.where(col == 0, hi, jnp.where(col == 1, lo, 0)).astype(F32).astype(BF16)

        def tile(jj, carry, j0=j0, pos_row=pos_row):
            j = j0 + jj
            sub = lax.broadcasted_iota(I32, (SLOT_TILE, MOE_TOK), 0)
            onehot = jnp.where(pos_row - j * SLOT_TILE == sub, 1.0, 0.0).astype(BF16)
            rows = pl.ds(pl.multiple_of(j * SLOT_TILE, SLOT_TILE), SLOT_TILE)
            idx_ref[0, rows, :] += jnp.dot(onehot, digits_ref[...], preferred_element_type=F32)
            return carry

        lax.fori_loop(0, n_tiles, tile, 0)


def _slot_tokens(pos, off_flat, cap):
    T = pos.shape[1]
    nt = T // MOE_TOK
    assert T // IDX_BASE <= IDX_BASE and nt % SLOT_SUBTILES == 0
    grid_spec = pltpu.PrefetchScalarGridSpec(
        num_scalar_prefetch=1,
        grid=(N_EXPERTS, nt // SLOT_SUBTILES),
        in_specs=[pl.BlockSpec((N_EXPERTS, SLOT_SUBTILES * MOE_TOK), lambda e, g, off: (0, g))],
        out_specs=pl.BlockSpec((1, cap, LANES), lambda e, i, off: (e, 0, 0)),
        scratch_shapes=[pltpu.VMEM((MOE_TOK, LANES), BF16)])
    idx = pl.pallas_call(
        functools.partial(_slots_kernel, n_tok_tiles=nt),
        grid_spec=grid_spec,
        out_shape=jax.ShapeDtypeStruct((N_EXPERTS, cap, LANES), F32),
        compiler_params=_params(("arbitrary", "arbitrary")),
        name="slots",
    )(off_flat, pos)
    return (idx[:, :, 0] * IDX_BASE + idx[:, :, 1]).astype(I32).reshape(-1)


MOE_SLOT = 512
FF_BLK = 256
N_FF = D_FF_EXPERT // FF_BLK
ROWS_PER_FF = 48
GATHER_ROWS = ROWS_PER_FF * N_FF
assert D_FF_EXPERT % FF_BLK == 0 and GATHER_ROWS >= MOE_SLOT and GATHER_ROWS % 8 == 0


def _moe_kernel(tok_ref, x_hbm, wg_ref, wu_ref, wd_ref, out_ref, xg_ref, xb_ref, acc_ref, sem, *, n_slot_tiles):
    step = pl.program_id(0) * n_slot_tiles + pl.program_id(1)
    total = N_EXPERTS * n_slot_tiles
    cur = step % 2

    def row_copy(tile_idx, r, buf):
        slot = jnp.minimum(r, MOE_SLOT - 1)
        tok = tok_ref[tile_idx * MOE_SLOT + slot]
        return pltpu.make_async_copy(x_hbm.at[pl.ds(tok, 1)], xg_ref.at[buf, pl.ds(r, 1)], sem.at[buf])

    def wait_tile(buf):
        pltpu.make_async_copy(x_hbm.at[pl.ds(0, GATHER_ROWS)], xg_ref.at[buf], sem.at[buf]).wait()

    @pl.when(step == 0)
    def _():
        def first(r, carry):
            row_copy(0, r, 0).start()
            return carry

        lax.fori_loop(0, GATHER_ROWS, first, 0)

    wait_tile(cur)
    nxt = (step + 1) % total
    xb_ref[...] = xg_ref[cur, :MOE_SLOT, :].astype(BF16)
    acc_ref[...] = jnp.zeros(acc_ref.shape, F32)

    def ff(c, carry):
        for r in range(ROWS_PER_FF):
            row_copy(nxt, c * ROWS_PER_FF + r, 1 - cur).start()
        cols = pl.ds(pl.multiple_of(c * FF_BLK, FF_BLK), FF_BLK)
        xb = xb_ref[...]
        g = jnp.dot(xb, wg_ref[0, :, cols], preferred_element_type=F32)
        u = jnp.dot(xb, wu_ref[0, :, cols], preferred_element_type=F32)
        h = ((g / (1.0 + jnp.exp(-g))) * u).astype(BF16)
        acc_ref[...] += jnp.dot(h, wd_ref[0, cols, :], preferred_element_type=F32)
        return carry

    lax.fori_loop(0, N_FF, ff, 0)
    out_ref[0] = acc_ref[...].astype(out_ref.dtype)

    @pl.when(step == total - 1)
    def _():
        wait_tile(1 - cur)


def _moe(x1, tok_flat, wg, wu, wd, cap):
    assert cap % MOE_SLOT == 0
    nj = cap // MOE_SLOT
    grid_spec = pltpu.PrefetchScalarGridSpec(
        num_scalar_prefetch=1,
        grid=(N_EXPERTS, nj),
        in_specs=[pl.BlockSpec(memory_space=pl.ANY),
                  pl.BlockSpec((1, D_MODEL, D_FF_EXPERT), lambda e, j, tok: (e, 0, 0)),
                  pl.BlockSpec((1, D_MODEL, D_FF_EXPERT), lambda e, j, tok: (e, 0, 0)),
                  pl.BlockSpec((1, D_FF_EXPERT, D_MODEL), lambda e, j, tok: (e, 0, 0))],
        out_specs=pl.BlockSpec((1, MOE_SLOT, D_MODEL), lambda e, j, tok: (e, j, 0)),
        scratch_shapes=[pltpu.VMEM((2, GATHER_ROWS, D_MODEL), F32),
                        pltpu.VMEM((MOE_SLOT, D_MODEL), BF16),
                        pltpu.VMEM((MOE_SLOT, D_MODEL), F32),
                        pltpu.SemaphoreType.DMA((2,))])
    return pl.pallas_call(
        functools.partial(_moe_kernel, n_slot_tiles=nj),
        grid_spec=grid_spec,
        out_shape=jax.ShapeDtypeStruct((N_EXPERTS, cap, D_MODEL), BF16),
        compiler_params=_params(("arbitrary", "arbitrary")),
        name="moe",
    )(tok_flat, x1, wg, wu, wd)


CMB_TOK = 256
ROW_CHUNK = 16
STAGE_ROWS = N_EXPERTS * (CMB_TOK + 2 * ROW_CHUNK)
NO_ROW = -(2 ** 30)


def _combine_kernel(off_ref, x1_ref, gid_ref, gate_ref, g_ref, b_ref, eo_ref, o_ref, stage_ref, w_ref, sem,
                    *, cap, n_tok_tiles):
    i = pl.program_id(0)
    cur = i % 2

    def chunk_copy(src_row, dst_row, buf):
        return pltpu.make_async_copy(eo_ref.at[pl.ds(src_row, ROW_CHUNK)],
                                     stage_ref.at[buf, pl.ds(dst_row, ROW_CHUNK)], sem.at[buf])

    def segments(tile):
        base = jnp.int32(0)
        segs = []
        for e in range(N_EXPERTS):
            n0 = off_ref[e * (n_tok_tiles + 1) + tile]
            n1 = off_ref[e * (n_tok_tiles + 1) + tile + 1]
            nch = (n1 + ROW_CHUNK - 1) // ROW_CHUNK - n0 // ROW_CHUNK
            segs.append((base, nch, e * cap + (n0 // ROW_CHUNK) * ROW_CHUNK))
            base = base + nch * ROW_CHUNK
        return segs, base

    def issue(tile, buf):
        segs, _ = segments(tile)
        for seg_base, nch, seg_row in segs:
            def one(c, carry, seg_base=seg_base, seg_row=seg_row):
                src = pl.multiple_of(seg_row + c * ROW_CHUNK, ROW_CHUNK)
                dst = pl.multiple_of(seg_base + c * ROW_CHUNK, ROW_CHUNK)
                chunk_copy(src, dst, buf).start()
                return carry

            lax.fori_loop(0, nch, one, 0)

    @pl.when(i == 0)
    def _():
        stage_ref[...] = jnp.zeros(stage_ref.shape, stage_ref.dtype)
        issue(0, 0)

    @pl.when(i + 1 < n_tok_tiles)
    def _():
        issue(i + 1, 1 - cur)

    segs, total = segments(i)

    def wait(c, carry):
        chunk_copy(0, 0, cur).wait()
        return carry

    lax.fori_loop(0, total // ROW_CHUNK, wait, 0)

    jsub = lax.broadcasted_iota(I32, (SLOT_TILE, CMB_TOK), 0)
    o_ref[...] = ALPHA * x1_ref[...]

    def kchunk(kc, carry):
        j0 = kc * SLOT_TILE
        w_ref[...] = jnp.zeros(w_ref.shape, F32)
        for e, (seg_base, nch, seg_row) in enumerate(segs):
            seg_end = seg_base + nch * ROW_CHUNK

            @pl.when((seg_base < j0 + SLOT_TILE) & (seg_end > j0))
            def _(e=e, seg_base=seg_base, seg_row=seg_row):
                row_id = jsub + (j0 + seg_row - seg_base)
                w_ref[...] = jnp.where(gid_ref[e:e + 1, :] == row_id, gate_ref[e:e + 1, :], w_ref[...])

        rows = stage_ref[cur, pl.ds(pl.multiple_of(j0, SLOT_TILE), SLOT_TILE), :]
        o_ref[...] += lax.dot_general(w_ref[...].astype(BF16), rows, (((0,), (0,)), ((), ())),
                                      preferred_element_type=F32)
        return carry

    lax.fori_loop(0, (total + SLOT_TILE - 1) // SLOT_TILE, kchunk, 0)
    o_ref[...] = _layer_norm(o_ref[...], g_ref[...], b_ref[...])


def _combine(x1, gid_tok, gate_tok, off_flat, eo_flat, g, b, cap):
    T = x1.shape[0]
    nt = T // CMB_TOK
    stage_rows = -(-STAGE_ROWS // SLOT_TILE) * SLOT_TILE
    row = lambda i, off: (i, 0)
    const = lambda i, off: (0, 0)
    grid_spec = pltpu.PrefetchScalarGridSpec(
        num_scalar_prefetch=1,
        grid=(nt,),
        in_specs=[pl.BlockSpec((CMB_TOK, D_MODEL), row),
                  pl.BlockSpec((N_EXPERTS, CMB_TOK), lambda i, off: (0, i)),
                  pl.BlockSpec((N_EXPERTS, CMB_TOK), lambda i, off: (0, i)),
                  pl.BlockSpec((1, D_MODEL), const),
                  pl.BlockSpec((1, D_MODEL), const),
                  pl.BlockSpec(memory_space=pl.ANY)],
        out_specs=pl.BlockSpec((CMB_TOK, D_MODEL), row),
        scratch_shapes=[pltpu.VMEM((2, stage_rows, D_MODEL), BF16),
                        pltpu.VMEM((SLOT_TILE, CMB_TOK), F32),
                        pltpu.SemaphoreType.DMA((2,))])
    return pl.pallas_call(
        functools.partial(_combine_kernel, cap=cap, n_tok_tiles=nt),
        grid_spec=grid_spec,
        out_shape=jax.ShapeDtypeStruct((T, D_MODEL), F32),
        compiler_params=_params(("arbitrary",)),
        name="combine",
    )(off_flat, x1, gid_tok, gate_tok, g.reshape(1, D_MODEL), b.reshape(1, D_MODEL), eo_flat)


def _tile_offsets(off256, cap, T, tok):
    step = tok // SEL_BLK
    o = off256[:, :T // SEL_BLK:step]
    o = jnp.concatenate([o, jnp.full((N_EXPERTS, 1), cap, I32)], axis=1)
    return o.reshape(-1)


def _trunk(x, w_in_p, conv_w, conv_b, dt_bias, a_log, d_skip, ssm_norm_w, w_out_b, ln1_g, ln1_b,
           wr_t, wg, wu, wd, ln2_g, ln2_b):
    B, seq, _ = x.shape
    T = B * seq
    cap = CAPACITY_FACTOR * T // N_EXPERTS
    x2 = x.reshape(T, D_MODEL)
    q, k, v, z, xbc, dt = _inproj(x2, w_in_p, seq)
    shp = (B, seq // RES, RES * ATTN_WIDTH)
    attn = _attention(q.reshape(shp), k.reshape(shp), v.reshape(shp)).reshape(T, ATTN_WIDTH)
    y = _ssd(xbc, dt, z, conv_w, conv_b, dt_bias, a_log, d_skip, ssm_norm_w, B, seq)
    x1, lg_t = _outproj(attn, y, x2, w_out_b, ln1_g, ln1_b, wr_t)
    aff_t, pos, off256 = _select(lg_t, cap)
    tok_flat = _slot_tokens(pos, _tile_offsets(off256, cap, T, MOE_TOK), cap)
    eo = _moe(x1, tok_flat, wg, wu, wd, cap)
    gid = jnp.where(pos >= 0, pos + jnp.arange(N_EXPERTS, dtype=I32)[:, None] * cap, NO_ROW)
    out = _combine(x1, gid, aff_t, _tile_offsets(off256, cap, T, CMB_TOK),
                   eo.reshape(N_EXPERTS * cap, D_MODEL), ln2_g, ln2_b, cap)
    return out.reshape(B, seq, D_MODEL)


def kernel(x_prompt, x_sample, w_in, conv_w, conv_b, dt_bias, a_log, d_skip, ssm_norm_w, w_out, ln1_g, ln1_b,
           w_router, w_gate, w_up, w_down, ln2_g, ln2_b):
    assert DEPTH == 1
    l = 0
    w_in_p = jnp.pad(w_in[l], ((0, 0), (0, IN_PAD - w_in.shape[-1]))).astype(BF16)
    args = (w_in_p, conv_w[l], conv_b[l], dt_bias[l], a_log[l], d_skip[l], ssm_norm_w[l],
            w_out[l].astype(BF16), ln1_g[l], ln1_b[l], w_router[l].T,
            w_gate[l].astype(BF16), w_up[l].astype(BF16), w_down[l].astype(BF16), ln2_g[l], ln2_b[l])
    return (_trunk(x_prompt, *args), _trunk(x_sample, *args))
```

```python
import functools

import jax
import jax.numpy as jnp
from jax import lax
from jax.experimental import pallas as pl
from jax.experimental.pallas import tpu as pltpu

F32 = jnp.float32
BF16 = jnp.bfloat16
I32 = jnp.int32
HIGHEST = lax.Precision.HIGHEST

D_MODEL = 1024
DEPTH = 1
HEAD_DIM = 64
N_ATTN_HEADS = 8
ATTN_WIDTH = N_ATTN_HEADS * HEAD_DIM
ROT_DIM = HEAD_DIM // 4
ROPE_THETA = 500000.0
DILATED_PATTERNS = ((128, 1), (512, 4), (2048, 16))
N_SSM_HEADS = 8
SSM_HEAD_DIM = 64
SSM_WIDTH = N_SSM_HEADS * SSM_HEAD_DIM
SSM_GROUPS = 2
D_STATE = 128
CONV_K = 5
CHUNK = 128
CONV_CH = SSM_WIDTH + 2 * SSM_GROUPS * D_STATE
MIX_WIDTH = ATTN_WIDTH + SSM_WIDTH
N_EXPERTS = 16
CAPACITY_FACTOR = 2
D_FF_EXPERT = 2816
ALPHA = (2.0 * DEPTH) ** 0.25
LN_EPS = 1e-5
RMS_EPS = 1e-5

LANES = 128
VMEM_LIMIT = 56 * 1024 * 1024
NEG = -1e30
HALF_WIN = 64
RES = 4

IN_PAD = 3 * ATTN_WIDTH + SSM_WIDTH + CONV_CH + LANES


def _params(sem):
    return pltpu.CompilerParams(dimension_semantics=sem, vmem_limit_bytes=VMEM_LIMIT)


def _inproj_kernel(x_ref, w_ref, c_ref, sa_ref, sb_ref, q_ref, k_ref, v_ref, z_ref, xbc_ref, dt_ref, tmp_ref):
    x = x_ref[...].astype(BF16)
    tm = x_ref.shape[0]

    def put_grouped(out_ref, val):
        for g in range(ATTN_WIDTH // LANES):
            tmp_ref[g] = val[:, g * LANES:(g + 1) * LANES]
        for r in range(RES):
            for g in range(ATTN_WIDTH // LANES):
                lo = r * ATTN_WIDTH + g * LANES
                out_ref[:, lo:lo + LANES] = tmp_ref[g, pl.ds(r, tm // RES, stride=RES), :]

    def seg(lo, hi):
        return jnp.dot(x, w_ref[:, lo:hi], preferred_element_type=F32)

    c = c_ref[...]
    sa = sa_ref[...]
    sb = sb_ref[...]

    def rotary(t, scale):
        outs = []
        for g in range(ATTN_WIDTH // LANES):
            tg = t[:, g * LANES:(g + 1) * LANES]
            up = pltpu.roll(tg, LANES - ROT_DIM // 2, 1)
            dn = pltpu.roll(tg, ROT_DIM // 2, 1)
            outs.append((tg * c + up * sa + dn * sb) * scale)
        return jnp.concatenate(outs, axis=1)

    a = ATTN_WIDTH
    put_grouped(q_ref, rotary(seg(0, a), HEAD_DIM ** -0.5))
    put_grouped(k_ref, rotary(seg(a, 2 * a), 1.0))
    put_grouped(v_ref, seg(2 * a, 3 * a))
    z_ref[...] = seg(3 * a, 3 * a + SSM_WIDTH)
    o = 3 * a + SSM_WIDTH
    xbc_ref[...] = seg(o, o + CONV_CH)
    dt_ref[...] = seg(o + CONV_CH, o + CONV_CH + LANES)


def _rotary_tables(seq):
    half = ROT_DIM // 2
    inv = ROPE_THETA ** (-jnp.arange(half, dtype=F32) * 2.0 / ROT_DIM)
    ang = jnp.arange(seq).astype(F32)[:, None] * inv[None, :]
    cos = jnp.cos(ang)
    sin = jnp.sin(ang)
    m = jnp.arange(LANES) % HEAD_DIM
    c = jnp.where(m[None, :] < ROT_DIM, cos[:, m % half], 1.0)
    sa = jnp.where(m[None, :] < half, -sin[:, m % half], 0.0)
    sb = jnp.where((m[None, :] >= half) & (m[None, :] < ROT_DIM), sin[:, m % half], 0.0)
    return c.astype(F32), sa.astype(F32), sb.astype(F32)


def _inproj(x2, w_pad, seq, tm=512):
    T = x2.shape[0]
    c, sa, sb = _rotary_tables(seq)
    nseq = seq // tm
    row = lambda i: (i, 0)
    tab = lambda i: (i % nseq, 0)
    outs = [jax.ShapeDtypeStruct((T // RES, RES * ATTN_WIDTH), F32)] * 3 + [
        jax.ShapeDtypeStruct((T, SSM_WIDTH), F32),
        jax.ShapeDtypeStruct((T, CONV_CH), F32),
        jax.ShapeDtypeStruct((T, LANES), F32)]
    return pl.pallas_call(
        _inproj_kernel,
        grid=(T // tm,),
        in_specs=[pl.BlockSpec((tm, D_MODEL), row),
                  pl.BlockSpec((D_MODEL, IN_PAD), lambda i: (0, 0)),
                  pl.BlockSpec((tm, LANES), tab),
                  pl.BlockSpec((tm, LANES), tab),
                  pl.BlockSpec((tm, LANES), tab)],
        out_specs=[pl.BlockSpec((tm // RES, RES * ATTN_WIDTH), row)] * 3 + [
            pl.BlockSpec((tm, SSM_WIDTH), row),
            pl.BlockSpec((tm, CONV_CH), row),
            pl.BlockSpec((tm, LANES), row)],
        out_shape=outs,
        scratch_shapes=[pltpu.VMEM((ATTN_WIDTH // LANES, tm, LANES), F32)],
        compiler_params=_params(("parallel",)),
        name="inproj",
    )(x2, w_pad, c, sa, sb)


QB = 128
KB = 256
ATTN_GROUP = 8


def _attn_kernel(q_hbm, k_hbm, v_hbm, o_ref, qa, ka, va, m_ref, l_ref, acc_ref, bias_ref, sem, *, seq):
    b = pl.program_id(0)
    h = pl.program_id(1)
    sub = seq // RES
    npair = ATTN_WIDTH // LANES

    copies = []
    for src, dst in ((q_hbm, qa), (k_hbm, ka), (v_hbm, va)):
        for r in range(RES):
            lanes = pl.ds(pl.multiple_of((r * npair + h) * LANES, LANES), LANES)
            copies.append(pltpu.make_async_copy(src.at[b, :, lanes], dst.at[pl.ds(r * sub, sub), :], sem))
    for c in copies:
        c.start()

    row = lax.broadcasted_iota(I32, (2 * QB, KB), 0)
    col = lax.broadcasted_iota(I32, (2 * QB, KB), 1)
    iq = jnp.where(row >= QB, row - QB, row)
    iq_runs = RES * (iq % (QB // RES)) + iq // (QB // RES)
    ik_runs = RES * (col % (KB // RES)) + col // (KB // RES)
    for di in range(3):
        bias_ref[0, di] = jnp.where(jnp.abs(di * HALF_WIN + iq - col) <= HALF_WIN, 0.0, NEG).astype(F32)
        bias_ref[1, di] = jnp.where(jnp.abs(di * HALF_WIN + iq_runs - ik_runs) <= HALF_WIN, 0.0, NEG).astype(F32)
    m_ref[...] = jnp.full(m_ref.shape, NEG, F32)
    l_ref[...] = jnp.zeros(l_ref.shape, F32)
    acc_ref[...] = jnp.zeros(acc_ref.shape, F32)
    for c in copies:
        c.wait()

    lane = lax.broadcasted_iota(I32, (QB, LANES), 1)
    head0 = lane < HEAD_DIM
    ones = jnp.ones((KB, LANES), BF16)

    def load(ref, runs, *lead):
        parts = [ref[lead + (rn, slice(None))] for rn in runs]
        return parts[0] if len(parts) == 1 else jnp.concatenate(parts, axis=0)

    def store(ref, runs, val, *lead):
        at = 0
        for rn, size in runs:
            ref[lead + (rn, slice(None))] = val[at:at + size]
            at += size

    for window, dil in DILATED_PATTERNS:
        assert window // (2 * dil) == HALF_WIN and (dil == 1 or dil % RES == 0)
        sub_len = seq // dil
        nblk = sub_len // QB
        assert sub_len >= KB and sub_len % QB == 0 and (dil * nblk) % ATTN_GROUP == 0

        def block_rows(idx, dil=dil, sub_len=sub_len, nblk=nblk):
            r = idx // nblk
            n = idx - r * nblk
            kb = jnp.clip(n * QB - HALF_WIN, 0, sub_len - KB)
            di = (n * QB - kb) // HALF_WIN
            if dil == 1:
                qn, kn = QB // RES, KB // RES
                q_runs = [(pl.ds(pl.multiple_of(c * sub + n * qn, qn), qn), qn) for c in range(RES)]
                k_runs = [pl.ds(pl.multiple_of(c * sub + kb // RES, 8), kn) for c in range(RES)]
                return q_runs, k_runs, bias_ref[1, di]
            step = dil // RES
            base = (r % RES) * sub + r // RES
            if step == 1:
                q_runs = [(pl.ds(pl.multiple_of(base + n * QB, QB), QB), QB)]
                k_runs = [pl.ds(pl.multiple_of(base + kb, HALF_WIN), KB)]
            else:
                q_runs = [(pl.ds(base + step * QB * n, QB, stride=step), QB)]
                k_runs = [pl.ds(base + step * kb, KB, stride=step)]
            return q_runs, k_runs, bias_ref[0, di]

        def body(gi, carry, block_rows=block_rows):
            loaded = []
            for u in range(ATTN_GROUP):
                q_runs, k_runs, bias = block_rows(gi * ATTN_GROUP + u)
                qr = [rn for rn, _ in q_runs]
                loaded.append((q_runs, load(qa, qr), load(ka, k_runs), load(va, k_runs), bias,
                               load(m_ref, qr, 0), load(m_ref, qr, 1), load(l_ref, qr, 0), load(l_ref, qr, 1),
                               load(acc_ref, qr)))
            results = []
            for q_runs, q, k, v, bias, m0, m1, l0, l1, acc in loaded:
                qs = jnp.concatenate([jnp.where(head0, q, 0.0), jnp.where(head0, 0.0, q)], axis=0).astype(BF16)
                s = lax.dot_general(qs, k.astype(BF16), (((1,), (1,)), ((), ())), preferred_element_type=F32)
                s = s + bias
                m_prev = jnp.concatenate([m0, m1], axis=0)
                l_prev = jnp.concatenate([l0, l1], axis=0)
                m_new = jnp.maximum(m_prev, jnp.max(s, axis=1, keepdims=True))
                alpha = jnp.exp(m_prev - m_new)
                p = jnp.exp(s - jnp.concatenate([m_new, m_new], axis=1))
                pv = jnp.dot(p.astype(BF16), jnp.concatenate([v.astype(BF16), ones], axis=1),
                             preferred_element_type=F32)
                l_new = alpha * l_prev + pv[:, LANES:]
                a_l = jnp.where(head0, alpha[:QB], alpha[QB:])
                pv_l = jnp.where(head0, pv[:QB, :LANES], pv[QB:, :LANES])
                results.append((q_runs, a_l * acc + pv_l, m_new, l_new))
            for q_runs, acc_new, m_new, l_new in results:
                store(acc_ref, q_runs, acc_new)
                store(m_ref, q_runs, m_new[:QB], 0)
                store(m_ref, q_runs, m_new[QB:], 1)
                store(l_ref, q_runs, l_new[:QB], 0)
                store(l_ref, q_runs, l_new[QB:], 1)
            return carry

        lax.fori_loop(0, dil * nblk // ATTN_GROUP, body, 0)

    for r in range(RES):
        def fin(i, carry, r=r):
            rws = pl.ds(pl.multiple_of(r * sub + i * QB, QB), QB)
            den = jnp.where(head0, l_ref[0, rws, :], l_ref[1, rws, :])
            o_ref[0, pl.ds(r + RES * QB * i, QB, stride=RES), :] = acc_ref[rws, :] / den
            return carry

        lax.fori_loop(0, sub // QB, fin, 0)


def _attention(q, k, v):
    B, sub, _ = q.shape
    seq = sub * RES
    npair = ATTN_WIDTH // LANES
    any_spec = pl.BlockSpec(memory_space=pl.ANY)
    return pl.pallas_call(
        functools.partial(_attn_kernel, seq=seq),
        grid=(B, npair),
        in_specs=[any_spec, any_spec, any_spec],
        out_specs=pl.BlockSpec((1, seq, LANES), lambda b, h: (b, 0, h)),
        out_shape=jax.ShapeDtypeStruct((B, seq, ATTN_WIDTH), F32),
        scratch_shapes=[pltpu.VMEM((seq, LANES), F32),
                        pltpu.VMEM((seq, LANES), F32),
                        pltpu.VMEM((seq, LANES), F32),
                        pltpu.VMEM((2, seq, LANES), F32),
                        pltpu.VMEM((2, seq, LANES), F32),
                        pltpu.VMEM((seq, LANES), F32),
                        pltpu.VMEM((2, 3, 2 * QB, KB), F32),
                        pltpu.SemaphoreType.DMA(())],
        compiler_params=_params(("parallel", "parallel")),
        name="attn",
    )(q, k, v)


SSD_TILE = 512
HALO = 8


def _ssd_kernel(*refs, reverse):
    if reverse:
        u_ref, dt_ref, dtb_ref, alog_ref, yf_ref, z_ref, dsk_ref, nw_ref, y_ref, st_ref = refs
    else:
        (xbc_ref, hp_ref, hn_ref, dt_ref, cw_ref, cb_ref, dtb_ref, alog_ref,
         y_ref, u_ref, ext_ref, st_ref) = refs
    lane_off = N_SSM_HEADS if reverse else 0
    nchunk = SSD_TILE // CHUNK

    @pl.when(pl.program_id(1) == 0)
    def _():
        st_ref[...] = jnp.zeros(st_ref.shape, F32)

    if not reverse:
        ext_ref[0:HALO, :] = hp_ref[0]
        ext_ref[HALO:HALO + SSD_TILE, :] = xbc_ref[...]
        ext_ref[HALO + SSD_TILE:HALO + SSD_TILE + HALO, :] = hn_ref[0]

    li = lax.broadcasted_iota(I32, (CHUNK, CHUNK), 0)
    si = lax.broadcasted_iota(I32, (CHUNK, CHUNK), 1)
    tri = (si >= li) if reverse else (si <= li)
    tri_f = tri.astype(F32)
    ej = lax.broadcasted_iota(I32, (LANES, SSM_WIDTH), 0)
    ec = lax.broadcasted_iota(I32, (LANES, SSM_WIDTH), 1)
    expand = (ej == lane_off + ec // SSM_HEAD_DIM).astype(BF16)
    lane = lax.broadcasted_iota(I32, (CHUNK, LANES), 1)
    lo_half = lane < SSM_HEAD_DIM
    a_row = -jnp.exp(alog_ref[...])
    gw = SSM_WIDTH // SSM_GROUPS
    heads_per_group = N_SSM_HEADS // SSM_GROUPS

    order = range(nchunk - 1, -1, -1) if reverse else range(nchunk)
    for c in order:
        crow = pl.ds(c * CHUNK, CHUNK)
        if reverse:
            u = u_ref[crow, :]
        else:
            base = HALO + c * CHUNK
            conv = None
            for kk in range(CONV_K):
                tap = ext_ref[pl.ds(base + kk - CONV_K // 2, CHUNK), :] * cw_ref[kk:kk + 1, :]
                conv = tap if conv is None else conv + tap
            u = conv + cb_ref[...]
            u = u / (1.0 + jnp.exp(-u))
            u_ref[crow, :] = u
        xs = u[:, :SSM_WIDTH]
        bm = u[:, SSM_WIDTH:SSM_WIDTH + SSM_GROUPS * D_STATE]
        cm = u[:, SSM_WIDTH + SSM_GROUPS * D_STATE:]

        dpre = dt_ref[pl.ds(c * CHUNK, CHUNK), :] + dtb_ref[...]
        dtv = jnp.maximum(dpre, 0.0) + jnp.log(1.0 + jnp.exp(-jnp.abs(dpre)))
        a = dtv * a_row
        acs = jnp.dot(tri_f, a, precision=HIGHEST, preferred_element_type=F32)
        acs_t = acs.T
        tot = acs[0:1, :] if reverse else acs[CHUNK - 1:CHUNK, :]
        stacked = jnp.concatenate([dtv, jnp.exp(acs), jnp.exp(tot - acs)], axis=0)
        s_hi = stacked.astype(BF16)
        s_lo = (stacked - s_hi.astype(F32)).astype(BF16)
        ex = (jnp.dot(s_hi, expand, preferred_element_type=F32)
              + jnp.dot(s_lo, expand, preferred_element_type=F32))
        dt_x = ex[:CHUNK]
        eacs_x = ex[CHUNK:2 * CHUNK]
        dend_x = ex[2 * CHUNK:]
        cdec_x = eacs_x[0:1, :] if reverse else eacs_x[CHUNK - 1:CHUNK, :]
        xdt = xs * dt_x
        wst = (dend_x * xdt).astype(BF16)
        st_prev = st_ref[...]
        st_b = st_prev.astype(BF16)

        ygs = []
        st_new = []
        for g in range(SSM_GROUPS):
            ys = []
            bg = bm[:, g * D_STATE:(g + 1) * D_STATE]
            cg = cm[:, g * D_STATE:(g + 1) * D_STATE].astype(BF16)
            cb = lax.dot_general(cg, bg.astype(BF16), (((1,), (1,)), ((), ())), preferred_element_type=F32)
            ms = []
            for hh in range(heads_per_group):
                h = lane_off + g * heads_per_group + hh
                diff = acs[:, h:h + 1] - acs_t[h:h + 1, :]
                lm = jnp.exp(jnp.where(tri, diff, NEG))
                ms.append((cb * lm).astype(BF16))
            for pp in range(heads_per_group // 2):
                hp = g * (heads_per_group // 2) + pp
                lhs = jnp.concatenate([ms[2 * pp], ms[2 * pp + 1]], axis=1)
                xp = xdt[:, hp * LANES:(hp + 1) * LANES]
                rhs = jnp.concatenate([jnp.where(lo_half, xp, 0.0), jnp.where(lo_half, 0.0, xp)],
                                      axis=0).astype(BF16)
                ys.append(jnp.dot(lhs, rhs, preferred_element_type=F32))
            sc = jnp.dot(bg.T.astype(BF16), wst[:, g * gw:(g + 1) * gw], preferred_element_type=F32)
            yoff = jnp.dot(cg, st_b[:, g * gw:(g + 1) * gw], preferred_element_type=F32)
            ygs.append(jnp.concatenate(ys, axis=1) + yoff * eacs_x[:, g * gw:(g + 1) * gw])
            st_new.append(st_prev[:, g * gw:(g + 1) * gw] * cdec_x[:, g * gw:(g + 1) * gw] + sc)
        st_ref[...] = jnp.concatenate(st_new, axis=1)
        y = jnp.concatenate(ygs, axis=1)

        crow = pl.ds(c * CHUNK, CHUNK)
        if reverse:
            ytot = yf_ref[crow, :] + y + dsk_ref[...] * xs
            zz = z_ref[crow, :]
            yz = ytot * (zz / (1.0 + jnp.exp(-zz)))
            yn = yz * lax.rsqrt(jnp.mean(jnp.square(yz), axis=-1, keepdims=True) + RMS_EPS) * nw_ref[...]
            y_ref[crow, :] = yn.astype(y_ref.dtype)
        else:
            y_ref[crow, :] = y


def _ssd(xbc, dt, z, conv_w, conv_b, dt_bias, a_log, d_skip, ssm_norm_w, B, seq):
    T = B * seq
    nt = seq // SSD_TILE
    x4 = xbc.reshape(B, nt, SSD_TILE, CONV_CH)
    zeros = jnp.zeros((B, 1, HALO, CONV_CH), F32)
    hprev = jnp.concatenate([zeros, x4[:, :-1, SSD_TILE - HALO:, :]], axis=1).reshape(B * nt, HALO, CONV_CH)
    hnext = jnp.concatenate([x4[:, 1:, :HALO, :], zeros], axis=1).reshape(B * nt, HALO, CONV_CH)
    cw = jnp.zeros((HALO, CONV_CH), F32).at[:CONV_K].set(conv_w)
    cb = conv_b.reshape(1, CONV_CH)
    pad = LANES - 2 * N_SSM_HEADS
    dtb = jnp.pad(dt_bias.reshape(1, 2 * N_SSM_HEADS), ((0, 0), (0, pad)))
    alog = jnp.pad(a_log.reshape(1, 2 * N_SSM_HEADS), ((0, 0), (0, pad)), constant_values=-1e30)
    dsk = jnp.repeat(d_skip, SSM_HEAD_DIM).reshape(1, SSM_WIDTH)
    nw = ssm_norm_w.reshape(1, SSM_WIDTH)

    const = lambda b, i: (0, 0)
    state = pltpu.VMEM((D_STATE, SSM_WIDTH), F32)

    fmap = lambda b, i: (b * nt + i, 0)
    hmap = lambda b, i: (b * nt + i, 0, 0)
    yf, u = pl.pallas_call(
        functools.partial(_ssd_kernel, reverse=False),
        grid=(B, nt),
        in_specs=[pl.BlockSpec((SSD_TILE, CONV_CH), fmap),
                  pl.BlockSpec((1, HALO, CONV_CH), hmap),
                  pl.BlockSpec((1, HALO, CONV_CH), hmap),
                  pl.BlockSpec((SSD_TILE, LANES), fmap),
                  pl.BlockSpec((HALO, CONV_CH), const),
                  pl.BlockSpec((1, CONV_CH), const),
                  pl.BlockSpec((1, LANES), const),
                  pl.BlockSpec((1, LANES), const)],
        out_specs=[pl.BlockSpec((SSD_TILE, SSM_WIDTH), fmap), pl.BlockSpec((SSD_TILE, CONV_CH), fmap)],
        out_shape=[jax.ShapeDtypeStruct((T, SSM_WIDTH), F32), jax.ShapeDtypeStruct((T, CONV_CH), F32)],
        scratch_shapes=[pltpu.VMEM((SSD_TILE + 2 * HALO, CONV_CH), F32), state],
        compiler_params=_params(("parallel", "arbitrary")),
        name="ssd_fwd",
    )(xbc, hprev, hnext, dt, cw, cb, dtb, alog)

    rmap = lambda b, i: (b * nt + nt - 1 - i, 0)
    return pl.pallas_call(
        functools.partial(_ssd_kernel, reverse=True),
        grid=(B, nt),
        in_specs=[pl.BlockSpec((SSD_TILE, CONV_CH), rmap),
                  pl.BlockSpec((SSD_TILE, LANES), rmap),
                  pl.BlockSpec((1, LANES), const),
                  pl.BlockSpec((1, LANES), const),
                  pl.BlockSpec((SSD_TILE, SSM_WIDTH), rmap),
                  pl.BlockSpec((SSD_TILE, SSM_WIDTH), rmap),
                  pl.BlockSpec((1, SSM_WIDTH), const),
                  pl.BlockSpec((1, SSM_WIDTH), const)],
        out_specs=pl.BlockSpec((SSD_TILE, SSM_WIDTH), rmap),
        out_shape=jax.ShapeDtypeStruct((T, SSM_WIDTH), BF16),
        scratch_shapes=[state],
        compiler_params=_params(("parallel", "arbitrary")),
        name="ssd_bwd",
    )(u, dt, dtb, alog, yf, z, dsk, nw)


def _layer_norm(h, g, b):
    mu = jnp.mean(h, axis=-1, keepdims=True)
    var = jnp.mean(jnp.square(h - mu), axis=-1, keepdims=True)
    return (h - mu) * lax.rsqrt(var + LN_EPS) * g + b


def _outproj_kernel(a_ref, y_ref, x_ref, w_ref, g_ref, b_ref, wr_ref, x1_ref, lg_ref):
    mix = jnp.dot(a_ref[...].astype(BF16), w_ref[:ATTN_WIDTH, :], preferred_element_type=F32)
    mix = mix + jnp.dot(y_ref[...], w_ref[ATTN_WIDTH:, :], preferred_element_type=F32)
    x1 = _layer_norm(ALPHA * x_ref[...] + mix, g_ref[...], b_ref[...])
    x1_ref[...] = x1
    x_hi = x1.astype(BF16)
    x_lo = (x1 - x_hi.astype(F32)).astype(BF16)
    w_hi = wr_ref[...].astype(BF16)
    w_lo = (wr_ref[...] - w_hi.astype(F32)).astype(BF16)
    nt_dims = (((1,), (1,)), ((), ()))
    lg_ref[...] = (lax.dot_general(w_hi, x_hi, nt_dims, preferred_element_type=F32)
                   + lax.dot_general(w_lo, x_hi, nt_dims, preferred_element_type=F32)
                   + lax.dot_general(w_hi, x_lo, nt_dims, preferred_element_type=F32))


def _outproj(attn, y, x2, w_out_b, g, b, wr_t, tm=512):
    T = x2.shape[0]
    row = lambda i: (i, 0)
    const = lambda i: (0, 0)
    return pl.pallas_call(
        _outproj_kernel,
        grid=(T // tm,),
        in_specs=[pl.BlockSpec((tm, ATTN_WIDTH), row),
                  pl.BlockSpec((tm, SSM_WIDTH), row),
                  pl.BlockSpec((tm, D_MODEL), row),
                  pl.BlockSpec((MIX_WIDTH, D_MODEL), const),
                  pl.BlockSpec((1, D_MODEL), const),
                  pl.BlockSpec((1, D_MODEL), const),
                  pl.BlockSpec((N_EXPERTS, D_MODEL), const)],
        out_specs=[pl.BlockSpec((tm, D_MODEL), row),
                   pl.BlockSpec((N_EXPERTS, tm), lambda i: (0, i))],
        out_shape=[jax.ShapeDtypeStruct((T, D_MODEL), F32),
                   jax.ShapeDtypeStruct((N_EXPERTS, T), F32)],
        compiler_params=_params(("parallel",)),
        name="outproj",
    )(attn, y, x2, w_out_b, g.reshape(1, D_MODEL), b.reshape(1, D_MODEL), wr_t)


SEL_BLK = 256


MIN_NORMAL_BITS = 0x00800000


def _select_kernel(lg_ref, aff_ref, pos_ref, off_ref, res_ref, *, cap, T):
    lg = lg_ref[...]
    ex = jnp.exp(lg - jnp.max(lg, axis=0, keepdims=True))
    aff_ref[...] = ex / jnp.sum(ex, axis=0, keepdims=True)

    def as_f32(bits):
        return lax.bitcast_convert_type(bits, F32)

    def kth_largest(ref):
        def search(i, t):
            cand = t | jnp.left_shift(jnp.int32(1), 30 - i)
            cnt = jnp.sum((ref[...] >= as_f32(cand)).astype(I32), axis=1, keepdims=True)
            return jnp.where((cnt >= cap) & (cand >= MIN_NORMAL_BITS), cand, t)

        return lax.fori_loop(0, 31, search, jnp.zeros((N_EXPERTS, 1), I32))

    thr1 = as_f32(kth_largest(aff_ref))
    res_ref[...] = aff_ref[...] - thr1
    thr2_bits = kth_largest(res_ref)
    thr2 = as_f32(thr2_bits)
    nxt2 = as_f32(jnp.where(thr2_bits == 0, MIN_NORMAL_BITS, thr2_bits + 1))
    n_gt = jnp.sum((res_ref[...] >= nxt2).astype(I32), axis=1, keepdims=True)
    need = (cap - n_gt).astype(F32)

    uj = lax.broadcasted_iota(I32, (SEL_BLK, SEL_BLK), 0)
    ut = lax.broadcasted_iota(I32, (SEL_BLK, SEL_BLK), 1)
    upper = (uj <= ut).astype(BF16)
    nblk = T // SEL_BLK
    olane = lax.broadcasted_iota(I32, off_ref.shape, 1)

    def blk(i, carry):
        c_gt, c_eq, offs = carry
        cols = pl.ds(pl.multiple_of(i * SEL_BLK, SEL_BLK), SEL_BLK)
        res = res_ref[:, cols]
        gt = res >= nxt2
        eq = (res >= thr2) & jnp.logical_not(gt)
        gt_f = gt.astype(F32)
        eq_f = eq.astype(F32)
        st = jnp.concatenate([gt_f, eq_f], axis=0).astype(BF16)
        cs = jnp.dot(st, upper, preferred_element_type=F32)
        gt_ex = c_gt + cs[:N_EXPERTS] - gt_f
        eq_ex = c_eq + cs[N_EXPERTS:] - eq_f
        sel = gt | (eq & (eq_ex < need))
        slot = gt_ex + jnp.minimum(eq_ex, need)
        pos_ref[:, cols] = jnp.where(sel, slot, -1.0).astype(I32)
        start = (c_gt + jnp.minimum(c_eq, need)).astype(I32)
        offs = jnp.where(olane == i, start, offs)
        return (c_gt + cs[:N_EXPERTS, SEL_BLK - 1:SEL_BLK], c_eq + cs[N_EXPERTS:, SEL_BLK - 1:SEL_BLK], offs)

    zero = jnp.zeros((N_EXPERTS, 1), F32)
    _, _, offs = lax.fori_loop(0, nblk, blk, (zero, zero, jnp.zeros(off_ref.shape, I32)))
    off_ref[...] = offs


def _select(lg_t, cap):
    T = lg_t.shape[1]
    nblk = T // SEL_BLK
    owidth = -(-nblk // LANES) * LANES
    return pl.pallas_call(
        functools.partial(_select_kernel, cap=cap, T=T),
        out_shape=[jax.ShapeDtypeStruct((N_EXPERTS, T), F32),
                   jax.ShapeDtypeStruct((N_EXPERTS, T), I32),
                   jax.ShapeDtypeStruct((N_EXPERTS, owidth), I32)],
        scratch_shapes=[pltpu.VMEM((N_EXPERTS, T), F32)],
        compiler_params=pltpu.CompilerParams(vmem_limit_bytes=VMEM_LIMIT),
        name="select",
    )(lg_t)


SLOT_TILE = 256
COMPACT_TILE = 128
COMPACT_STATIC = 2
MOE_TOK = 512
IDX_BASE = 256
SLOT_SUBTILES = 4


def _slots_kernel(off_ref, pos_ref, idx_ref, digits_ref, *, n_tok_tiles):
    e = pl.program_id(0)
    g = pl.program_id(1)

    @pl.when(g == 0)
    def _():
        idx_ref[...] = jnp.zeros(idx_ref.shape, F32)

    for s in range(SLOT_SUBTILES):
        i = g * SLOT_SUBTILES + s
        n0 = off_ref[e * (n_tok_tiles + 1) + i]
        n1 = off_ref[e * (n_tok_tiles + 1) + i + 1]
        j0 = n0 // COMPACT_TILE
        n_tiles = jnp.where(n1 > n0, (n1 - 1) // COMPACT_TILE - j0 + 1, 0)
        pos_row = pos_ref[pl.ds(e, 1), s * MOE_TOK:(s + 1) * MOE_TOK]
        tok = i * MOE_TOK + lax.broadcasted_iota(I32, (MOE_TOK, LANES), 0)
        col = lax.broadcasted_iota(I32, (MOE_TOK, LANES), 1)
        hi = jnp.right_shift(tok, IDX_BASE.bit_length() - 1)
        lo = jnp.bitwise_and(tok, IDX_BASE - 1)
        digits_ref[...] = jnp.where(col == 0, hi, jnp.where(col == 1, lo, 0)).astype(F32).astype(BF16)

        def tile(j, live, pos_row=pos_row):
            sub = lax.broadcasted_iota(I32, (COMPACT_TILE, MOE_TOK), 0)
            onehot = jnp.where((pos_row - j * COMPACT_TILE == sub) & live, 1.0, 0.0).astype(BF16)
            rows = pl.ds(pl.multiple_of(j * COMPACT_TILE, COMPACT_TILE), COMPACT_TILE)
            idx_ref[0, rows, :] += jnp.dot(onehot, digits_ref[...], preferred_element_type=F32)

        last = idx_ref.shape[1] // COMPACT_TILE - 1
        for jj in range(COMPACT_STATIC):
            tile(jnp.minimum(j0 + jj, last), jj < n_tiles)

        def rest(jj, carry, j0=j0, tile=tile):
            tile(j0 + jj, True)
            return carry

        lax.fori_loop(COMPACT_STATIC, n_tiles, rest, 0)


def _slot_tokens(pos, off_flat, cap):
    T = pos.shape[1]
    nt = T // MOE_TOK
    assert T // IDX_BASE <= IDX_BASE and nt % SLOT_SUBTILES == 0
    grid_spec = pltpu.PrefetchScalarGridSpec(
        num_scalar_prefetch=1,
        grid=(N_EXPERTS, nt // SLOT_SUBTILES),
        in_specs=[pl.BlockSpec((N_EXPERTS, SLOT_SUBTILES * MOE_TOK), lambda e, g, off: (0, g))],
        out_specs=pl.BlockSpec((1, cap, LANES), lambda e, i, off: (e, 0, 0)),
        scratch_shapes=[pltpu.VMEM((MOE_TOK, LANES), BF16)])
    idx = pl.pallas_call(
        functools.partial(_slots_kernel, n_tok_tiles=nt),
        grid_spec=grid_spec,
        out_shape=jax.ShapeDtypeStruct((N_EXPERTS, cap, LANES), F32),
        compiler_params=_params(("arbitrary", "arbitrary")),
        name="slots",
    )(off_flat, pos)
    return (idx[:, :, 0] * IDX_BASE + idx[:, :, 1]).astype(I32).reshape(-1)


MOE_SLOT = 512
FF_BLK = 256
N_FF = D_FF_EXPERT // FF_BLK
ROWS_PER_FF = 48
GATHER_ROWS = ROWS_PER_FF * N_FF
assert D_FF_EXPERT % FF_BLK == 0 and GATHER_ROWS >= MOE_SLOT and GATHER_ROWS % 8 == 0


def _moe_kernel(tok_ref, x_hbm, wg_ref, wu_ref, wd_ref, out_ref, xg_ref, xb_ref, acc_ref, sem, *, n_slot_tiles):
    step = pl.program_id(0) * n_slot_tiles + pl.program_id(1)
    total = N_EXPERTS * n_slot_tiles
    cur = step % 2

    def row_copy(tile_idx, r, buf):
        slot = jnp.minimum(r, MOE_SLOT - 1)
        tok = tok_ref[tile_idx * MOE_SLOT + slot]
        return pltpu.make_async_copy(x_hbm.at[pl.ds(tok, 1)], xg_ref.at[buf, pl.ds(r, 1)], sem.at[buf])

    def wait_tile(buf):
        pltpu.make_async_copy(x_hbm.at[pl.ds(0, GATHER_ROWS)], xg_ref.at[buf], sem.at[buf]).wait()

    @pl.when(step == 0)
    def _():
        def first(r, carry):
            row_copy(0, r, 0).start()
            return carry

        lax.fori_loop(0, GATHER_ROWS, first, 0)

    wait_tile(cur)
    nxt = (step + 1) % total
    xb_ref[...] = xg_ref[cur, :MOE_SLOT, :].astype(BF16)
    acc_ref[...] = jnp.zeros(acc_ref.shape, F32)

    def ff(c, carry):
        for r in range(ROWS_PER_FF):
            row_copy(nxt, c * ROWS_PER_FF + r, 1 - cur).start()
        cols = pl.ds(pl.multiple_of(c * FF_BLK, FF_BLK), FF_BLK)
        xb = xb_ref[...]
        g = jnp.dot(xb, wg_ref[0, :, cols], preferred_element_type=F32)
        u = jnp.dot(xb, wu_ref[0, :, cols], preferred_element_type=F32)
        h = ((g / (1.0 + jnp.exp(-g))) * u).astype(BF16)
        acc_ref[...] += jnp.dot(h, wd_ref[0, cols, :], preferred_element_type=F32)
        return carry

    lax.fori_loop(0, N_FF, ff, 0)
    out_ref[0] = acc_ref[...].astype(out_ref.dtype)

    @pl.when(step == total - 1)
    def _():
        wait_tile(1 - cur)


def _moe(x1, tok_flat, wg, wu, wd, cap):
    assert cap % MOE_SLOT == 0
    nj = cap // MOE_SLOT
    grid_spec = pltpu.PrefetchScalarGridSpec(
        num_scalar_prefetch=1,
        grid=(N_EXPERTS, nj),
        in_specs=[pl.BlockSpec(memory_space=pl.ANY),
                  pl.BlockSpec((1, D_MODEL, D_FF_EXPERT), lambda e, j, tok: (e, 0, 0)),
                  pl.BlockSpec((1, D_MODEL, D_FF_EXPERT), lambda e, j, tok: (e, 0, 0)),
                  pl.BlockSpec((1, D_FF_EXPERT, D_MODEL), lambda e, j, tok: (e, 0, 0))],
        out_specs=pl.BlockSpec((1, MOE_SLOT, D_MODEL), lambda e, j, tok: (e, j, 0)),
        scratch_shapes=[pltpu.VMEM((2, GATHER_ROWS, D_MODEL), F32),
                        pltpu.VMEM((MOE_SLOT, D_MODEL), BF16),
                        pltpu.VMEM((MOE_SLOT, D_MODEL), F32),
                        pltpu.SemaphoreType.DMA((2,))])
    return pl.pallas_call(
        functools.partial(_moe_kernel, n_slot_tiles=nj),
        grid_spec=grid_spec,
        out_shape=jax.ShapeDtypeStruct((N_EXPERTS, cap, D_MODEL), BF16),
        compiler_params=_params(("arbitrary", "arbitrary")),
        name="moe",
    )(tok_flat, x1, wg, wu, wd)


CMB_TOK = 256
ROW_CHUNK = 16
STAGE_ROWS = N_EXPERTS * (CMB_TOK + 2 * ROW_CHUNK)
NO_ROW = -(2 ** 30)


def _combine_kernel(off_ref, x1_ref, gid_ref, gate_ref, g_ref, b_ref, eo_ref, o_ref, stage_ref, w_ref, sem,
                    *, cap, n_tok_tiles):
    i = pl.program_id(0)
    cur = i % 2

    def chunk_copy(src_row, dst_row, buf):
        return pltpu.make_async_copy(eo_ref.at[pl.ds(src_row, ROW_CHUNK)],
                                     stage_ref.at[buf, pl.ds(dst_row, ROW_CHUNK)], sem.at[buf])

    def segments(tile):
        base = jnp.int32(0)
        segs = []
        for e in range(N_EXPERTS):
            n0 = off_ref[e * (n_tok_tiles + 1) + tile]
            n1 = off_ref[e * (n_tok_tiles + 1) + tile + 1]
            nch = (n1 + ROW_CHUNK - 1) // ROW_CHUNK - n0 // ROW_CHUNK
            segs.append((base, nch, e * cap + (n0 // ROW_CHUNK) * ROW_CHUNK))
            base = base + nch * ROW_CHUNK
        return segs, base

    def issue(tile, buf):
        segs, _ = segments(tile)
        for seg_base, nch, seg_row in segs:
            def one(c, carry, seg_base=seg_base, seg_row=seg_row):
                src = pl.multiple_of(seg_row + c * ROW_CHUNK, ROW_CHUNK)
                dst = pl.multiple_of(seg_base + c * ROW_CHUNK, ROW_CHUNK)
                chunk_copy(src, dst, buf).start()
                return carry

            lax.fori_loop(0, nch, one, 0)

    @pl.when(i == 0)
    def _():
        stage_ref[...] = jnp.zeros(stage_ref.shape, stage_ref.dtype)
        issue(0, 0)

    @pl.when(i + 1 < n_tok_tiles)
    def _():
        issue(i + 1, 1 - cur)

    segs, total = segments(i)

    def wait(c, carry):
        chunk_copy(0, 0, cur).wait()
        return carry

    lax.fori_loop(0, total // ROW_CHUNK, wait, 0)

    jsub = lax.broadcasted_iota(I32, (SLOT_TILE, CMB_TOK), 0)
    o_ref[...] = ALPHA * x1_ref[...]

    def kchunk(kc, carry):
        j0 = kc * SLOT_TILE
        w_ref[...] = jnp.zeros(w_ref.shape, F32)
        for e, (seg_base, nch, seg_row) in enumerate(segs):
            seg_end = seg_base + nch * ROW_CHUNK

            @pl.when((seg_base < j0 + SLOT_TILE) & (seg_end > j0))
            def _(e=e, seg_base=seg_base, seg_row=seg_row):
                row_id = jsub + (j0 + seg_row - seg_base)
                w_ref[...] = jnp.where(gid_ref[e:e + 1, :] == row_id, gate_ref[e:e + 1, :], w_ref[...])

        rows = stage_ref[cur, pl.ds(pl.multiple_of(j0, SLOT_TILE), SLOT_TILE), :]
        o_ref[...] += lax.dot_general(w_ref[...].astype(BF16), rows, (((0,), (0,)), ((), ())),
                                      preferred_element_type=F32)
        return carry

    lax.fori_loop(0, (total + SLOT_TILE - 1) // SLOT_TILE, kchunk, 0)
    o_ref[...] = _layer_norm(o_ref[...], g_ref[...], b_ref[...])


def _combine(x1, gid_tok, gate_tok, off_flat, eo_flat, g, b, cap):
    T = x1.shape[0]
    nt = T // CMB_TOK
    stage_rows = -(-STAGE_ROWS // SLOT_TILE) * SLOT_TILE
    row = lambda i, off: (i, 0)
    const = lambda i, off: (0, 0)
    grid_spec = pltpu.PrefetchScalarGridSpec(
        num_scalar_prefetch=1,
        grid=(nt,),
        in_specs=[pl.BlockSpec((CMB_TOK, D_MODEL), row),
                  pl.BlockSpec((N_EXPERTS, CMB_TOK), lambda i, off: (0, i)),
                  pl.BlockSpec((N_EXPERTS, CMB_TOK), lambda i, off: (0, i)),
                  pl.BlockSpec((1, D_MODEL), const),
                  pl.BlockSpec((1, D_MODEL), const),
                  pl.BlockSpec(memory_space=pl.ANY)],
        out_specs=pl.BlockSpec((CMB_TOK, D_MODEL), row),
        scratch_shapes=[pltpu.VMEM((2, stage_rows, D_MODEL), BF16),
                        pltpu.VMEM((SLOT_TILE, CMB_TOK), F32),
                        pltpu.SemaphoreType.DMA((2,))])
    return pl.pallas_call(
        functools.partial(_combine_kernel, cap=cap, n_tok_tiles=nt),
        grid_spec=grid_spec,
        out_shape=jax.ShapeDtypeStruct((T, D_MODEL), F32),
        compiler_params=_params(("arbitrary",)),
        name="combine",
    )(off_flat, x1, gid_tok, gate_tok, g.reshape(1, D_MODEL), b.reshape(1, D_MODEL), eo_flat)


def _tile_offsets(off256, cap, T, tok):
    step = tok // SEL_BLK
    o = off256[:, :T // SEL_BLK:step]
    o = jnp.concatenate([o, jnp.full((N_EXPERTS, 1), cap, I32)], axis=1)
    return o.reshape(-1)


def _trunk(x, w_in_p, conv_w, conv_b, dt_bias, a_log, d_skip, ssm_norm_w, w_out_b, ln1_g, ln1_b,
           wr_t, wg, wu, wd, ln2_g, ln2_b):
    B, seq, _ = x.shape
    T = B * seq
    cap = CAPACITY_FACTOR * T // N_EXPERTS
    x2 = x.reshape(T, D_MODEL)
    q, k, v, z, xbc, dt = _inproj(x2, w_in_p, seq)
    shp = (B, seq // RES, RES * ATTN_WIDTH)
    attn = _attention(q.reshape(shp), k.reshape(shp), v.reshape(shp)).reshape(T, ATTN_WIDTH)
    y = _ssd(xbc, dt, z, conv_w, conv_b, dt_bias, a_log, d_skip, ssm_norm_w, B, seq)
    x1, lg_t = _outproj(attn, y, x2, w_out_b, ln1_g, ln1_b, wr_t)
    aff_t, pos, off256 = _select(lg_t, cap)
    tok_flat = _slot_tokens(pos, _tile_offsets(off256, cap, T, MOE_TOK), cap)
    eo = _moe(x1, tok_flat, wg, wu, wd, cap)
    gid = jnp.where(pos >= 0, pos + jnp.arange(N_EXPERTS, dtype=I32)[:, None] * cap, NO_ROW)
    out = _combine(x1, gid, aff_t, _tile_offsets(off256, cap, T, CMB_TOK),
                   eo.reshape(N_EXPERTS * cap, D_MODEL), ln2_g, ln2_b, cap)
    return out.reshape(B, seq, D_MODEL)


def kernel(x_prompt, x_sample, w_in, conv_w, conv_b, dt_bias, a_log, d_skip, ssm_norm_w, w_out, ln1_g, ln1_b,
           w_router, w_gate, w_up, w_down, ln2_g, ln2_b):
    assert DEPTH == 1
    l = 0
    w_in_p = jnp.pad(w_in[l], ((0, 0), (0, IN_PAD - w_in.shape[-1]))).astype(BF16)
    args = (w_in_p, conv_w[l], conv_b[l], dt_bias[l], a_log[l], d_skip[l], ssm_norm_w[l],
            w_out[l].astype(BF16), ln1_g[l], ln1_b[l], w_router[l].T,
            w_gate[l].astype(BF16), w_up[l].astype(BF16), w_down[l].astype(BF16), ln2_g[l], ln2_b[l])
    return (_trunk(x_prompt, *args), _trunk(x_sample, *args))
```

```python
import functools

import jax
import jax.numpy as jnp
from jax import lax
from jax.experimental import pallas as pl
from jax.experimental.pallas import tpu as pltpu

F32 = jnp.float32
BF16 = jnp.bfloat16
I32 = jnp.int32
HIGHEST = lax.Precision.HIGHEST

D_MODEL = 1024
DEPTH = 1
HEAD_DIM = 64
N_ATTN_HEADS = 8
ATTN_WIDTH = N_ATTN_HEADS * HEAD_DIM
ROT_DIM = HEAD_DIM // 4
ROPE_THETA = 500000.0
DILATED_PATTERNS = ((128, 1), (512, 4), (2048, 16))
N_SSM_HEADS = 8
SSM_HEAD_DIM = 64
SSM_WIDTH = N_SSM_HEADS * SSM_HEAD_DIM
SSM_GROUPS = 2
D_STATE = 128
CONV_K = 5
CHUNK = 128
CONV_CH = SSM_WIDTH + 2 * SSM_GROUPS * D_STATE
MIX_WIDTH = ATTN_WIDTH + SSM_WIDTH
N_EXPERTS = 16
CAPACITY_FACTOR = 2
D_FF_EXPERT = 2816
ALPHA = (2.0 * DEPTH) ** 0.25
LN_EPS = 1e-5
RMS_EPS = 1e-5

LANES = 128
VMEM_LIMIT = 56 * 1024 * 1024
NEG = -1e30
HALF_WIN = 64
RES = 4

IN_PAD = 3 * ATTN_WIDTH + SSM_WIDTH + CONV_CH + LANES


def _params(sem):
    return pltpu.CompilerParams(dimension_semantics=sem, vmem_limit_bytes=VMEM_LIMIT)


def _inproj_kernel(x_ref, w_ref, c_ref, sa_ref, sb_ref, q_ref, k_ref, v_ref, z_ref, xbc_ref, dt_ref, tmp_ref):
    x = x_ref[...].astype(BF16)
    tm = x_ref.shape[0]

    def put_grouped(out_ref, val):
        for g in range(ATTN_WIDTH // LANES):
            tmp_ref[g] = val[:, g * LANES:(g + 1) * LANES]
        for r in range(RES):
            for g in range(ATTN_WIDTH // LANES):
                lo = r * ATTN_WIDTH + g * LANES
                out_ref[:, lo:lo + LANES] = tmp_ref[g, pl.ds(r, tm // RES, stride=RES), :]

    def seg(lo, hi):
        return jnp.dot(x, w_ref[:, lo:hi], preferred_element_type=F32)

    c = c_ref[...]
    sa = sa_ref[...]
    sb = sb_ref[...]

    def rotary(t, scale):
        outs = []
        for g in range(ATTN_WIDTH // LANES):
            tg = t[:, g * LANES:(g + 1) * LANES]
            up = pltpu.roll(tg, LANES - ROT_DIM // 2, 1)
            dn = pltpu.roll(tg, ROT_DIM // 2, 1)
            outs.append((tg * c + up * sa + dn * sb) * scale)
        return jnp.concatenate(outs, axis=1)

    a = ATTN_WIDTH
    put_grouped(q_ref, rotary(seg(0, a), HEAD_DIM ** -0.5))
    put_grouped(k_ref, rotary(seg(a, 2 * a), 1.0))
    put_grouped(v_ref, seg(2 * a, 3 * a))
    z_ref[...] = seg(3 * a, 3 * a + SSM_WIDTH)
    o = 3 * a + SSM_WIDTH
    xbc_ref[...] = seg(o, o + CONV_CH)
    dt_ref[...] = seg(o + CONV_CH, o + CONV_CH + LANES)


def _rotary_tables(seq):
    half = ROT_DIM // 2
    inv = ROPE_THETA ** (-jnp.arange(half, dtype=F32) * 2.0 / ROT_DIM)
    ang = jnp.arange(seq).astype(F32)[:, None] * inv[None, :]
    cos = jnp.cos(ang)
    sin = jnp.sin(ang)
    m = jnp.arange(LANES) % HEAD_DIM
    c = jnp.where(m[None, :] < ROT_DIM, cos[:, m % half], 1.0)
    sa = jnp.where(m[None, :] < half, -sin[:, m % half], 0.0)
    sb = jnp.where((m[None, :] >= half) & (m[None, :] < ROT_DIM), sin[:, m % half], 0.0)
    return c.astype(F32), sa.astype(F32), sb.astype(F32)


def _inproj(x2, w_pad, seq, tm=512):
    T = x2.shape[0]
    c, sa, sb = _rotary_tables(seq)
    nseq = seq // tm
    row = lambda i: (i, 0)
    tab = lambda i: (i % nseq, 0)
    outs = [jax.ShapeDtypeStruct((T // RES, RES * ATTN_WIDTH), F32)] * 3 + [
        jax.ShapeDtypeStruct((T, SSM_WIDTH), F32),
        jax.ShapeDtypeStruct((T, CONV_CH), F32),
        jax.ShapeDtypeStruct((T, LANES), F32)]
    return pl.pallas_call(
        _inproj_kernel,
        grid=(T // tm,),
        in_specs=[pl.BlockSpec((tm, D_MODEL), row),
                  pl.BlockSpec((D_MODEL, IN_PAD), lambda i: (0, 0)),
                  pl.BlockSpec((tm, LANES), tab),
                  pl.BlockSpec((tm, LANES), tab),
                  pl.BlockSpec((tm, LANES), tab)],
        out_specs=[pl.BlockSpec((tm // RES, RES * ATTN_WIDTH), row)] * 3 + [
            pl.BlockSpec((tm, SSM_WIDTH), row),
            pl.BlockSpec((tm, CONV_CH), row),
            pl.BlockSpec((tm, LANES), row)],
        out_shape=outs,
        scratch_shapes=[pltpu.VMEM((ATTN_WIDTH // LANES, tm, LANES), F32)],
        compiler_params=_params(("parallel",)),
        name="inproj",
    )(x2, w_pad, c, sa, sb)


QB = 128
KB = 256
ATTN_GROUP = 8


def _attn_kernel(q_hbm, k_hbm, v_hbm, o_ref, qa, ka, va, m_ref, l_ref, acc_ref, bias_ref, sem, *, seq):
    b = pl.program_id(0)
    h = pl.program_id(1)
    sub = seq // RES
    npair = ATTN_WIDTH // LANES

    copies = []
    for src, dst in ((q_hbm, qa), (k_hbm, ka), (v_hbm, va)):
        for r in range(RES):
            lanes = pl.ds(pl.multiple_of((r * npair + h) * LANES, LANES), LANES)
            copies.append(pltpu.make_async_copy(src.at[b, :, lanes], dst.at[pl.ds(r * sub, sub), :], sem))
    for c in copies:
        c.start()

    row = lax.broadcasted_iota(I32, (2 * QB, KB), 0)
    col = lax.broadcasted_iota(I32, (2 * QB, KB), 1)
    iq = jnp.where(row >= QB, row - QB, row)
    iq_runs = RES * (iq % (QB // RES)) + iq // (QB // RES)
    ik_runs = RES * (col % (KB // RES)) + col // (KB // RES)
    for di in range(3):
        bias_ref[0, di] = jnp.where(jnp.abs(di * HALF_WIN + iq - col) <= HALF_WIN, 0.0, NEG).astype(F32)
        bias_ref[1, di] = jnp.where(jnp.abs(di * HALF_WIN + iq_runs - ik_runs) <= HALF_WIN, 0.0, NEG).astype(F32)
    for c in copies:
        c.wait()

    lane = lax.broadcasted_iota(I32, (QB, LANES), 1)
    head0 = lane < HEAD_DIM
    ones = jnp.ones((KB, LANES), BF16)

    def load(ref, runs, *lead):
        parts = [ref[lead + (rn, slice(None))] for rn in runs]
        return parts[0] if len(parts) == 1 else jnp.concatenate(parts, axis=0)

    def store(ref, runs, val, *lead):
        at = 0
        for rn, size in runs:
            ref[lead + (rn, slice(None))] = val[at:at + size]
            at += size

    for window, dil in DILATED_PATTERNS:
        assert window // (2 * dil) == HALF_WIN and (dil == 1 or dil % RES == 0)
        sub_len = seq // dil
        nblk = sub_len // QB
        assert sub_len >= KB and sub_len % QB == 0 and (dil * nblk) % ATTN_GROUP == 0

        def block_rows(idx, dil=dil, sub_len=sub_len, nblk=nblk):
            r = idx // nblk
            n = idx - r * nblk
            kb = jnp.clip(n * QB - HALF_WIN, 0, sub_len - KB)
            di = (n * QB - kb) // HALF_WIN
            if dil == 1:
                qn, kn = QB // RES, KB // RES
                q_runs = [(pl.ds(pl.multiple_of(c * sub + n * qn, qn), qn), qn) for c in range(RES)]
                k_runs = [pl.ds(pl.multiple_of(c * sub + kb // RES, 8), kn) for c in range(RES)]
                return q_runs, k_runs, bias_ref[1, di]
            step = dil // RES
            base = (r % RES) * sub + r // RES
            if step == 1:
                q_runs = [(pl.ds(pl.multiple_of(base + n * QB, QB), QB), QB)]
                k_runs = [pl.ds(pl.multiple_of(base + kb, HALF_WIN), KB)]
            else:
                q_runs = [(pl.ds(base + step * QB * n, QB, stride=step), QB)]
                k_runs = [pl.ds(base + step * kb, KB, stride=step)]
            return q_runs, k_runs, bias_ref[0, di]

        first = (window, dil) == DILATED_PATTERNS[0]
        assert not first or dil == 1

        def body(gi, carry, block_rows=block_rows, first=first):
            loaded = []
            for u in range(ATTN_GROUP):
                q_runs, k_runs, bias = block_rows(gi * ATTN_GROUP + u)
                qr = [rn for rn, _ in q_runs]
                state = () if first else (load(m_ref, qr, 0), load(m_ref, qr, 1), load(l_ref, qr, 0),
                                          load(l_ref, qr, 1), load(acc_ref, qr))
                loaded.append((q_runs, load(qa, qr), load(ka, k_runs), load(va, k_runs), bias, state))
            results = []
            for q_runs, q, k, v, bias, state in loaded:
                qs = jnp.concatenate([jnp.where(head0, q, 0.0), jnp.where(head0, 0.0, q)], axis=0).astype(BF16)
                s = lax.dot_general(qs, k.astype(BF16), (((1,), (1,)), ((), ())), preferred_element_type=F32)
                s = s + bias
                m_cur = jnp.max(s, axis=1, keepdims=True)
                if first:
                    m_new = jnp.broadcast_to(m_cur, (2 * QB, LANES))
                else:
                    m0, m1, l0, l1, acc = state
                    m_prev = jnp.concatenate([m0, m1], axis=0)
                    m_new = jnp.maximum(m_prev, m_cur)
                p = jnp.exp(s - jnp.concatenate([m_new, m_new], axis=1))
                pv = jnp.dot(p.astype(BF16), jnp.concatenate([v.astype(BF16), ones], axis=1),
                             preferred_element_type=F32)
                pv_l = jnp.where(head0, pv[:QB, :LANES], pv[QB:, :LANES])
                if first:
                    results.append((q_runs, pv_l, m_new, pv[:, LANES:]))
                else:
                    alpha = jnp.exp(m_prev - m_new)
                    l_new = alpha * jnp.concatenate([l0, l1], axis=0) + pv[:, LANES:]
                    a_l = jnp.where(head0, alpha[:QB], alpha[QB:])
                    results.append((q_runs, a_l * acc + pv_l, m_new, l_new))
            for q_runs, acc_new, m_new, l_new in results:
                store(acc_ref, q_runs, acc_new)
                store(m_ref, q_runs, m_new[:QB], 0)
                store(m_ref, q_runs, m_new[QB:], 1)
                store(l_ref, q_runs, l_new[:QB], 0)
                store(l_ref, q_runs, l_new[QB:], 1)
            return carry

        lax.fori_loop(0, dil * nblk // ATTN_GROUP, body, 0)

    for r in range(RES):
        def fin(i, carry, r=r):
            rws = pl.ds(pl.multiple_of(r * sub + i * QB, QB), QB)
            den = jnp.where(head0, l_ref[0, rws, :], l_ref[1, rws, :])
            o_ref[0, pl.ds(r + RES * QB * i, QB, stride=RES), :] = acc_ref[rws, :] / den
            return carry

        lax.fori_loop(0, sub // QB, fin, 0)


def _attention(q, k, v):
    B, sub, _ = q.shape
    seq = sub * RES
    npair = ATTN_WIDTH // LANES
    any_spec = pl.BlockSpec(memory_space=pl.ANY)
    return pl.pallas_call(
        functools.partial(_attn_kernel, seq=seq),
        grid=(B, npair),
        in_specs=[any_spec, any_spec, any_spec],
        out_specs=pl.BlockSpec((1, seq, LANES), lambda b, h: (b, 0, h)),
        out_shape=jax.ShapeDtypeStruct((B, seq, ATTN_WIDTH), F32),
        scratch_shapes=[pltpu.VMEM((seq, LANES), F32),
                        pltpu.VMEM((seq, LANES), F32),
                        pltpu.VMEM((seq, LANES), F32),
                        pltpu.VMEM((2, seq, LANES), F32),
                        pltpu.VMEM((2, seq, LANES), F32),
                        pltpu.VMEM((seq, LANES), F32),
                        pltpu.VMEM((2, 3, 2 * QB, KB), F32),
                        pltpu.SemaphoreType.DMA(())],
        compiler_params=_params(("parallel", "parallel")),
        name="attn",
    )(q, k, v)


SSD_TILE = 512
HALO = 8


def _ssd_kernel(*refs, reverse):
    if reverse:
        u_ref, dt_ref, dtb_ref, alog_ref, yf_ref, z_ref, dsk_ref, nw_ref, y_ref, st_ref = refs
    else:
        (xbc_ref, hp_ref, hn_ref, dt_ref, cw_ref, cb_ref, dtb_ref, alog_ref,
         y_ref, u_ref, ext_ref, st_ref) = refs
    lane_off = N_SSM_HEADS if reverse else 0
    nchunk = SSD_TILE // CHUNK

    @pl.when(pl.program_id(1) == 0)
    def _():
        st_ref[...] = jnp.zeros(st_ref.shape, F32)

    if not reverse:
        ext_ref[0:HALO, :] = hp_ref[0]
        ext_ref[HALO:HALO + SSD_TILE, :] = xbc_ref[...]
        ext_ref[HALO + SSD_TILE:HALO + SSD_TILE + HALO, :] = hn_ref[0]

    li = lax.broadcasted_iota(I32, (CHUNK, CHUNK), 0)
    si = lax.broadcasted_iota(I32, (CHUNK, CHUNK), 1)
    tri = (si >= li) if reverse else (si <= li)
    tri_f = tri.astype(F32)
    ej = lax.broadcasted_iota(I32, (LANES, SSM_WIDTH), 0)
    ec = lax.broadcasted_iota(I32, (LANES, SSM_WIDTH), 1)
    expand = (ej == lane_off + ec // SSM_HEAD_DIM).astype(BF16)
    lane = lax.broadcasted_iota(I32, (CHUNK, LANES), 1)
    lo_half = lane < SSM_HEAD_DIM
    a_row = -jnp.exp(alog_ref[...])
    gw = SSM_WIDTH // SSM_GROUPS
    heads_per_group = N_SSM_HEADS // SSM_GROUPS

    order = range(nchunk - 1, -1, -1) if reverse else range(nchunk)
    for c in order:
        crow = pl.ds(c * CHUNK, CHUNK)
        if reverse:
            u = u_ref[crow, :]
        else:
            base = HALO + c * CHUNK
            conv = None
            for kk in range(CONV_K):
                tap = ext_ref[pl.ds(base + kk - CONV_K // 2, CHUNK), :] * cw_ref[kk:kk + 1, :]
                conv = tap if conv is None else conv + tap
            u = conv + cb_ref[...]
            u = u / (1.0 + jnp.exp(-u))
            u_ref[crow, :] = u
        xs = u[:, :SSM_WIDTH]
        bm = u[:, SSM_WIDTH:SSM_WIDTH + SSM_GROUPS * D_STATE]
        cm = u[:, SSM_WIDTH + SSM_GROUPS * D_STATE:]

        dpre = dt_ref[pl.ds(c * CHUNK, CHUNK), :] + dtb_ref[...]
        dtv = jnp.maximum(dpre, 0.0) + jnp.log(1.0 + jnp.exp(-jnp.abs(dpre)))
        a = dtv * a_row
        acs = jnp.dot(tri_f, a, precision=HIGHEST, preferred_element_type=F32)
        acs_t = acs.T
        tot = acs[0:1, :] if reverse else acs[CHUNK - 1:CHUNK, :]
        stacked = jnp.concatenate([dtv, jnp.exp(acs), jnp.exp(tot - acs)], axis=0)
        s_hi = stacked.astype(BF16)
        s_lo = (stacked - s_hi.astype(F32)).astype(BF16)
        ex = (jnp.dot(s_hi, expand, preferred_element_type=F32)
              + jnp.dot(s_lo, expand, preferred_element_type=F32))
        dt_x = ex[:CHUNK]
        eacs_x = ex[CHUNK:2 * CHUNK]
        dend_x = ex[2 * CHUNK:]
        cdec_x = eacs_x[0:1, :] if reverse else eacs_x[CHUNK - 1:CHUNK, :]
        xdt = xs * dt_x
        wst = (dend_x * xdt).astype(BF16)
        st_prev = st_ref[...]
        st_b = st_prev.astype(BF16)

        ygs = []
        st_new = []
        for g in range(SSM_GROUPS):
            ys = []
            bg = bm[:, g * D_STATE:(g + 1) * D_STATE]
            cg = cm[:, g * D_STATE:(g + 1) * D_STATE].astype(BF16)
            cb = lax.dot_general(cg, bg.astype(BF16), (((1,), (1,)), ((), ())), preferred_element_type=F32)
            ms = []
            for hh in range(heads_per_group):
                h = lane_off + g * heads_per_group + hh
                diff = acs[:, h:h + 1] - acs_t[h:h + 1, :]
                lm = jnp.exp(jnp.where(tri, diff, NEG))
                ms.append((cb * lm).astype(BF16))
            for pp in range(heads_per_group // 2):
                hp = g * (heads_per_group // 2) + pp
                lhs = jnp.concatenate([ms[2 * pp], ms[2 * pp + 1]], axis=1)
                xp = xdt[:, hp * LANES:(hp + 1) * LANES]
                rhs = jnp.concatenate([jnp.where(lo_half, xp, 0.0), jnp.where(lo_half, 0.0, xp)],
                                      axis=0).astype(BF16)
                ys.append(jnp.dot(lhs, rhs, preferred_element_type=F32))
            sc = jnp.dot(bg.T.astype(BF16), wst[:, g * gw:(g + 1) * gw], preferred_element_type=F32)
            yoff = jnp.dot(cg, st_b[:, g * gw:(g + 1) * gw], preferred_element_type=F32)
            ygs.append(jnp.concatenate(ys, axis=1) + yoff * eacs_x[:, g * gw:(g + 1) * gw])
            st_new.append(st_prev[:, g * gw:(g + 1) * gw] * cdec_x[:, g * gw:(g + 1) * gw] + sc)
        st_ref[...] = jnp.concatenate(st_new, axis=1)
        y = jnp.concatenate(ygs, axis=1)

        crow = pl.ds(c * CHUNK, CHUNK)
        if reverse:
            ytot = yf_ref[crow, :] + y + dsk_ref[...] * xs
            zz = z_ref[crow, :]
            yz = ytot * (zz / (1.0 + jnp.exp(-zz)))
            yn = yz * lax.rsqrt(jnp.mean(jnp.square(yz), axis=-1, keepdims=True) + RMS_EPS) * nw_ref[...]
            y_ref[crow, :] = yn.astype(y_ref.dtype)
        else:
            y_ref[crow, :] = y


def _ssd(xbc, dt, z, conv_w, conv_b, dt_bias, a_log, d_skip, ssm_norm_w, B, seq):
    T = B * seq
    nt = seq // SSD_TILE
    x4 = xbc.reshape(B, nt, SSD_TILE, CONV_CH)
    zeros = jnp.zeros((B, 1, HALO, CONV_CH), F32)
    hprev = jnp.concatenate([zeros, x4[:, :-1, SSD_TILE - HALO:, :]], axis=1).reshape(B * nt, HALO, CONV_CH)
    hnext = jnp.concatenate([x4[:, 1:, :HALO, :], zeros], axis=1).reshape(B * nt, HALO, CONV_CH)
    cw = jnp.zeros((HALO, CONV_CH), F32).at[:CONV_K].set(conv_w)
    cb = conv_b.reshape(1, CONV_CH)
    pad = LANES - 2 * N_SSM_HEADS
    dtb = jnp.pad(dt_bias.reshape(1, 2 * N_SSM_HEADS), ((0, 0), (0, pad)))
    alog = jnp.pad(a_log.reshape(1, 2 * N_SSM_HEADS), ((0, 0), (0, pad)), constant_values=-1e30)
    dsk = jnp.repeat(d_skip, SSM_HEAD_DIM).reshape(1, SSM_WIDTH)
    nw = ssm_norm_w.reshape(1, SSM_WIDTH)

    const = lambda b, i: (0, 0)
    state = pltpu.VMEM((D_STATE, SSM_WIDTH), F32)

    fmap = lambda b, i: (b * nt + i, 0)
    hmap = lambda b, i: (b * nt + i, 0, 0)
    yf, u = pl.pallas_call(
        functools.partial(_ssd_kernel, reverse=False),
        grid=(B, nt),
        in_specs=[pl.BlockSpec((SSD_TILE, CONV_CH), fmap),
                  pl.BlockSpec((1, HALO, CONV_CH), hmap),
                  pl.BlockSpec((1, HALO, CONV_CH), hmap),
                  pl.BlockSpec((SSD_TILE, LANES), fmap),
                  pl.BlockSpec((HALO, CONV_CH), const),
                  pl.BlockSpec((1, CONV_CH), const),
                  pl.BlockSpec((1, LANES), const),
                  pl.BlockSpec((1, LANES), const)],
        out_specs=[pl.BlockSpec((SSD_TILE, SSM_WIDTH), fmap), pl.BlockSpec((SSD_TILE, CONV_CH), fmap)],
        out_shape=[jax.ShapeDtypeStruct((T, SSM_WIDTH), F32), jax.ShapeDtypeStruct((T, CONV_CH), F32)],
        scratch_shapes=[pltpu.VMEM((SSD_TILE + 2 * HALO, CONV_CH), F32), state],
        compiler_params=_params(("parallel", "arbitrary")),
        name="ssd_fwd",
    )(xbc, hprev, hnext, dt, cw, cb, dtb, alog)

    rmap = lambda b, i: (b * nt + nt - 1 - i, 0)
    return pl.pallas_call(
        functools.partial(_ssd_kernel, reverse=True),
        grid=(B, nt),
        in_specs=[pl.BlockSpec((SSD_TILE, CONV_CH), rmap),
                  pl.BlockSpec((SSD_TILE, LANES), rmap),
                  pl.BlockSpec((1, LANES), const),
                  pl.BlockSpec((1, LANES), const),
                  pl.BlockSpec((SSD_TILE, SSM_WIDTH), rmap),
                  pl.BlockSpec((SSD_TILE, SSM_WIDTH), rmap),
                  pl.BlockSpec((1, SSM_WIDTH), const),
                  pl.BlockSpec((1, SSM_WIDTH), const)],
        out_specs=pl.BlockSpec((SSD_TILE, SSM_WIDTH), rmap),
        out_shape=jax.ShapeDtypeStruct((T, SSM_WIDTH), BF16),
        scratch_shapes=[state],
        compiler_params=_params(("parallel", "arbitrary")),
        name="ssd_bwd",
    )(u, dt, dtb, alog, yf, z, dsk, nw)


def _layer_norm(h, g, b):
    mu = jnp.mean(h, axis=-1, keepdims=True)
    var = jnp.mean(jnp.square(h - mu), axis=-1, keepdims=True)
    return (h - mu) * lax.rsqrt(var + LN_EPS) * g + b


def _outproj_kernel(a_ref, y_ref, x_ref, w_ref, g_ref, b_ref, wr_ref, x1_ref, lg_ref):
    mix = jnp.dot(a_ref[...].astype(BF16), w_ref[:ATTN_WIDTH, :], preferred_element_type=F32)
    mix = mix + jnp.dot(y_ref[...], w_ref[ATTN_WIDTH:, :], preferred_element_type=F32)
    x1 = _layer_norm(ALPHA * x_ref[...] + mix, g_ref[...], b_ref[...])
    x1_ref[...] = x1
    x_hi = x1.astype(BF16)
    x_lo = (x1 - x_hi.astype(F32)).astype(BF16)
    w_hi = wr_ref[...].astype(BF16)
    w_lo = (wr_ref[...] - w_hi.astype(F32)).astype(BF16)
    nt_dims = (((1,), (1,)), ((), ()))
    lg_ref[...] = (lax.dot_general(w_hi, x_hi, nt_dims, preferred_element_type=F32)
                   + lax.dot_general(w_lo, x_hi, nt_dims, preferred_element_type=F32)
                   + lax.dot_general(w_hi, x_lo, nt_dims, preferred_element_type=F32))


def _outproj(attn, y, x2, w_out_b, g, b, wr_t, tm=512):
    T = x2.shape[0]
    row = lambda i: (i, 0)
    const = lambda i: (0, 0)
    return pl.pallas_call(
        _outproj_kernel,
        grid=(T // tm,),
        in_specs=[pl.BlockSpec((tm, ATTN_WIDTH), row),
                  pl.BlockSpec((tm, SSM_WIDTH), row),
                  pl.BlockSpec((tm, D_MODEL), row),
                  pl.BlockSpec((MIX_WIDTH, D_MODEL), const),
                  pl.BlockSpec((1, D_MODEL), const),
                  pl.BlockSpec((1, D_MODEL), const),
                  pl.BlockSpec((N_EXPERTS, D_MODEL), const)],
        out_specs=[pl.BlockSpec((tm, D_MODEL), row),
                   pl.BlockSpec((N_EXPERTS, tm), lambda i: (0, i))],
        out_shape=[jax.ShapeDtypeStruct((T, D_MODEL), F32),
                   jax.ShapeDtypeStruct((N_EXPERTS, T), F32)],
        compiler_params=_params(("parallel",)),
        name="outproj",
    )(attn, y, x2, w_out_b, g.reshape(1, D_MODEL), b.reshape(1, D_MODEL), wr_t)


SEL_BLK = 256


MIN_NORMAL_BITS = 0x00800000


def _select_kernel(lg_ref, aff_ref, pos_ref, off_ref, res_ref, *, cap, T):
    lg = lg_ref[...]
    ex = jnp.exp(lg - jnp.max(lg, axis=0, keepdims=True))
    aff_ref[...] = ex / jnp.sum(ex, axis=0, keepdims=True)

    def as_f32(bits):
        return lax.bitcast_convert_type(bits, F32)

    def kth_largest(ref):
        def search(i, t):
            cand = t | jnp.left_shift(jnp.int32(1), 30 - i)
            cnt = jnp.sum((ref[...] >= as_f32(cand)).astype(I32), axis=1, keepdims=True)
            return jnp.where((cnt >= cap) & (cand >= MIN_NORMAL_BITS), cand, t)

        return lax.fori_loop(0, 31, search, jnp.zeros((N_EXPERTS, 1), I32))

    thr1 = as_f32(kth_largest(aff_ref))
    res_ref[...] = aff_ref[...] - thr1
    thr2_bits = kth_largest(res_ref)
    thr2 = as_f32(thr2_bits)
    nxt2 = as_f32(jnp.where(thr2_bits == 0, MIN_NORMAL_BITS, thr2_bits + 1))
    n_gt = jnp.sum((res_ref[...] >= nxt2).astype(I32), axis=1, keepdims=True)
    need = (cap - n_gt).astype(F32)

    uj = lax.broadcasted_iota(I32, (SEL_BLK, SEL_BLK), 0)
    ut = lax.broadcasted_iota(I32, (SEL_BLK, SEL_BLK), 1)
    upper = (uj <= ut).astype(BF16)
    nblk = T // SEL_BLK
    olane = lax.broadcasted_iota(I32, off_ref.shape, 1)

    def blk(i, carry):
        c_gt, c_eq, offs = carry
        cols = pl.ds(pl.multiple_of(i * SEL_BLK, SEL_BLK), SEL_BLK)
        res = res_ref[:, cols]
        gt = res >= nxt2
        eq = (res >= thr2) & jnp.logical_not(gt)
        gt_f = gt.astype(F32)
        eq_f = eq.astype(F32)
        st = jnp.concatenate([gt_f, eq_f], axis=0).astype(BF16)
        cs = jnp.dot(st, upper, preferred_element_type=F32)
        gt_ex = c_gt + cs[:N_EXPERTS] - gt_f
        eq_ex = c_eq + cs[N_EXPERTS:] - eq_f
        sel = gt | (eq & (eq_ex < need))
        slot = gt_ex + jnp.minimum(eq_ex, need)
        pos_ref[:, cols] = jnp.where(sel, slot, -1.0).astype(I32)
        start = (c_gt + jnp.minimum(c_eq, need)).astype(I32)
        offs = jnp.where(olane == i, start, offs)
        return (c_gt + cs[:N_EXPERTS, SEL_BLK - 1:SEL_BLK], c_eq + cs[N_EXPERTS:, SEL_BLK - 1:SEL_BLK], offs)

    zero = jnp.zeros((N_EXPERTS, 1), F32)
    _, _, offs = lax.fori_loop(0, nblk, blk, (zero, zero, jnp.zeros(off_ref.shape, I32)))
    off_ref[...] = offs


def _select(lg_t, cap):
    T = lg_t.shape[1]
    nblk = T // SEL_BLK
    owidth = -(-nblk // LANES) * LANES
    return pl.pallas_call(
        functools.partial(_select_kernel, cap=cap, T=T),
        out_shape=[jax.ShapeDtypeStruct((N_EXPERTS, T), F32),
                   jax.ShapeDtypeStruct((N_EXPERTS, T), I32),
                   jax.ShapeDtypeStruct((N_EXPERTS, owidth), I32)],
        scratch_shapes=[pltpu.VMEM((N_EXPERTS, T), F32)],
        compiler_params=pltpu.CompilerParams(vmem_limit_bytes=VMEM_LIMIT),
        name="select",
    )(lg_t)


SLOT_TILE = 256
COMPACT_TILE = 128
COMPACT_STATIC = 2
MOE_TOK = 512
IDX_BASE = 256
SLOT_SUBTILES = 4


def _slots_kernel(off_ref, pos_ref, idx_ref, digits_ref, *, n_tok_tiles):
    e = pl.program_id(0)
    g = pl.program_id(1)

    @pl.when(g == 0)
    def _():
        idx_ref[...] = jnp.zeros(idx_ref.shape, F32)

    for s in range(SLOT_SUBTILES):
        i = g * SLOT_SUBTILES + s
        n0 = off_ref[e * (n_tok_tiles + 1) + i]
        n1 = off_ref[e * (n_tok_tiles + 1) + i + 1]
        j0 = n0 // COMPACT_TILE
        n_tiles = jnp.where(n1 > n0, (n1 - 1) // COMPACT_TILE - j0 + 1, 0)
        pos_row = pos_ref[pl.ds(e, 1), s * MOE_TOK:(s + 1) * MOE_TOK]
        tok = i * MOE_TOK + lax.broadcasted_iota(I32, (MOE_TOK, LANES), 0)
        col = lax.broadcasted_iota(I32, (MOE_TOK, LANES), 1)
        hi = jnp.right_shift(tok, IDX_BASE.bit_length() - 1)
        lo = jnp.bitwise_and(tok, IDX_BASE - 1)
        digits_ref[...] = jnp.where(col == 0, hi, jnp.where(col == 1, lo, 0)).astype(F32).astype(BF16)

        def tile(j, live, pos_row=pos_row):
            sub = lax.broadcasted_iota(I32, (COMPACT_TILE, MOE_TOK), 0)
            onehot = jnp.where((pos_row - j * COMPACT_TILE == sub) & live, 1.0, 0.0).astype(BF16)
            rows = pl.ds(pl.multiple_of(j * COMPACT_TILE, COMPACT_TILE), COMPACT_TILE)
            idx_ref[0, rows, :] += jnp.dot(onehot, digits_ref[...], preferred_element_type=F32)

        last = idx_ref.shape[1] // COMPACT_TILE - 1
        for jj in range(COMPACT_STATIC):
            tile(jnp.minimum(j0 + jj, last), jj < n_tiles)

        def rest(jj, carry, j0=j0, tile=tile):
            tile(j0 + jj, True)
            return carry

        lax.fori_loop(COMPACT_STATIC, n_tiles, rest, 0)


def _slot_tokens(pos, off_flat, cap):
    T = pos.shape[1]
    nt = T // MOE_TOK
    assert T // IDX_BASE <= IDX_BASE and nt % SLOT_SUBTILES == 0
    grid_spec = pltpu.PrefetchScalarGridSpec(
        num_scalar_prefetch=1,
        grid=(N_EXPERTS, nt // SLOT_SUBTILES),
        in_specs=[pl.BlockSpec((N_EXPERTS, SLOT_SUBTILES * MOE_TOK), lambda e, g, off: (0, g))],
        out_specs=pl.BlockSpec((1, cap, LANES), lambda e, i, off: (e, 0, 0)),
        scratch_shapes=[pltpu.VMEM((MOE_TOK, LANES), BF16)])
    idx = pl.pallas_call(
        functools.partial(_slots_kernel, n_tok_tiles=nt),
        grid_spec=grid_spec,
        out_shape=jax.ShapeDtypeStruct((N_EXPERTS, cap, LANES), F32),
        compiler_params=_params(("arbitrary", "arbitrary")),
        name="slots",
    )(off_flat, pos)
    return (idx[:, :, 0] * IDX_BASE + idx[:, :, 1]).astype(I32).reshape(-1)


MOE_SLOT = 512
FF_BLK = 256
N_FF = D_FF_EXPERT // FF_BLK
ROWS_PER_FF = 48
GATHER_ROWS = ROWS_PER_FF * N_FF
assert D_FF_EXPERT % FF_BLK == 0 and GATHER_ROWS >= MOE_SLOT and GATHER_ROWS % 8 == 0


def _moe_kernel(tok_ref, x_hbm, wg_ref, wu_ref, wd_ref, out_ref, xg_ref, xb_ref, acc_ref, sem, *, n_slot_tiles):
    step = pl.program_id(0) * n_slot_tiles + pl.program_id(1)
    total = N_EXPERTS * n_slot_tiles
    cur = step % 2

    def row_copy(tile_idx, r, buf):
        slot = jnp.minimum(r, MOE_SLOT - 1)
        tok = tok_ref[tile_idx * MOE_SLOT + slot]
        return pltpu.make_async_copy(x_hbm.at[pl.ds(tok, 1)], xg_ref.at[buf, pl.ds(r, 1)], sem.at[buf])

    def wait_tile(buf):
        pltpu.make_async_copy(x_hbm.at[pl.ds(0, GATHER_ROWS)], xg_ref.at[buf], sem.at[buf]).wait()

    @pl.when(step == 0)
    def _():
        def first(r, carry):
            row_copy(0, r, 0).start()
            return carry

        lax.fori_loop(0, GATHER_ROWS, first, 0)

    wait_tile(cur)
    nxt = (step + 1) % total
    xb_ref[...] = xg_ref[cur, :MOE_SLOT, :].astype(BF16)
    acc_ref[...] = jnp.zeros(acc_ref.shape, F32)

    def ff(c, carry):
        for r in range(ROWS_PER_FF):
            row_copy(nxt, c * ROWS_PER_FF + r, 1 - cur).start(priority=r % 2)
        cols = pl.ds(pl.multiple_of(c * FF_BLK, FF_BLK), FF_BLK)
        xb = xb_ref[...]
        g = jnp.dot(xb, wg_ref[0, :, cols], preferred_element_type=F32)
        u = jnp.dot(xb, wu_ref[0, :, cols], preferred_element_type=F32)
        h = ((g / (1.0 + jnp.exp(-g))) * u).astype(BF16)
        acc_ref[...] += jnp.dot(h, wd_ref[0, cols, :], preferred_element_type=F32)
        return carry

    lax.fori_loop(0, N_FF, ff, 0)
    out_ref[0] = acc_ref[...].astype(out_ref.dtype)

    @pl.when(step == total - 1)
    def _():
        wait_tile(1 - cur)


def _moe(x1, tok_flat, wg, wu, wd, cap):
    assert cap % MOE_SLOT == 0
    nj = cap // MOE_SLOT
    grid_spec = pltpu.PrefetchScalarGridSpec(
        num_scalar_prefetch=1,
        grid=(N_EXPERTS, nj),
        in_specs=[pl.BlockSpec(memory_space=pl.ANY),
                  pl.BlockSpec((1, D_MODEL, D_FF_EXPERT), lambda e, j, tok: (e, 0, 0)),
                  pl.BlockSpec((1, D_MODEL, D_FF_EXPERT), lambda e, j, tok: (e, 0, 0)),
                  pl.BlockSpec((1, D_FF_EXPERT, D_MODEL), lambda e, j, tok: (e, 0, 0))],
        out_specs=pl.BlockSpec((1, MOE_SLOT, D_MODEL), lambda e, j, tok: (e, j, 0)),
        scratch_shapes=[pltpu.VMEM((2, GATHER_ROWS, D_MODEL), F32),
                        pltpu.VMEM((MOE_SLOT, D_MODEL), BF16),
                        pltpu.VMEM((MOE_SLOT, D_MODEL), F32),
                        pltpu.SemaphoreType.DMA((2,))])
    return pl.pallas_call(
        functools.partial(_moe_kernel, n_slot_tiles=nj),
        grid_spec=grid_spec,
        out_shape=jax.ShapeDtypeStruct((N_EXPERTS, cap, D_MODEL), BF16),
        compiler_params=_params(("arbitrary", "arbitrary")),
        name="moe",
    )(tok_flat, x1, wg, wu, wd)


CMB_TOK = 256
ROW_CHUNK = 16
STAGE_ROWS = N_EXPERTS * (CMB_TOK + 2 * ROW_CHUNK)
NO_ROW = -(2 ** 30)


def _combine_kernel(off_ref, x1_ref, gid_ref, gate_ref, g_ref, b_ref, eo_ref, o_ref, stage_ref, w_ref, sem,
                    *, cap, n_tok_tiles):
    i = pl.program_id(0)
    cur = i % 2

    def chunk_copy(src_row, dst_row, buf):
        return pltpu.make_async_copy(eo_ref.at[pl.ds(src_row, ROW_CHUNK)],
                                     stage_ref.at[buf, pl.ds(dst_row, ROW_CHUNK)], sem.at[buf])

    def segments(tile):
        base = jnp.int32(0)
        segs = []
        for e in range(N_EXPERTS):
            n0 = off_ref[e * (n_tok_tiles + 1) + tile]
            n1 = off_ref[e * (n_tok_tiles + 1) + tile + 1]
            nch = (n1 + ROW_CHUNK - 1) // ROW_CHUNK - n0 // ROW_CHUNK
            segs.append((base, nch, e * cap + (n0 // ROW_CHUNK) * ROW_CHUNK))
            base = base + nch * ROW_CHUNK
        return segs, base

    def issue(tile, buf):
        segs, _ = segments(tile)
        for seg_base, nch, seg_row in segs:
            def one(c, carry, seg_base=seg_base, seg_row=seg_row):
                src = pl.multiple_of(seg_row + c * ROW_CHUNK, ROW_CHUNK)
                dst = pl.multiple_of(seg_base + c * ROW_CHUNK, ROW_CHUNK)
                chunk_copy(src, dst, buf).start()
                return carry

            lax.fori_loop(0, nch, one, 0)

    @pl.when(i == 0)
    def _():
        stage_ref[...] = jnp.zeros(stage_ref.shape, stage_ref.dtype)
        issue(0, 0)

    @pl.when(i + 1 < n_tok_tiles)
    def _():
        issue(i + 1, 1 - cur)

    segs, total = segments(i)

    def wait(c, carry):
        chunk_copy(0, 0, cur).wait()
        return carry

    lax.fori_loop(0, total // ROW_CHUNK, wait, 0)

    jsub = lax.broadcasted_iota(I32, (SLOT_TILE, CMB_TOK), 0)
    o_ref[...] = ALPHA * x1_ref[...]

    def kchunk(kc, carry):
        j0 = kc * SLOT_TILE
        w_ref[...] = jnp.zeros(w_ref.shape, F32)
        for e, (seg_base, nch, seg_row) in enumerate(segs):
            seg_end = seg_base + nch * ROW_CHUNK

            @pl.when((seg_base < j0 + SLOT_TILE) & (seg_end > j0))
            def _(e=e, seg_base=seg_base, seg_row=seg_row):
                row_id = jsub + (j0 + seg_row - seg_base)
                w_ref[...] = jnp.where(gid_ref[e:e + 1, :] == row_id, gate_ref[e:e + 1, :], w_ref[...])

        rows = stage_ref[cur, pl.ds(pl.multiple_of(j0, SLOT_TILE), SLOT_TILE), :]
        o_ref[...] += lax.dot_general(w_ref[...].astype(BF16), rows, (((0,), (0,)), ((), ())),
                                      preferred_element_type=F32)
        return carry

    lax.fori_loop(0, (total + SLOT_TILE - 1) // SLOT_TILE, kchunk, 0)
    o_ref[...] = _layer_norm(o_ref[...], g_ref[...], b_ref[...])


def _combine(x1, gid_tok, gate_tok, off_flat, eo_flat, g, b, cap):
    T = x1.shape[0]
    nt = T // CMB_TOK
    stage_rows = -(-STAGE_ROWS // SLOT_TILE) * SLOT_TILE
    row = lambda i, off: (i, 0)
    const = lambda i, off: (0, 0)
    grid_spec = pltpu.PrefetchScalarGridSpec(
        num_scalar_prefetch=1,
        grid=(nt,),
        in_specs=[pl.BlockSpec((CMB_TOK, D_MODEL), row),
                  pl.BlockSpec((N_EXPERTS, CMB_TOK), lambda i, off: (0, i)),
                  pl.BlockSpec((N_EXPERTS, CMB_TOK), lambda i, off: (0, i)),
                  pl.BlockSpec((1, D_MODEL), const),
                  pl.BlockSpec((1, D_MODEL), const),
                  pl.BlockSpec(memory_space=pl.ANY)],
        out_specs=pl.BlockSpec((CMB_TOK, D_MODEL), row),
        scratch_shapes=[pltpu.VMEM((2, stage_rows, D_MODEL), BF16),
                        pltpu.VMEM((SLOT_TILE, CMB_TOK), F32),
                        pltpu.SemaphoreType.DMA((2,))])
    return pl.pallas_call(
        functools.partial(_combine_kernel, cap=cap, n_tok_tiles=nt),
        grid_spec=grid_spec,
        out_shape=jax.ShapeDtypeStruct((T, D_MODEL), F32),
        compiler_params=_params(("arbitrary",)),
        name="combine",
    )(off_flat, x1, gid_tok, gate_tok, g.reshape(1, D_MODEL), b.reshape(1, D_MODEL), eo_flat)


def _tile_offsets(off256, cap, T, tok):
    step = tok // SEL_BLK
    o = off256[:, :T // SEL_BLK:step]
    o = jnp.concatenate([o, jnp.full((N_EXPERTS, 1), cap, I32)], axis=1)
    return o.reshape(-1)


def _trunk(x, w_in_p, conv_w, conv_b, dt_bias, a_log, d_skip, ssm_norm_w, w_out_b, ln1_g, ln1_b,
           wr_t, wg, wu, wd, ln2_g, ln2_b):
    B, seq, _ = x.shape
    T = B * seq
    cap = CAPACITY_FACTOR * T // N_EXPERTS
    x2 = x.reshape(T, D_MODEL)
    q, k, v, z, xbc, dt = _inproj(x2, w_in_p, seq)
    shp = (B, seq // RES, RES * ATTN_WIDTH)
    attn = _attention(q.reshape(shp), k.reshape(shp), v.reshape(shp)).reshape(T, ATTN_WIDTH)
    y = _ssd(xbc, dt, z, conv_w, conv_b, dt_bias, a_log, d_skip, ssm_norm_w, B, seq)
    x1, lg_t = _outproj(attn, y, x2, w_out_b, ln1_g, ln1_b, wr_t)
    aff_t, pos, off256 = _select(lg_t, cap)
    tok_flat = _slot_tokens(pos, _tile_offsets(off256, cap, T, MOE_TOK), cap)
    eo = _moe(x1, tok_flat, wg, wu, wd, cap)
    gid = jnp.where(pos >= 0, pos + jnp.arange(N_EXPERTS, dtype=I32)[:, None] * cap, NO_ROW)
    out = _combine(x1, gid, aff_t, _tile_offsets(off256, cap, T, CMB_TOK),
                   eo.reshape(N_EXPERTS * cap, D_MODEL), ln2_g, ln2_b, cap)
    return out.reshape(B, seq, D_MODEL)


def kernel(x_prompt, x_sample, w_in, conv_w, conv_b, dt_bias, a_log, d_skip, ssm_norm_w, w_out, ln1_g, ln1_b,
           w_router, w_gate, w_up, w_down, ln2_g, ln2_b):
    assert DEPTH == 1
    l = 0
    w_in_p = jnp.pad(w_in[l], ((0, 0), (0, IN_PAD - w_in.shape[-1]))).astype(BF16)
    args = (w_in_p, conv_w[l], conv_b[l], dt_bias[l], a_log[l], d_skip[l], ssm_norm_w[l],
            w_out[l].astype(BF16), ln1_g[l], ln1_b[l], w_router[l].T,
            w_gate[l].astype(BF16), w_up[l].astype(BF16), w_down[l].astype(BF16), ln2_g[l], ln2_b[l])
    return (_trunk(x_prompt, *args), _trunk(x_sample, *args))
```

```python
import functools

import jax
import jax.numpy as jnp
from jax import lax
from jax.experimental import pallas as pl
from jax.experimental.pallas import tpu as pltpu

F32 = jnp.float32
BF16 = jnp.bfloat16
I32 = jnp.int32
HIGHEST = lax.Precision.HIGHEST

D_MODEL = 1024
DEPTH = 1
HEAD_DIM = 64
N_ATTN_HEADS = 8
ATTN_WIDTH = N_ATTN_HEADS * HEAD_DIM
ROT_DIM = HEAD_DIM // 4
ROPE_THETA = 500000.0
DILATED_PATTERNS = ((128, 1), (512, 4), (2048, 16))
N_SSM_HEADS = 8
SSM_HEAD_DIM = 64
SSM_WIDTH = N_SSM_HEADS * SSM_HEAD_DIM
SSM_GROUPS = 2
D_STATE = 128
CONV_K = 5
CHUNK = 128
CONV_CH = SSM_WIDTH + 2 * SSM_GROUPS * D_STATE
MIX_WIDTH = ATTN_WIDTH + SSM_WIDTH
N_EXPERTS = 16
CAPACITY_FACTOR = 2
D_FF_EXPERT = 2816
ALPHA = (2.0 * DEPTH) ** 0.25
LN_EPS = 1e-5
RMS_EPS = 1e-5

LANES = 128
VMEM_LIMIT = 56 * 1024 * 1024
NEG = -1e30
HALF_WIN = 64
RES = 4

IN_PAD = 3 * ATTN_WIDTH + SSM_WIDTH + CONV_CH + LANES


def _params(sem):
    return pltpu.CompilerParams(dimension_semantics=sem, vmem_limit_bytes=VMEM_LIMIT)


def _inproj_kernel(x_ref, w_ref, c_ref, sa_ref, sb_ref, q_ref, k_ref, v_ref, z_ref, xbc_ref, dt_ref, tmp_ref):
    x = x_ref[...].astype(BF16)
    tm = x_ref.shape[0]

    def put_grouped(out_ref, val):
        for g in range(ATTN_WIDTH // LANES):
            tmp_ref[g] = val[:, g * LANES:(g + 1) * LANES]
        for r in range(RES):
            for g in range(ATTN_WIDTH // LANES):
                lo = r * ATTN_WIDTH + g * LANES
                out_ref[:, lo:lo + LANES] = tmp_ref[g, pl.ds(r, tm // RES, stride=RES), :]

    def seg(lo, hi):
        return jnp.dot(x, w_ref[:, lo:hi], preferred_element_type=F32)

    c = c_ref[...]
    sa = sa_ref[...]
    sb = sb_ref[...]

    def rotary(t, scale):
        outs = []
        for g in range(ATTN_WIDTH // LANES):
            tg = t[:, g * LANES:(g + 1) * LANES]
            up = pltpu.roll(tg, LANES - ROT_DIM // 2, 1)
            dn = pltpu.roll(tg, ROT_DIM // 2, 1)
            outs.append((tg * c + up * sa + dn * sb) * scale)
        return jnp.concatenate(outs, axis=1)

    a = ATTN_WIDTH
    put_grouped(q_ref, rotary(seg(0, a), HEAD_DIM ** -0.5))
    put_grouped(k_ref, rotary(seg(a, 2 * a), 1.0))
    put_grouped(v_ref, seg(2 * a, 3 * a))
    z_ref[...] = seg(3 * a, 3 * a + SSM_WIDTH)
    o = 3 * a + SSM_WIDTH
    xbc_ref[...] = seg(o, o + CONV_CH)
    dt_ref[...] = seg(o + CONV_CH, o + CONV_CH + LANES)


def _rotary_tables(seq):
    half = ROT_DIM // 2
    inv = ROPE_THETA ** (-jnp.arange(half, dtype=F32) * 2.0 / ROT_DIM)
    ang = jnp.arange(seq).astype(F32)[:, None] * inv[None, :]
    cos = jnp.cos(ang)
    sin = jnp.sin(ang)
    m = jnp.arange(LANES) % HEAD_DIM
    c = jnp.where(m[None, :] < ROT_DIM, cos[:, m % half], 1.0)
    sa = jnp.where(m[None, :] < half, -sin[:, m % half], 0.0)
    sb = jnp.where((m[None, :] >= half) & (m[None, :] < ROT_DIM), sin[:, m % half], 0.0)
    return c.astype(F32), sa.astype(F32), sb.astype(F32)


def _inproj(x2, w_pad, seq, tm=512):
    T = x2.shape[0]
    c, sa, sb = _rotary_tables(seq)
    nseq = seq // tm
    row = lambda i: (i, 0)
    tab = lambda i: (i % nseq, 0)
    outs = [jax.ShapeDtypeStruct((T // RES, RES * ATTN_WIDTH), F32)] * 3 + [
        jax.ShapeDtypeStruct((T, SSM_WIDTH), F32),
        jax.ShapeDtypeStruct((T, CONV_CH), F32),
        jax.ShapeDtypeStruct((T, LANES), F32)]
    return pl.pallas_call(
        _inproj_kernel,
        grid=(T // tm,),
        in_specs=[pl.BlockSpec((tm, D_MODEL), row),
                  pl.BlockSpec((D_MODEL, IN_PAD), lambda i: (0, 0)),
                  pl.BlockSpec((tm, LANES), tab),
                  pl.BlockSpec((tm, LANES), tab),
                  pl.BlockSpec((tm, LANES), tab)],
        out_specs=[pl.BlockSpec((tm // RES, RES * ATTN_WIDTH), row)] * 3 + [
            pl.BlockSpec((tm, SSM_WIDTH), row),
            pl.BlockSpec((tm, CONV_CH), row),
            pl.BlockSpec((tm, LANES), row)],
        out_shape=outs,
        scratch_shapes=[pltpu.VMEM((ATTN_WIDTH // LANES, tm, LANES), F32)],
        compiler_params=_params(("parallel",)),
        name="inproj",
    )(x2, w_pad, c, sa, sb)


QB = 128
KB = 256
ATTN_GROUP = 8


def _attn_kernel(q_hbm, k_hbm, v_hbm, o_ref, qa, ka, va, m_ref, l_ref, acc_ref, bias_ref, sem, *, seq):
    b = pl.program_id(0)
    h = pl.program_id(1)
    sub = seq // RES
    npair = ATTN_WIDTH // LANES

    copies = []
    for src, dst in ((q_hbm, qa), (k_hbm, ka), (v_hbm, va)):
        for r in range(RES):
            lanes = pl.ds(pl.multiple_of((r * npair + h) * LANES, LANES), LANES)
            copies.append(pltpu.make_async_copy(src.at[b, :, lanes], dst.at[pl.ds(r * sub, sub), :], sem))
    for c in copies:
        c.start()

    row = lax.broadcasted_iota(I32, (2 * QB, KB), 0)
    col = lax.broadcasted_iota(I32, (2 * QB, KB), 1)
    iq = jnp.where(row >= QB, row - QB, row)
    iq_runs = RES * (iq % (QB // RES)) + iq // (QB // RES)
    ik_runs = RES * (col % (KB // RES)) + col // (KB // RES)
    for di in range(3):
        bias_ref[0, di] = jnp.where(jnp.abs(di * HALF_WIN + iq - col) <= HALF_WIN, 0.0, NEG).astype(F32)
        bias_ref[1, di] = jnp.where(jnp.abs(di * HALF_WIN + iq_runs - ik_runs) <= HALF_WIN, 0.0, NEG).astype(F32)
    for c in copies:
        c.wait()

    lane = lax.broadcasted_iota(I32, (QB, LANES), 1)
    head0 = lane < HEAD_DIM
    ones = jnp.ones((KB, LANES), BF16)

    def load(ref, runs, *lead):
        parts = [ref[lead + (rn, slice(None))] for rn in runs]
        return parts[0] if len(parts) == 1 else jnp.concatenate(parts, axis=0)

    def store(ref, runs, val, *lead):
        at = 0
        for rn, size in runs:
            ref[lead + (rn, slice(None))] = val[at:at + size]
            at += size

    for window, dil in DILATED_PATTERNS:
        assert window // (2 * dil) == HALF_WIN and (dil == 1 or dil % RES == 0)
        sub_len = seq // dil
        nblk = sub_len // QB
        assert sub_len >= KB and sub_len % QB == 0 and (dil * nblk) % ATTN_GROUP == 0

        def block_rows(idx, dil=dil, sub_len=sub_len, nblk=nblk):
            r = idx // nblk
            n = idx - r * nblk
            kb = jnp.clip(n * QB - HALF_WIN, 0, sub_len - KB)
            di = (n * QB - kb) // HALF_WIN
            if dil == 1:
                qn, kn = QB // RES, KB // RES
                q_runs = [(pl.ds(pl.multiple_of(c * sub + n * qn, qn), qn), qn) for c in range(RES)]
                k_runs = [pl.ds(pl.multiple_of(c * sub + kb // RES, 8), kn) for c in range(RES)]
                return q_runs, k_runs, bias_ref[1, di]
            step = dil // RES
            base = (r % RES) * sub + r // RES
            if step == 1:
                q_runs = [(pl.ds(pl.multiple_of(base + n * QB, QB), QB), QB)]
                k_runs = [pl.ds(pl.multiple_of(base + kb, HALF_WIN), KB)]
            else:
                q_runs = [(pl.ds(base + step * QB * n, QB, stride=step), QB)]
                k_runs = [pl.ds(base + step * kb, KB, stride=step)]
            return q_runs, k_runs, bias_ref[0, di]

        first = (window, dil) == DILATED_PATTERNS[0]
        assert not first or dil == 1

        def body(gi, carry, block_rows=block_rows, first=first):
            loaded = []
            for u in range(ATTN_GROUP):
                q_runs, k_runs, bias = block_rows(gi * ATTN_GROUP + u)
                qr = [rn for rn, _ in q_runs]
                state = () if first else (load(m_ref, qr, 0), load(m_ref, qr, 1), load(l_ref, qr, 0),
                                          load(l_ref, qr, 1), load(acc_ref, qr))
                loaded.append((q_runs, load(qa, qr), load(ka, k_runs), load(va, k_runs), bias, state))
            results = []
            for q_runs, q, k, v, bias, state in loaded:
                qs = jnp.concatenate([jnp.where(head0, q, 0.0), jnp.where(head0, 0.0, q)], axis=0).astype(BF16)
                s = lax.dot_general(qs, k.astype(BF16), (((1,), (1,)), ((), ())), preferred_element_type=F32)
                s = s + bias
                m_cur = jnp.max(s, axis=1, keepdims=True)
                if first:
                    m_new = jnp.broadcast_to(m_cur, (2 * QB, LANES))
                else:
                    m0, m1, l0, l1, acc = state
                    m_prev = jnp.concatenate([m0, m1], axis=0)
                    m_new = jnp.maximum(m_prev, m_cur)
                p = jnp.exp(s - jnp.concatenate([m_new, m_new], axis=1))
                pv = jnp.dot(p.astype(BF16), jnp.concatenate([v.astype(BF16), ones], axis=1),
                             preferred_element_type=F32)
                pv_l = jnp.where(head0, pv[:QB, :LANES], pv[QB:, :LANES])
                if first:
                    results.append((q_runs, pv_l, m_new, pv[:, LANES:]))
                else:
                    alpha = jnp.exp(m_prev - m_new)
                    l_new = alpha * jnp.concatenate([l0, l1], axis=0) + pv[:, LANES:]
                    a_l = jnp.where(head0, alpha[:QB], alpha[QB:])
                    results.append((q_runs, a_l * acc + pv_l, m_new, l_new))
            for q_runs, acc_new, m_new, l_new in results:
                store(acc_ref, q_runs, acc_new)
                store(m_ref, q_runs, m_new[:QB], 0)
                store(m_ref, q_runs, m_new[QB:], 1)
                store(l_ref, q_runs, l_new[:QB], 0)
                store(l_ref, q_runs, l_new[QB:], 1)
            return carry

        lax.fori_loop(0, dil * nblk // ATTN_GROUP, body, 0)

    for r in range(RES):
        def fin(i, carry, r=r):
            rws = pl.ds(pl.multiple_of(r * sub + i * QB, QB), QB)
            den = jnp.where(head0, l_ref[0, rws, :], l_ref[1, rws, :])
            o_ref[0, pl.ds(r + RES * QB * i, QB, stride=RES), :] = acc_ref[rws, :] / den
            return carry

        lax.fori_loop(0, sub // QB, fin, 0)


def _attention(q, k, v):
    B, sub, _ = q.shape
    seq = sub * RES
    npair = ATTN_WIDTH // LANES
    any_spec = pl.BlockSpec(memory_space=pl.ANY)
    return pl.pallas_call(
        functools.partial(_attn_kernel, seq=seq),
        grid=(B, npair),
        in_specs=[any_spec, any_spec, any_spec],
        out_specs=pl.BlockSpec((1, seq, LANES), lambda b, h: (b, 0, h)),
        out_shape=jax.ShapeDtypeStruct((B, seq, ATTN_WIDTH), F32),
        scratch_shapes=[pltpu.VMEM((seq, LANES), F32),
                        pltpu.VMEM((seq, LANES), F32),
                        pltpu.VMEM((seq, LANES), F32),
                        pltpu.VMEM((2, seq, LANES), F32),
                        pltpu.VMEM((2, seq, LANES), F32),
                        pltpu.VMEM((seq, LANES), F32),
                        pltpu.VMEM((2, 3, 2 * QB, KB), F32),
                        pltpu.SemaphoreType.DMA(())],
        compiler_params=_params(("parallel", "parallel")),
        name="attn",
    )(q, k, v)


SSD_TILE = 512
HALO = 8


def _ssd_kernel(*refs, reverse):
    if reverse:
        u_ref, dt_ref, dtb_ref, alog_ref, yf_ref, z_ref, dsk_ref, nw_ref, y_ref, st_ref = refs
    else:
        (xbc_ref, hp_ref, hn_ref, dt_ref, cw_ref, cb_ref, dtb_ref, alog_ref,
         y_ref, u_ref, ext_ref, st_ref) = refs
    lane_off = N_SSM_HEADS if reverse else 0
    nchunk = SSD_TILE // CHUNK

    @pl.when(pl.program_id(1) == 0)
    def _():
        st_ref[...] = jnp.zeros(st_ref.shape, F32)

    if not reverse:
        ext_ref[0:HALO, :] = hp_ref[0]
        ext_ref[HALO:HALO + SSD_TILE, :] = xbc_ref[...]
        ext_ref[HALO + SSD_TILE:HALO + SSD_TILE + HALO, :] = hn_ref[0]

    li = lax.broadcasted_iota(I32, (CHUNK, CHUNK), 0)
    si = lax.broadcasted_iota(I32, (CHUNK, CHUNK), 1)
    tri = (si >= li) if reverse else (si <= li)
    tri_f = tri.astype(F32)
    ej = lax.broadcasted_iota(I32, (LANES, SSM_WIDTH), 0)
    ec = lax.broadcasted_iota(I32, (LANES, SSM_WIDTH), 1)
    expand = (ej == lane_off + ec // SSM_HEAD_DIM).astype(BF16)
    lane = lax.broadcasted_iota(I32, (CHUNK, LANES), 1)
    lo_half = lane < SSM_HEAD_DIM
    a_row = -jnp.exp(alog_ref[...])
    gw = SSM_WIDTH // SSM_GROUPS
    heads_per_group = N_SSM_HEADS // SSM_GROUPS

    order = range(nchunk - 1, -1, -1) if reverse else range(nchunk)
    for c in order:
        crow = pl.ds(c * CHUNK, CHUNK)
        if reverse:
            u = u_ref[crow, :]
        else:
            base = HALO + c * CHUNK
            conv = None
            for kk in range(CONV_K):
                tap = ext_ref[pl.ds(base + kk - CONV_K // 2, CHUNK), :] * cw_ref[kk:kk + 1, :]
                conv = tap if conv is None else conv + tap
            u = conv + cb_ref[...]
            u = u / (1.0 + jnp.exp(-u))
            u_ref[crow, :] = u
        xs = u[:, :SSM_WIDTH]
        bm = u[:, SSM_WIDTH:SSM_WIDTH + SSM_GROUPS * D_STATE]
        cm = u[:, SSM_WIDTH + SSM_GROUPS * D_STATE:]

        dpre = dt_ref[pl.ds(c * CHUNK, CHUNK), :] + dtb_ref[...]
        dtv = jnp.maximum(dpre, 0.0) + jnp.log(1.0 + jnp.exp(-jnp.abs(dpre)))
        a = dtv * a_row
        acs = jnp.dot(tri_f, a, precision=HIGHEST, preferred_element_type=F32)
        acs_t = acs.T
        tot = acs[0:1, :] if reverse else acs[CHUNK - 1:CHUNK, :]
        stacked = jnp.concatenate([dtv, jnp.exp(acs), jnp.exp(tot - acs)], axis=0)
        s_hi = stacked.astype(BF16)
        s_lo = (stacked - s_hi.astype(F32)).astype(BF16)
        ex = (jnp.dot(s_hi, expand, preferred_element_type=F32)
              + jnp.dot(s_lo, expand, preferred_element_type=F32))
        dt_x = ex[:CHUNK]
        eacs_x = ex[CHUNK:2 * CHUNK]
        dend_x = ex[2 * CHUNK:]
        cdec_x = eacs_x[0:1, :] if reverse else eacs_x[CHUNK - 1:CHUNK, :]
        xdt = xs * dt_x
        wst = (dend_x * xdt).astype(BF16)
        st_prev = st_ref[...]
        st_b = st_prev.astype(BF16)

        ygs = []
        st_new = []
        for g in range(SSM_GROUPS):
            ys = []
            bg = bm[:, g * D_STATE:(g + 1) * D_STATE]
            cg = cm[:, g * D_STATE:(g + 1) * D_STATE].astype(BF16)
            cb = lax.dot_general(cg, bg.astype(BF16), (((1,), (1,)), ((), ())), preferred_element_type=F32)
            ms = []
            for hh in range(heads_per_group):
                h = lane_off + g * heads_per_group + hh
                diff = acs[:, h:h + 1] - acs_t[h:h + 1, :]
                lm = jnp.exp(jnp.where(tri, diff, NEG))
                ms.append((cb * lm).astype(BF16))
            for pp in range(heads_per_group // 2):
                hp = g * (heads_per_group // 2) + pp
                lhs = jnp.concatenate([ms[2 * pp], ms[2 * pp + 1]], axis=1)
                xp = xdt[:, hp * LANES:(hp + 1) * LANES]
                rhs = jnp.concatenate([jnp.where(lo_half, xp, 0.0), jnp.where(lo_half, 0.0, xp)],
                                      axis=0).astype(BF16)
                ys.append(jnp.dot(lhs, rhs, preferred_element_type=F32))
            sc = jnp.dot(bg.T.astype(BF16), wst[:, g * gw:(g + 1) * gw], preferred_element_type=F32)
            yoff = jnp.dot(cg, st_b[:, g * gw:(g + 1) * gw], preferred_element_type=F32)
            ygs.append(jnp.concatenate(ys, axis=1) + yoff * eacs_x[:, g * gw:(g + 1) * gw])
            st_new.append(st_prev[:, g * gw:(g + 1) * gw] * cdec_x[:, g * gw:(g + 1) * gw] + sc)
        st_ref[...] = jnp.concatenate(st_new, axis=1)
        y = jnp.concatenate(ygs, axis=1)

        crow = pl.ds(c * CHUNK, CHUNK)
        if reverse:
            ytot = yf_ref[crow, :] + y + dsk_ref[...] * xs
            zz = z_ref[crow, :]
            yz = ytot * (zz / (1.0 + jnp.exp(-zz)))
            yn = yz * lax.rsqrt(jnp.mean(jnp.square(yz), axis=-1, keepdims=True) + RMS_EPS) * nw_ref[...]
            y_ref[crow, :] = yn.astype(y_ref.dtype)
        else:
            y_ref[crow, :] = y


def _ssd(xbc, dt, z, conv_w, conv_b, dt_bias, a_log, d_skip, ssm_norm_w, B, seq):
    T = B * seq
    nt = seq // SSD_TILE
    x4 = xbc.reshape(B, nt, SSD_TILE, CONV_CH)
    zeros = jnp.zeros((B, 1, HALO, CONV_CH), F32)
    hprev = jnp.concatenate([zeros, x4[:, :-1, SSD_TILE - HALO:, :]], axis=1).reshape(B * nt, HALO, CONV_CH)
    hnext = jnp.concatenate([x4[:, 1:, :HALO, :], zeros], axis=1).reshape(B * nt, HALO, CONV_CH)
    cw = jnp.zeros((HALO, CONV_CH), F32).at[:CONV_K].set(conv_w)
    cb = conv_b.reshape(1, CONV_CH)
    pad = LANES - 2 * N_SSM_HEADS
    dtb = jnp.pad(dt_bias.reshape(1, 2 * N_SSM_HEADS), ((0, 0), (0, pad)))
    alog = jnp.pad(a_log.reshape(1, 2 * N_SSM_HEADS), ((0, 0), (0, pad)), constant_values=-1e30)
    dsk = jnp.repeat(d_skip, SSM_HEAD_DIM).reshape(1, SSM_WIDTH)
    nw = ssm_norm_w.reshape(1, SSM_WIDTH)

    const = lambda b, i: (0, 0)
    state = pltpu.VMEM((D_STATE, SSM_WIDTH), F32)

    fmap = lambda b, i: (b * nt + i, 0)
    hmap = lambda b, i: (b * nt + i, 0, 0)
    yf, u = pl.pallas_call(
        functools.partial(_ssd_kernel, reverse=False),
        grid=(B, nt),
        in_specs=[pl.BlockSpec((SSD_TILE, CONV_CH), fmap),
                  pl.BlockSpec((1, HALO, CONV_CH), hmap),
                  pl.BlockSpec((1, HALO, CONV_CH), hmap),
                  pl.BlockSpec((SSD_TILE, LANES), fmap),
                  pl.BlockSpec((HALO, CONV_CH), const),
                  pl.BlockSpec((1, CONV_CH), const),
                  pl.BlockSpec((1, LANES), const),
                  pl.BlockSpec((1, LANES), const)],
        out_specs=[pl.BlockSpec((SSD_TILE, SSM_WIDTH), fmap), pl.BlockSpec((SSD_TILE, CONV_CH), fmap)],
        out_shape=[jax.ShapeDtypeStruct((T, SSM_WIDTH), F32), jax.ShapeDtypeStruct((T, CONV_CH), F32)],
        scratch_shapes=[pltpu.VMEM((SSD_TILE + 2 * HALO, CONV_CH), F32), state],
        compiler_params=_params(("parallel", "arbitrary")),
        name="ssd_fwd",
    )(xbc, hprev, hnext, dt, cw, cb, dtb, alog)

    rmap = lambda b, i: (b * nt + nt - 1 - i, 0)
    return pl.pallas_call(
        functools.partial(_ssd_kernel, reverse=True),
        grid=(B, nt),
        in_specs=[pl.BlockSpec((SSD_TILE, CONV_CH), rmap),
                  pl.BlockSpec((SSD_TILE, LANES), rmap),
                  pl.BlockSpec((1, LANES), const),
                  pl.BlockSpec((1, LANES), const),
                  pl.BlockSpec((SSD_TILE, SSM_WIDTH), rmap),
                  pl.BlockSpec((SSD_TILE, SSM_WIDTH), rmap),
                  pl.BlockSpec((1, SSM_WIDTH), const),
                  pl.BlockSpec((1, SSM_WIDTH), const)],
        out_specs=pl.BlockSpec((SSD_TILE, SSM_WIDTH), rmap),
        out_shape=jax.ShapeDtypeStruct((T, SSM_WIDTH), BF16),
        scratch_shapes=[state],
        compiler_params=_params(("parallel", "arbitrary")),
        name="ssd_bwd",
    )(u, dt, dtb, alog, yf, z, dsk, nw)


def _layer_norm(h, g, b):
    mu = jnp.mean(h, axis=-1, keepdims=True)
    var = jnp.mean(jnp.square(h - mu), axis=-1, keepdims=True)
    return (h - mu) * lax.rsqrt(var + LN_EPS) * g + b


def _outproj_kernel(a_ref, y_ref, x_ref, w_ref, g_ref, b_ref, wr_ref, x1_ref, lg_ref):
    mix = jnp.dot(a_ref[...].astype(BF16), w_ref[:ATTN_WIDTH, :], preferred_element_type=F32)
    mix = mix + jnp.dot(y_ref[...], w_ref[ATTN_WIDTH:, :], preferred_element_type=F32)
    x1 = _layer_norm(ALPHA * x_ref[...] + mix, g_ref[...], b_ref[...])
    x1_ref[...] = x1
    x_hi = x1.astype(BF16)
    x_lo = (x1 - x_hi.astype(F32)).astype(BF16)
    w_hi = wr_ref[...].astype(BF16)
    w_lo = (wr_ref[...] - w_hi.astype(F32)).astype(BF16)
    nt_dims = (((1,), (1,)), ((), ()))
    lg_ref[...] = (lax.dot_general(w_hi, x_hi, nt_dims, preferred_element_type=F32)
                   + lax.dot_general(w_lo, x_hi, nt_dims, preferred_element_type=F32)
                   + lax.dot_general(w_hi, x_lo, nt_dims, preferred_element_type=F32))


def _outproj(attn, y, x2, w_out_b, g, b, wr_t, tm=512):
    T = x2.shape[0]
    row = lambda i: (i, 0)
    const = lambda i: (0, 0)
    return pl.pallas_call(
        _outproj_kernel,
        grid=(T // tm,),
        in_specs=[pl.BlockSpec((tm, ATTN_WIDTH), row),
                  pl.BlockSpec((tm, SSM_WIDTH), row),
                  pl.BlockSpec((tm, D_MODEL), row),
                  pl.BlockSpec((MIX_WIDTH, D_MODEL), const),
                  pl.BlockSpec((1, D_MODEL), const),
                  pl.BlockSpec((1, D_MODEL), const),
                  pl.BlockSpec((N_EXPERTS, D_MODEL), const)],
        out_specs=[pl.BlockSpec((tm, D_MODEL), row),
                   pl.BlockSpec((N_EXPERTS, tm), lambda i: (0, i))],
        out_shape=[jax.ShapeDtypeStruct((T, D_MODEL), F32),
                   jax.ShapeDtypeStruct((N_EXPERTS, T), F32)],
        compiler_params=_params(("parallel",)),
        name="outproj",
    )(attn, y, x2, w_out_b, g.reshape(1, D_MODEL), b.reshape(1, D_MODEL), wr_t)


SEL_BLK = 256


MIN_NORMAL_BITS = 0x00800000


def _select_kernel(lg_ref, aff_ref, pos_ref, off_ref, res_ref, *, cap, T):
    lg = lg_ref[...]
    ex = jnp.exp(lg - jnp.max(lg, axis=0, keepdims=True))
    aff_ref[...] = ex / jnp.sum(ex, axis=0, keepdims=True)

    def as_f32(bits):
        return lax.bitcast_convert_type(bits, F32)

    def kth_largest(ref):
        def search(i, t):
            cand = t | jnp.left_shift(jnp.int32(1), 30 - i)
            cnt = jnp.sum((ref[...] >= as_f32(cand)).astype(I32), axis=1, keepdims=True)
            return jnp.where((cnt >= cap) & (cand >= MIN_NORMAL_BITS), cand, t)

        return lax.fori_loop(0, 31, search, jnp.zeros((N_EXPERTS, 1), I32))

    thr1 = as_f32(kth_largest(aff_ref))
    res_ref[...] = aff_ref[...] - thr1
    thr2_bits = kth_largest(res_ref)
    thr2 = as_f32(thr2_bits)
    nxt2 = as_f32(jnp.where(thr2_bits == 0, MIN_NORMAL_BITS, thr2_bits + 1))
    n_gt = jnp.sum((res_ref[...] >= nxt2).astype(I32), axis=1, keepdims=True)
    need = (cap - n_gt).astype(F32)

    uj = lax.broadcasted_iota(I32, (SEL_BLK, SEL_BLK), 0)
    ut = lax.broadcasted_iota(I32, (SEL_BLK, SEL_BLK), 1)
    upper = (uj <= ut).astype(BF16)
    nblk = T // SEL_BLK
    olane = lax.broadcasted_iota(I32, off_ref.shape, 1)

    def blk(i, carry):
        c_gt, c_eq, offs = carry
        cols = pl.ds(pl.multiple_of(i * SEL_BLK, SEL_BLK), SEL_BLK)
        res = res_ref[:, cols]
        gt = res >= nxt2
        eq = (res >= thr2) & jnp.logical_not(gt)
        gt_f = gt.astype(F32)
        eq_f = eq.astype(F32)
        st = jnp.concatenate([gt_f, eq_f], axis=0).astype(BF16)
        cs = jnp.dot(st, upper, preferred_element_type=F32)
        gt_ex = c_gt + cs[:N_EXPERTS] - gt_f
        eq_ex = c_eq + cs[N_EXPERTS:] - eq_f
        sel = gt | (eq & (eq_ex < need))
        slot = gt_ex + jnp.minimum(eq_ex, need)
        pos_ref[:, cols] = jnp.where(sel, slot, -1.0).astype(I32)
        start = (c_gt + jnp.minimum(c_eq, need)).astype(I32)
        offs = jnp.where(olane == i, start, offs)
        return (c_gt + cs[:N_EXPERTS, SEL_BLK - 1:SEL_BLK], c_eq + cs[N_EXPERTS:, SEL_BLK - 1:SEL_BLK], offs)

    zero = jnp.zeros((N_EXPERTS, 1), F32)
    _, _, offs = lax.fori_loop(0, nblk, blk, (zero, zero, jnp.zeros(off_ref.shape, I32)))
    off_ref[...] = offs


def _select(lg_t, cap):
    T = lg_t.shape[1]
    nblk = T // SEL_BLK
    owidth = -(-nblk // LANES) * LANES
    return pl.pallas_call(
        functools.partial(_select_kernel, cap=cap, T=T),
        out_shape=[jax.ShapeDtypeStruct((N_EXPERTS, T), F32),
                   jax.ShapeDtypeStruct((N_EXPERTS, T), I32),
                   jax.ShapeDtypeStruct((N_EXPERTS, owidth), I32)],
        scratch_shapes=[pltpu.VMEM((N_EXPERTS, T), F32)],
        compiler_params=pltpu.CompilerParams(vmem_limit_bytes=VMEM_LIMIT),
        name="select",
    )(lg_t)


SLOT_TILE = 256
COMPACT_TILE = 128
COMPACT_STATIC = 2
MOE_TOK = 512
IDX_BASE = 256
SLOT_SUBTILES = 8


def _slots_kernel(off_ref, pos_ref, idx_ref, digits_ref, *, n_tok_tiles):
    e = pl.program_id(0)
    g = pl.program_id(1)

    @pl.when(g == 0)
    def _():
        idx_ref[...] = jnp.zeros(idx_ref.shape, F32)

    for s in range(SLOT_SUBTILES):
        i = g * SLOT_SUBTILES + s
        n0 = off_ref[e * (n_tok_tiles + 1) + i]
        n1 = off_ref[e * (n_tok_tiles + 1) + i + 1]
        j0 = n0 // COMPACT_TILE
        n_tiles = jnp.where(n1 > n0, (n1 - 1) // COMPACT_TILE - j0 + 1, 0)
        pos_row = pos_ref[pl.ds(e, 1), s * MOE_TOK:(s + 1) * MOE_TOK]
        tok = i * MOE_TOK + lax.broadcasted_iota(I32, (MOE_TOK, LANES), 0)
        col = lax.broadcasted_iota(I32, (MOE_TOK, LANES), 1)
        hi = jnp.right_shift(tok, IDX_BASE.bit_length() - 1)
        lo = jnp.bitwise_and(tok, IDX_BASE - 1)
        digits_ref[...] = jnp.where(col == 0, hi, jnp.where(col == 1, lo, 0)).astype(F32).astype(BF16)

        def tile(j, live, pos_row=pos_row):
            sub = lax.broadcasted_iota(I32, (COMPACT_TILE, MOE_TOK), 0)
            onehot = jnp.where((pos_row - j * COMPACT_TILE == sub) & live, 1.0, 0.0).astype(BF16)
            rows = pl.ds(pl.multiple_of(j * COMPACT_TILE, COMPACT_TILE), COMPACT_TILE)
            idx_ref[0, rows, :] += jnp.dot(onehot, digits_ref[...], preferred_element_type=F32)

        last = idx_ref.shape[1] // COMPACT_TILE - 1
        for jj in range(COMPACT_STATIC):
            tile(jnp.minimum(j0 + jj, last), jj < n_tiles)

        def rest(jj, carry, j0=j0, tile=tile):
            tile(j0 + jj, True)
            return carry

        lax.fori_loop(COMPACT_STATIC, n_tiles, rest, 0)


def _slot_tokens(pos, off_flat, cap):
    T = pos.shape[1]
    nt = T // MOE_TOK
    assert T // IDX_BASE <= IDX_BASE and nt % SLOT_SUBTILES == 0
    grid_spec = pltpu.PrefetchScalarGridSpec(
        num_scalar_prefetch=1,
        grid=(N_EXPERTS, nt // SLOT_SUBTILES),
        in_specs=[pl.BlockSpec((N_EXPERTS, SLOT_SUBTILES * MOE_TOK), lambda e, g, off: (0, g))],
        out_specs=pl.BlockSpec((1, cap, LANES), lambda e, i, off: (e, 0, 0)),
        scratch_shapes=[pltpu.VMEM((MOE_TOK, LANES), BF16)])
    idx = pl.pallas_call(
        functools.partial(_slots_kernel, n_tok_tiles=nt),
        grid_spec=grid_spec,
        out_shape=jax.ShapeDtypeStruct((N_EXPERTS, cap, LANES), F32),
        compiler_params=_params(("arbitrary", "arbitrary")),
        name="slots",
    )(off_flat, pos)
    return (idx[:, :, 0] * IDX_BASE + idx[:, :, 1]).astype(I32).reshape(-1)


MOE_SLOT = 512
FF_BLK = 256
N_FF = D_FF_EXPERT // FF_BLK
ROWS_PER_FF = 48
GATHER_ROWS = ROWS_PER_FF * N_FF
assert D_FF_EXPERT % FF_BLK == 0 and GATHER_ROWS >= MOE_SLOT and GATHER_ROWS % 8 == 0


def _moe_kernel(tok_ref, x_hbm, wg_ref, wu_ref, wd_ref, out_ref, xg_ref, xb_ref, acc_ref, sem, *, n_slot_tiles):
    step = pl.program_id(0) * n_slot_tiles + pl.program_id(1)
    total = N_EXPERTS * n_slot_tiles
    cur = step % 2

    def row_copy(tile_idx, r, buf):
        slot = jnp.minimum(r, MOE_SLOT - 1)
        tok = tok_ref[tile_idx * MOE_SLOT + slot]
        return pltpu.make_async_copy(x_hbm.at[pl.ds(tok, 1)], xg_ref.at[buf, pl.ds(r, 1)], sem.at[buf])

    def wait_tile(buf):
        pltpu.make_async_copy(x_hbm.at[pl.ds(0, GATHER_ROWS)], xg_ref.at[buf], sem.at[buf]).wait()

    @pl.when(step == 0)
    def _():
        def first(r, carry):
            row_copy(0, r, 0).start()
            return carry

        lax.fori_loop(0, GATHER_ROWS, first, 0)

    wait_tile(cur)
    nxt = (step + 1) % total
    xb_ref[...] = xg_ref[cur, :MOE_SLOT, :].astype(BF16)
    acc_ref[...] = jnp.zeros(acc_ref.shape, F32)

    def ff(c, carry):
        for r in range(ROWS_PER_FF):
            row_copy(nxt, c * ROWS_PER_FF + r, 1 - cur).start()
        cols = pl.ds(pl.multiple_of(c * FF_BLK, FF_BLK), FF_BLK)
        xb = xb_ref[...]
        g = jnp.dot(xb, wg_ref[0, :, cols], preferred_element_type=F32)
        u = jnp.dot(xb, wu_ref[0, :, cols], preferred_element_type=F32)
        h = ((g / (1.0 + jnp.exp(-g))) * u).astype(BF16)
        acc_ref[...] += jnp.dot(h, wd_ref[0, cols, :], preferred_element_type=F32)
        return carry

    lax.fori_loop(0, N_FF, ff, 0)
    out_ref[0] = acc_ref[...].astype(out_ref.dtype)

    @pl.when(step == total - 1)
    def _():
        wait_tile(1 - cur)


def _moe(x1, tok_flat, wg, wu, wd, cap):
    assert cap % MOE_SLOT == 0
    nj = cap // MOE_SLOT
    grid_spec = pltpu.PrefetchScalarGridSpec(
        num_scalar_prefetch=1,
        grid=(N_EXPERTS, nj),
        in_specs=[pl.BlockSpec(memory_space=pl.ANY),
                  pl.BlockSpec((1, D_MODEL, D_FF_EXPERT), lambda e, j, tok: (e, 0, 0)),
                  pl.BlockSpec((1, D_MODEL, D_FF_EXPERT), lambda e, j, tok: (e, 0, 0)),
                  pl.BlockSpec((1, D_FF_EXPERT, D_MODEL), lambda e, j, tok: (e, 0, 0))],
        out_specs=pl.BlockSpec((1, MOE_SLOT, D_MODEL), lambda e, j, tok: (e, j, 0)),
        scratch_shapes=[pltpu.VMEM((2, GATHER_ROWS, D_MODEL), F32),
                        pltpu.VMEM((MOE_SLOT, D_MODEL), BF16),
                        pltpu.VMEM((MOE_SLOT, D_MODEL), F32),
                        pltpu.SemaphoreType.DMA((2,))])
    return pl.pallas_call(
        functools.partial(_moe_kernel, n_slot_tiles=nj),
        grid_spec=grid_spec,
        out_shape=jax.ShapeDtypeStruct((N_EXPERTS, cap, D_MODEL), BF16),
        compiler_params=_params(("arbitrary", "arbitrary")),
        name="moe",
    )(tok_flat, x1, wg, wu, wd)


CMB_TOK = 256
ROW_CHUNK = 16
STAGE_ROWS = N_EXPERTS * (CMB_TOK + 2 * ROW_CHUNK)
NO_ROW = -(2 ** 30)


def _combine_kernel(off_ref, x1_ref, gid_ref, gate_ref, g_ref, b_ref, eo_ref, o_ref, stage_ref, w_ref, sem,
                    *, cap, n_tok_tiles):
    i = pl.program_id(0)
    cur = i % 2

    def chunk_copy(src_row, dst_row, buf):
        return pltpu.make_async_copy(eo_ref.at[pl.ds(src_row, ROW_CHUNK)],
                                     stage_ref.at[buf, pl.ds(dst_row, ROW_CHUNK)], sem.at[buf])

    def segments(tile):
        base = jnp.int32(0)
        segs = []
        for e in range(N_EXPERTS):
            n0 = off_ref[e * (n_tok_tiles + 1) + tile]
            n1 = off_ref[e * (n_tok_tiles + 1) + tile + 1]
            nch = (n1 + ROW_CHUNK - 1) // ROW_CHUNK - n0 // ROW_CHUNK
            segs.append((base, nch, e * cap + (n0 // ROW_CHUNK) * ROW_CHUNK))
            base = base + nch * ROW_CHUNK
        return segs, base

    def issue(tile, buf):
        segs, _ = segments(tile)
        for seg_base, nch, seg_row in segs:
            def one(c, carry, seg_base=seg_base, seg_row=seg_row):
                src = pl.multiple_of(seg_row + c * ROW_CHUNK, ROW_CHUNK)
                dst = pl.multiple_of(seg_base + c * ROW_CHUNK, ROW_CHUNK)
                chunk_copy(src, dst, buf).start()
                return carry

            lax.fori_loop(0, nch, one, 0)

    @pl.when(i == 0)
    def _():
        stage_ref[...] = jnp.zeros(stage_ref.shape, stage_ref.dtype)
        issue(0, 0)

    @pl.when(i + 1 < n_tok_tiles)
    def _():
        issue(i + 1, 1 - cur)

    segs, total = segments(i)

    def wait(c, carry):
        chunk_copy(0, 0, cur).wait()
        return carry

    lax.fori_loop(0, total // ROW_CHUNK, wait, 0)

    jsub = lax.broadcasted_iota(I32, (SLOT_TILE, CMB_TOK), 0)
    o_ref[...] = ALPHA * x1_ref[...]

    def kchunk(kc, carry):
        j0 = kc * SLOT_TILE
        w_ref[...] = jnp.zeros(w_ref.shape, F32)
        for e, (seg_base, nch, seg_row) in enumerate(segs):
            seg_end = seg_base + nch * ROW_CHUNK

            @pl.when((seg_base < j0 + SLOT_TILE) & (seg_end > j0))
            def _(e=e, seg_base=seg_base, seg_row=seg_row):
                row_id = jsub + (j0 + seg_row - seg_base)
                w_ref[...] = jnp.where(gid_ref[e:e + 1, :] == row_id, gate_ref[e:e + 1, :], w_ref[...])

        rows = stage_ref[cur, pl.ds(pl.multiple_of(j0, SLOT_TILE), SLOT_TILE), :]
        o_ref[...] += lax.dot_general(w_ref[...].astype(BF16), rows, (((0,), (0,)), ((), ())),
                                      preferred_element_type=F32)
        return carry

    lax.fori_loop(0, (total + SLOT_TILE - 1) // SLOT_TILE, kchunk, 0)
    o_ref[...] = _layer_norm(o_ref[...], g_ref[...], b_ref[...])


def _combine(x1, gid_tok, gate_tok, off_flat, eo_flat, g, b, cap):
    T = x1.shape[0]
    nt = T // CMB_TOK
    stage_rows = -(-STAGE_ROWS // SLOT_TILE) * SLOT_TILE
    row = lambda i, off: (i, 0)
    const = lambda i, off: (0, 0)
    grid_spec = pltpu.PrefetchScalarGridSpec(
        num_scalar_prefetch=1,
        grid=(nt,),
        in_specs=[pl.BlockSpec((CMB_TOK, D_MODEL), row),
                  pl.BlockSpec((N_EXPERTS, CMB_TOK), lambda i, off: (0, i)),
                  pl.BlockSpec((N_EXPERTS, CMB_TOK), lambda i, off: (0, i)),
                  pl.BlockSpec((1, D_MODEL), const),
                  pl.BlockSpec((1, D_MODEL), const),
                  pl.BlockSpec(memory_space=pl.ANY)],
        out_specs=pl.BlockSpec((CMB_TOK, D_MODEL), row),
        scratch_shapes=[pltpu.VMEM((2, stage_rows, D_MODEL), BF16),
                        pltpu.VMEM((SLOT_TILE, CMB_TOK), F32),
                        pltpu.SemaphoreType.DMA((2,))])
    return pl.pallas_call(
        functools.partial(_combine_kernel, cap=cap, n_tok_tiles=nt),
        grid_spec=grid_spec,
        out_shape=jax.ShapeDtypeStruct((T, D_MODEL), F32),
        compiler_params=_params(("arbitrary",)),
        name="combine",
    )(off_flat, x1, gid_tok, gate_tok, g.reshape(1, D_MODEL), b.reshape(1, D_MODEL), eo_flat)


def _tile_offsets(off256, cap, T, tok):
    step = tok // SEL_BLK
    o = off256[:, :T // SEL_BLK:step]
    o = jnp.concatenate([o, jnp.full((N_EXPERTS, 1), cap, I32)], axis=1)
    return o.reshape(-1)


def _trunk(x, w_in_p, conv_w, conv_b, dt_bias, a_log, d_skip, ssm_norm_w, w_out_b, ln1_g, ln1_b,
           wr_t, wg, wu, wd, ln2_g, ln2_b):
    B, seq, _ = x.shape
    T = B * seq
    cap = CAPACITY_FACTOR * T // N_EXPERTS
    x2 = x.reshape(T, D_MODEL)
    q, k, v, z, xbc, dt = _inproj(x2, w_in_p, seq)
    shp = (B, seq // RES, RES * ATTN_WIDTH)
    attn = _attention(q.reshape(shp), k.reshape(shp), v.reshape(shp)).reshape(T, ATTN_WIDTH)
    y = _ssd(xbc, dt, z, conv_w, conv_b, dt_bias, a_log, d_skip, ssm_norm_w, B, seq)
    x1, lg_t = _outproj(attn, y, x2, w_out_b, ln1_g, ln1_b, wr_t)
    aff_t, pos, off256 = _select(lg_t, cap)
    tok_flat = _slot_tokens(pos, _tile_offsets(off256, cap, T, MOE_TOK), cap)
    eo = _moe(x1, tok_flat, wg, wu, wd, cap)
    gid = jnp.where(pos >= 0, pos + jnp.arange(N_EXPERTS, dtype=I32)[:, None] * cap, NO_ROW)
    out = _combine(x1, gid, aff_t, _tile_offsets(off256, cap, T, CMB_TOK),
                   eo.reshape(N_EXPERTS * cap, D_MODEL), ln2_g, ln2_b, cap)
    return out.reshape(B, seq, D_MODEL)


def kernel(x_prompt, x_sample, w_in, conv_w, conv_b, dt_bias, a_log, d_skip, ssm_norm_w, w_out, ln1_g, ln1_b,
           w_router, w_gate, w_up, w_down, ln2_g, ln2_b):
    assert DEPTH == 1
    l = 0
    w_in_p = jnp.pad(w_in[l], ((0, 0), (0, IN_PAD - w_in.shape[-1]))).astype(BF16)
    args = (w_in_p, conv_w[l], conv_b[l], dt_bias[l], a_log[l], d_skip[l], ssm_norm_w[l],
            w_out[l].astype(BF16), ln1_g[l], ln1_b[l], w_router[l].T,
            w_gate[l].astype(BF16), w_up[l].astype(BF16), w_down[l].astype(BF16), ln2_g[l], ln2_b[l])
    return (_trunk(x_prompt, *args), _trunk(x_sample, *args))
```

```python
import functools

import jax
import jax.numpy as jnp
from jax import lax
from jax.experimental import pallas as pl
from jax.experimental.pallas import tpu as pltpu

F32 = jnp.float32
BF16 = jnp.bfloat16
I32 = jnp.int32
HIGHEST = lax.Precision.HIGHEST

D_MODEL = 1024
DEPTH = 1
HEAD_DIM = 64
N_ATTN_HEADS = 8
ATTN_WIDTH = N_ATTN_HEADS * HEAD_DIM
ROT_DIM = HEAD_DIM // 4
ROPE_THETA = 500000.0
DILATED_PATTERNS = ((128, 1), (512, 4), (2048, 16))
N_SSM_HEADS = 8
SSM_HEAD_DIM = 64
SSM_WIDTH = N_SSM_HEADS * SSM_HEAD_DIM
SSM_GROUPS = 2
D_STATE = 128
CONV_K = 5
CHUNK = 128
CONV_CH = SSM_WIDTH + 2 * SSM_GROUPS * D_STATE
MIX_WIDTH = ATTN_WIDTH + SSM_WIDTH
N_EXPERTS = 16
CAPACITY_FACTOR = 2
D_FF_EXPERT = 2816
ALPHA = (2.0 * DEPTH) ** 0.25
LN_EPS = 1e-5
RMS_EPS = 1e-5

LANES = 128
VMEM_LIMIT = 56 * 1024 * 1024
NEG = -1e30
HALF_WIN = 64
RES = 4

IN_PAD = 3 * ATTN_WIDTH + SSM_WIDTH + CONV_CH + LANES


def _params(sem):
    return pltpu.CompilerParams(dimension_semantics=sem, vmem_limit_bytes=VMEM_LIMIT)


def _inproj_kernel(x_ref, w_ref, c_ref, sa_ref, sb_ref, q_ref, k_ref, v_ref, z_ref, xbc_ref, dt_ref, tmp_ref):
    x = x_ref[...].astype(BF16)
    tm = x_ref.shape[0]

    def put_grouped(out_ref, val):
        for g in range(ATTN_WIDTH // LANES):
            tmp_ref[g] = val[:, g * LANES:(g + 1) * LANES]
        for r in range(RES):
            for g in range(ATTN_WIDTH // LANES):
                lo = r * ATTN_WIDTH + g * LANES
                out_ref[:, lo:lo + LANES] = tmp_ref[g, pl.ds(r, tm // RES, stride=RES), :]

    def seg(lo, hi):
        return jnp.dot(x, w_ref[:, lo:hi], preferred_element_type=F32)

    c = c_ref[...]
    sa = sa_ref[...]
    sb = sb_ref[...]

    def rotary(t, scale):
        outs = []
        for g in range(ATTN_WIDTH // LANES):
            tg = t[:, g * LANES:(g + 1) * LANES]
            up = pltpu.roll(tg, LANES - ROT_DIM // 2, 1)
            dn = pltpu.roll(tg, ROT_DIM // 2, 1)
            outs.append((tg * c + up * sa + dn * sb) * scale)
        return jnp.concatenate(outs, axis=1)

    a = ATTN_WIDTH
    put_grouped(q_ref, rotary(seg(0, a), HEAD_DIM ** -0.5))
    put_grouped(k_ref, rotary(seg(a, 2 * a), 1.0))
    put_grouped(v_ref, seg(2 * a, 3 * a))
    z_ref[...] = seg(3 * a, 3 * a + SSM_WIDTH)
    o = 3 * a + SSM_WIDTH
    xbc_ref[...] = seg(o, o + CONV_CH)
    dt_ref[...] = seg(o + CONV_CH, o + CONV_CH + LANES)


def _rotary_tables(seq):
    half = ROT_DIM // 2
    inv = ROPE_THETA ** (-jnp.arange(half, dtype=F32) * 2.0 / ROT_DIM)
    ang = jnp.arange(seq).astype(F32)[:, None] * inv[None, :]
    cos = jnp.cos(ang)
    sin = jnp.sin(ang)
    m = jnp.arange(LANES) % HEAD_DIM
    c = jnp.where(m[None, :] < ROT_DIM, cos[:, m % half], 1.0)
    sa = jnp.where(m[None, :] < half, -sin[:, m % half], 0.0)
    sb = jnp.where((m[None, :] >= half) & (m[None, :] < ROT_DIM), sin[:, m % half], 0.0)
    return c.astype(F32), sa.astype(F32), sb.astype(F32)


def _inproj(x2, w_pad, seq, tm=512):
    T = x2.shape[0]
    c, sa, sb = _rotary_tables(seq)
    nseq = seq // tm
    row = lambda i: (i, 0)
    tab = lambda i: (i % nseq, 0)
    outs = [jax.ShapeDtypeStruct((T // RES, RES * ATTN_WIDTH), F32)] * 3 + [
        jax.ShapeDtypeStruct((T, SSM_WIDTH), F32),
        jax.ShapeDtypeStruct((T, CONV_CH), F32),
        jax.ShapeDtypeStruct((T, LANES), F32)]
    return pl.pallas_call(
        _inproj_kernel,
        grid=(T // tm,),
        in_specs=[pl.BlockSpec((tm, D_MODEL), row),
                  pl.BlockSpec((D_MODEL, IN_PAD), lambda i: (0, 0)),
                  pl.BlockSpec((tm, LANES), tab),
                  pl.BlockSpec((tm, LANES), tab),
                  pl.BlockSpec((tm, LANES), tab)],
        out_specs=[pl.BlockSpec((tm // RES, RES * ATTN_WIDTH), row)] * 3 + [
            pl.BlockSpec((tm, SSM_WIDTH), row),
            pl.BlockSpec((tm, CONV_CH), row),
            pl.BlockSpec((tm, LANES), row)],
        out_shape=outs,
        scratch_shapes=[pltpu.VMEM((ATTN_WIDTH // LANES, tm, LANES), F32)],
        compiler_params=_params(("parallel",)),
        name="inproj",
    )(x2, w_pad, c, sa, sb)


QB = 128
KB = 256
ATTN_GROUP = 8


def _attn_kernel(q_hbm, k_hbm, v_hbm, o_ref, qa, ka, va, m_ref, l_ref, acc_ref, bias_ref, sem, *, seq):
    b = pl.program_id(0)
    h = pl.program_id(1)
    sub = seq // RES
    npair = ATTN_WIDTH // LANES

    copies = []
    for src, dst in ((q_hbm, qa), (k_hbm, ka), (v_hbm, va)):
        for r in range(RES):
            lanes = pl.ds(pl.multiple_of((r * npair + h) * LANES, LANES), LANES)
            copies.append(pltpu.make_async_copy(src.at[b, :, lanes], dst.at[pl.ds(r * sub, sub), :], sem))
    for c in copies:
        c.start()

    row = lax.broadcasted_iota(I32, (2 * QB, KB), 0)
    col = lax.broadcasted_iota(I32, (2 * QB, KB), 1)
    iq = jnp.where(row >= QB, row - QB, row)
    iq_runs = RES * (iq % (QB // RES)) + iq // (QB // RES)
    ik_runs = RES * (col % (KB // RES)) + col // (KB // RES)
    for di in range(3):
        bias_ref[0, di] = jnp.where(jnp.abs(di * HALF_WIN + iq - col) <= HALF_WIN, 0.0, NEG).astype(F32)
        bias_ref[1, di] = jnp.where(jnp.abs(di * HALF_WIN + iq_runs - ik_runs) <= HALF_WIN, 0.0, NEG).astype(F32)
    for c in copies:
        c.wait()

    lane = lax.broadcasted_iota(I32, (QB, LANES), 1)
    head0 = lane < HEAD_DIM
    ones = jnp.ones((KB, LANES), BF16)

    def load(ref, runs, *lead):
        parts = [ref[lead + (rn, slice(None))] for rn in runs]
        return parts[0] if len(parts) == 1 else jnp.concatenate(parts, axis=0)

    def store(ref, runs, val, *lead):
        at = 0
        for rn, size in runs:
            ref[lead + (rn, slice(None))] = val[at:at + size]
            at += size

    for window, dil in DILATED_PATTERNS:
        assert window // (2 * dil) == HALF_WIN and (dil == 1 or dil % RES == 0)
        sub_len = seq // dil
        nblk = sub_len // QB
        assert sub_len >= KB and sub_len % QB == 0 and (dil * nblk) % ATTN_GROUP == 0

        def block_rows(idx, dil=dil, sub_len=sub_len, nblk=nblk):
            r = idx // nblk
            n = idx - r * nblk
            kb = jnp.clip(n * QB - HALF_WIN, 0, sub_len - KB)
            di = (n * QB - kb) // HALF_WIN
            if dil == 1:
                qn, kn = QB // RES, KB // RES
                q_runs = [(pl.ds(pl.multiple_of(c * sub + n * qn, qn), qn), qn) for c in range(RES)]
                k_runs = [pl.ds(pl.multiple_of(c * sub + kb // RES, 8), kn) for c in range(RES)]
                return q_runs, k_runs, bias_ref[1, di]
            step = dil // RES
            base = (r % RES) * sub + r // RES
            if step == 1:
                q_runs = [(pl.ds(pl.multiple_of(base + n * QB, QB), QB), QB)]
                k_runs = [pl.ds(pl.multiple_of(base + kb, HALF_WIN), KB)]
            else:
                q_runs = [(pl.ds(base + step * QB * n, QB, stride=step), QB)]
                k_runs = [pl.ds(base + step * kb, KB, stride=step)]
            return q_runs, k_runs, bias_ref[0, di]

        first = (window, dil) == DILATED_PATTERNS[0]
        assert not first or dil == 1

        def body(gi, carry, block_rows=block_rows, first=first):
            loaded = []
            for u in range(ATTN_GROUP):
                q_runs, k_runs, bias = block_rows(gi * ATTN_GROUP + u)
                qr = [rn for rn, _ in q_runs]
                state = () if first else (load(m_ref, qr, 0), load(m_ref, qr, 1), load(l_ref, qr, 0),
                                          load(l_ref, qr, 1), load(acc_ref, qr))
                loaded.append((q_runs, load(qa, qr), load(ka, k_runs), load(va, k_runs), bias, state))
            results = []
            for q_runs, q, k, v, bias, state in loaded:
                qs = jnp.concatenate([jnp.where(head0, q, 0.0), jnp.where(head0, 0.0, q)], axis=0).astype(BF16)
                s = lax.dot_general(qs, k.astype(BF16), (((1,), (1,)), ((), ())), preferred_element_type=F32)
                s = s + bias
                m_cur = jnp.max(s, axis=1, keepdims=True)
                if first:
                    m_new = jnp.broadcast_to(m_cur, (2 * QB, LANES))
                else:
                    m0, m1, l0, l1, acc = state
                    m_prev = jnp.concatenate([m0, m1], axis=0)
                    m_new = jnp.maximum(m_prev, m_cur)
                p = jnp.exp(s - jnp.concatenate([m_new, m_new], axis=1))
                pv = jnp.dot(p.astype(BF16), jnp.concatenate([v.astype(BF16), ones], axis=1),
                             preferred_element_type=F32)
                pv_l = jnp.where(head0, pv[:QB, :LANES], pv[QB:, :LANES])
                if first:
                    results.append((q_runs, pv_l, m_new, pv[:, LANES:]))
                else:
                    alpha = jnp.exp(m_prev - m_new)
                    l_new = alpha * jnp.concatenate([l0, l1], axis=0) + pv[:, LANES:]
                    a_l = jnp.where(head0, alpha[:QB], alpha[QB:])
                    results.append((q_runs, a_l * acc + pv_l, m_new, l_new))
            for q_runs, acc_new, m_new, l_new in results:
                store(acc_ref, q_runs, acc_new)
                store(m_ref, q_runs, m_new[:QB], 0)
                store(m_ref, q_runs, m_new[QB:], 1)
                store(l_ref, q_runs, l_new[:QB], 0)
                store(l_ref, q_runs, l_new[QB:], 1)
            return carry

        lax.fori_loop(0, dil * nblk // ATTN_GROUP, body, 0)

    for r in range(RES):
        def fin(i, carry, r=r):
            rws = pl.ds(pl.multiple_of(r * sub + i * QB, QB), QB)
            den = jnp.where(head0, l_ref[0, rws, :], l_ref[1, rws, :])
            o_ref[0, pl.ds(r + RES * QB * i, QB, stride=RES), :] = acc_ref[rws, :] / den
            return carry

        lax.fori_loop(0, sub // QB, fin, 0)


def _attention(q, k, v):
    B, sub, _ = q.shape
    seq = sub * RES
    npair = ATTN_WIDTH // LANES
    any_spec = pl.BlockSpec(memory_space=pl.ANY)
    return pl.pallas_call(
        functools.partial(_attn_kernel, seq=seq),
        grid=(B, npair),
        in_specs=[any_spec, any_spec, any_spec],
        out_specs=pl.BlockSpec((1, seq, LANES), lambda b, h: (b, 0, h)),
        out_shape=jax.ShapeDtypeStruct((B, seq, ATTN_WIDTH), F32),
        scratch_shapes=[pltpu.VMEM((seq, LANES), F32),
                        pltpu.VMEM((seq, LANES), F32),
                        pltpu.VMEM((seq, LANES), F32),
                        pltpu.VMEM((2, seq, LANES), F32),
                        pltpu.VMEM((2, seq, LANES), F32),
                        pltpu.VMEM((seq, LANES), F32),
                        pltpu.VMEM((2, 3, 2 * QB, KB), F32),
                        pltpu.SemaphoreType.DMA(())],
        compiler_params=_params(("parallel", "parallel")),
        name="attn",
    )(q, k, v)


SSD_TILE = 512
HALO = 8


def _ssd_kernel(*refs, reverse):
    if reverse:
        u_ref, dt_ref, dtb_ref, alog_ref, yf_ref, z_ref, dsk_ref, nw_ref, y_ref, st_ref = refs
    else:
        (xbc_ref, hp_ref, hn_ref, dt_ref, cw_ref, cb_ref, dtb_ref, alog_ref,
         y_ref, u_ref, ext_ref, st_ref) = refs
    lane_off = N_SSM_HEADS if reverse else 0
    nchunk = SSD_TILE // CHUNK

    @pl.when(pl.program_id(1) == 0)
    def _():
        st_ref[...] = jnp.zeros(st_ref.shape, F32)

    if not reverse:
        ext_ref[0:HALO, :] = hp_ref[0]
        ext_ref[HALO:HALO + SSD_TILE, :] = xbc_ref[...]
        ext_ref[HALO + SSD_TILE:HALO + SSD_TILE + HALO, :] = hn_ref[0]

    li = lax.broadcasted_iota(I32, (CHUNK, CHUNK), 0)
    si = lax.broadcasted_iota(I32, (CHUNK, CHUNK), 1)
    tri = (si >= li) if reverse else (si <= li)
    tri_f = tri.astype(F32)
    ej = lax.broadcasted_iota(I32, (LANES, SSM_WIDTH), 0)
    ec = lax.broadcasted_iota(I32, (LANES, SSM_WIDTH), 1)
    expand = (ej == lane_off + ec // SSM_HEAD_DIM).astype(BF16)
    lane = lax.broadcasted_iota(I32, (CHUNK, LANES), 1)
    lo_half = lane < SSM_HEAD_DIM
    a_row = -jnp.exp(alog_ref[...])
    gw = SSM_WIDTH // SSM_GROUPS
    heads_per_group = N_SSM_HEADS // SSM_GROUPS

    order = range(nchunk - 1, -1, -1) if reverse else range(nchunk)
    for c in order:
        crow = pl.ds(c * CHUNK, CHUNK)
        if reverse:
            u = u_ref[crow, :]
        else:
            base = HALO + c * CHUNK
            conv = None
            for kk in range(CONV_K):
                tap = ext_ref[pl.ds(base + kk - CONV_K // 2, CHUNK), :] * cw_ref[kk:kk + 1, :]
                conv = tap if conv is None else conv + tap
            u = conv + cb_ref[...]
            u = u / (1.0 + jnp.exp(-u))
            u_ref[crow, :] = u
        xs = u[:, :SSM_WIDTH]
        bm = u[:, SSM_WIDTH:SSM_WIDTH + SSM_GROUPS * D_STATE]
        cm = u[:, SSM_WIDTH + SSM_GROUPS * D_STATE:]

        dpre = dt_ref[pl.ds(c * CHUNK, CHUNK), :] + dtb_ref[...]
        dtv = jnp.maximum(dpre, 0.0) + jnp.log(1.0 + jnp.exp(-jnp.abs(dpre)))
        a = dtv * a_row
        acs = jnp.dot(tri_f, a, precision=HIGHEST, preferred_element_type=F32)
        acs_t = acs.T
        tot = acs[0:1, :] if reverse else acs[CHUNK - 1:CHUNK, :]
        stacked = jnp.concatenate([dtv, jnp.exp(acs), jnp.exp(tot - acs)], axis=0)
        s_hi = stacked.astype(BF16)
        s_lo = (stacked - s_hi.astype(F32)).astype(BF16)
        ex = (jnp.dot(s_hi, expand, preferred_element_type=F32)
              + jnp.dot(s_lo, expand, preferred_element_type=F32))
        dt_x = ex[:CHUNK]
        eacs_x = ex[CHUNK:2 * CHUNK]
        dend_x = ex[2 * CHUNK:]
        cdec_x = eacs_x[0:1, :] if reverse else eacs_x[CHUNK - 1:CHUNK, :]
        xdt = xs * dt_x
        wst = (dend_x * xdt).astype(BF16)
        st_prev = st_ref[...]
        st_b = st_prev.astype(BF16)

        ygs = []
        st_new = []
        for g in range(SSM_GROUPS):
            ys = []
            bg = bm[:, g * D_STATE:(g + 1) * D_STATE]
            cg = cm[:, g * D_STATE:(g + 1) * D_STATE].astype(BF16)
            cb = lax.dot_general(cg, bg.astype(BF16), (((1,), (1,)), ((), ())), preferred_element_type=F32)
            ms = []
            for hh in range(heads_per_group):
                h = lane_off + g * heads_per_group + hh
                diff = acs[:, h:h + 1] - acs_t[h:h + 1, :]
                lm = jnp.exp(jnp.where(tri, diff, NEG))
                ms.append((cb * lm).astype(BF16))
            for pp in range(heads_per_group // 2):
                hp = g * (heads_per_group // 2) + pp
                lhs = jnp.concatenate([ms[2 * pp], ms[2 * pp + 1]], axis=1)
                xp = xdt[:, hp * LANES:(hp + 1) * LANES]
                rhs = jnp.concatenate([jnp.where(lo_half, xp, 0.0), jnp.where(lo_half, 0.0, xp)],
                                      axis=0).astype(BF16)
                ys.append(jnp.dot(lhs, rhs, preferred_element_type=F32))
            sc = jnp.dot(bg.T.astype(BF16), wst[:, g * gw:(g + 1) * gw], preferred_element_type=F32)
            yoff = jnp.dot(cg, st_b[:, g * gw:(g + 1) * gw], preferred_element_type=F32)
            ygs.append(jnp.concatenate(ys, axis=1) + yoff * eacs_x[:, g * gw:(g + 1) * gw])
            st_new.append(st_prev[:, g * gw:(g + 1) * gw] * cdec_x[:, g * gw:(g + 1) * gw] + sc)
        st_ref[...] = jnp.concatenate(st_new, axis=1)
        y = jnp.concatenate(ygs, axis=1)

        crow = pl.ds(c * CHUNK, CHUNK)
        if reverse:
            ytot = yf_ref[crow, :] + y + dsk_ref[...] * xs
            zz = z_ref[crow, :]
            yz = ytot * (zz / (1.0 + jnp.exp(-zz)))
            yn = yz * lax.rsqrt(jnp.mean(jnp.square(yz), axis=-1, keepdims=True) + RMS_EPS) * nw_ref[...]
            y_ref[crow, :] = yn.astype(y_ref.dtype)
        else:
            y_ref[crow, :] = y


def _ssd(xbc, dt, z, conv_w, conv_b, dt_bias, a_log, d_skip, ssm_norm_w, B, seq):
    T = B * seq
    nt = seq // SSD_TILE
    x4 = xbc.reshape(B, nt, SSD_TILE, CONV_CH)
    zeros = jnp.zeros((B, 1, HALO, CONV_CH), F32)
    hprev = jnp.concatenate([zeros, x4[:, :-1, SSD_TILE - HALO:, :]], axis=1).reshape(B * nt, HALO, CONV_CH)
    hnext = jnp.concatenate([x4[:, 1:, :HALO, :], zeros], axis=1).reshape(B * nt, HALO, CONV_CH)
    cw = jnp.zeros((HALO, CONV_CH), F32).at[:CONV_K].set(conv_w)
    cb = conv_b.reshape(1, CONV_CH)
    pad = LANES - 2 * N_SSM_HEADS
    dtb = jnp.pad(dt_bias.reshape(1, 2 * N_SSM_HEADS), ((0, 0), (0, pad)))
    alog = jnp.pad(a_log.reshape(1, 2 * N_SSM_HEADS), ((0, 0), (0, pad)), constant_values=-1e30)
    dsk = jnp.repeat(d_skip, SSM_HEAD_DIM).reshape(1, SSM_WIDTH)
    nw = ssm_norm_w.reshape(1, SSM_WIDTH)

    const = lambda b, i: (0, 0)
    state = pltpu.VMEM((D_STATE, SSM_WIDTH), F32)

    fmap = lambda b, i: (b * nt + i, 0)
    hmap = lambda b, i: (b * nt + i, 0, 0)
    yf, u = pl.pallas_call(
        functools.partial(_ssd_kernel, reverse=False),
        grid=(B, nt),
        in_specs=[pl.BlockSpec((SSD_TILE, CONV_CH), fmap),
                  pl.BlockSpec((1, HALO, CONV_CH), hmap),
                  pl.BlockSpec((1, HALO, CONV_CH), hmap),
                  pl.BlockSpec((SSD_TILE, LANES), fmap),
                  pl.BlockSpec((HALO, CONV_CH), const),
                  pl.BlockSpec((1, CONV_CH), const),
                  pl.BlockSpec((1, LANES), const),
                  pl.BlockSpec((1, LANES), const)],
        out_specs=[pl.BlockSpec((SSD_TILE, SSM_WIDTH), fmap), pl.BlockSpec((SSD_TILE, CONV_CH), fmap)],
        out_shape=[jax.ShapeDtypeStruct((T, SSM_WIDTH), F32), jax.ShapeDtypeStruct((T, CONV_CH), F32)],
        scratch_shapes=[pltpu.VMEM((SSD_TILE + 2 * HALO, CONV_CH), F32), state],
        compiler_params=_params(("parallel", "arbitrary")),
        name="ssd_fwd",
    )(xbc, hprev, hnext, dt, cw, cb, dtb, alog)

    rmap = lambda b, i: (b * nt + nt - 1 - i, 0)
    return pl.pallas_call(
        functools.partial(_ssd_kernel, reverse=True),
        grid=(B, nt),
        in_specs=[pl.BlockSpec((SSD_TILE, CONV_CH), rmap),
                  pl.BlockSpec((SSD_TILE, LANES), rmap),
                  pl.BlockSpec((1, LANES), const),
                  pl.BlockSpec((1, LANES), const),
                  pl.BlockSpec((SSD_TILE, SSM_WIDTH), rmap),
                  pl.BlockSpec((SSD_TILE, SSM_WIDTH), rmap),
                  pl.BlockSpec((1, SSM_WIDTH), const),
                  pl.BlockSpec((1, SSM_WIDTH), const)],
        out_specs=pl.BlockSpec((SSD_TILE, SSM_WIDTH), rmap),
        out_shape=jax.ShapeDtypeStruct((T, SSM_WIDTH), BF16),
        scratch_shapes=[state],
        compiler_params=_params(("parallel", "arbitrary")),
        name="ssd_bwd",
    )(u, dt, dtb, alog, yf, z, dsk, nw)


def _layer_norm(h, g, b):
    mu = jnp.mean(h, axis=-1, keepdims=True)
    var = jnp.mean(jnp.square(h - mu), axis=-1, keepdims=True)
    return (h - mu) * lax.rsqrt(var + LN_EPS) * g + b


def _outproj_kernel(a_ref, y_ref, x_ref, w_ref, g_ref, b_ref, wr_ref, x1_ref, lg_ref):
    mix = jnp.dot(a_ref[...].astype(BF16), w_ref[:ATTN_WIDTH, :], preferred_element_type=F32)
    mix = mix + jnp.dot(y_ref[...], w_ref[ATTN_WIDTH:, :], preferred_element_type=F32)
    x1 = _layer_norm(ALPHA * x_ref[...] + mix, g_ref[...], b_ref[...])
    x1_ref[...] = x1
    x_hi = x1.astype(BF16)
    x_lo = (x1 - x_hi.astype(F32)).astype(BF16)
    w_hi = wr_ref[...].astype(BF16)
    w_lo = (wr_ref[...] - w_hi.astype(F32)).astype(BF16)
    nt_dims = (((1,), (1,)), ((), ()))
    lg_ref[...] = (lax.dot_general(w_hi, x_hi, nt_dims, preferred_element_type=F32)
                   + lax.dot_general(w_lo, x_hi, nt_dims, preferred_element_type=F32)
                   + lax.dot_general(w_hi, x_lo, nt_dims, preferred_element_type=F32))


def _outproj(attn, y, x2, w_out_b, g, b, wr_t, tm=512):
    T = x2.shape[0]
    row = lambda i: (i, 0)
    const = lambda i: (0, 0)
    return pl.pallas_call(
        _outproj_kernel,
        grid=(T // tm,),
        in_specs=[pl.BlockSpec((tm, ATTN_WIDTH), row),
                  pl.BlockSpec((tm, SSM_WIDTH), row),
                  pl.BlockSpec((tm, D_MODEL), row),
                  pl.BlockSpec((MIX_WIDTH, D_MODEL), const),
                  pl.BlockSpec((1, D_MODEL), const),
                  pl.BlockSpec((1, D_MODEL), const),
                  pl.BlockSpec((N_EXPERTS, D_MODEL), const)],
        out_specs=[pl.BlockSpec((tm, D_MODEL), row),
                   pl.BlockSpec((N_EXPERTS, tm), lambda i: (0, i))],
        out_shape=[jax.ShapeDtypeStruct((T, D_MODEL), F32),
                   jax.ShapeDtypeStruct((N_EXPERTS, T), F32)],
        compiler_params=_params(("parallel",)),
        name="outproj",
    )(attn, y, x2, w_out_b, g.reshape(1, D_MODEL), b.reshape(1, D_MODEL), wr_t)


SEL_BLK = 256


MIN_NORMAL_BITS = 0x00800000


def _select_kernel(lg_ref, aff_ref, pos_ref, off_ref, res_ref, *, cap, T):
    lg = lg_ref[...]
    ex = jnp.exp(lg - jnp.max(lg, axis=0, keepdims=True))
    aff_ref[...] = ex / jnp.sum(ex, axis=0, keepdims=True)

    def as_f32(bits):
        return lax.bitcast_convert_type(bits, F32)

    def kth_largest(ref):
        def search(i, t):
            cand = t | jnp.left_shift(jnp.int32(1), 30 - i)
            cnt = jnp.sum((ref[...] >= as_f32(cand)).astype(I32), axis=1, keepdims=True)
            return jnp.where((cnt >= cap) & (cand >= MIN_NORMAL_BITS), cand, t)

        return lax.fori_loop(0, 31, search, jnp.zeros((N_EXPERTS, 1), I32))

    thr1 = as_f32(kth_largest(aff_ref))
    res_ref[...] = aff_ref[...] - thr1
    thr2_bits = kth_largest(res_ref)
    thr2 = as_f32(thr2_bits)
    nxt2 = as_f32(jnp.where(thr2_bits == 0, MIN_NORMAL_BITS, thr2_bits + 1))
    n_gt = jnp.sum((res_ref[...] >= nxt2).astype(I32), axis=1, keepdims=True)
    need = (cap - n_gt).astype(F32)

    uj = lax.broadcasted_iota(I32, (SEL_BLK, SEL_BLK), 0)
    ut = lax.broadcasted_iota(I32, (SEL_BLK, SEL_BLK), 1)
    upper = (uj <= ut).astype(BF16)
    nblk = T // SEL_BLK
    olane = lax.broadcasted_iota(I32, off_ref.shape, 1)

    def blk(i, carry):
        c_gt, c_eq, offs = carry
        cols = pl.ds(pl.multiple_of(i * SEL_BLK, SEL_BLK), SEL_BLK)
        res = res_ref[:, cols]
        gt = res >= nxt2
        eq = (res >= thr2) & jnp.logical_not(gt)
        gt_f = gt.astype(F32)
        eq_f = eq.astype(F32)
        st = jnp.concatenate([gt_f, eq_f], axis=0).astype(BF16)
        cs = jnp.dot(st, upper, preferred_element_type=F32)
        gt_ex = c_gt + cs[:N_EXPERTS] - gt_f
        eq_ex = c_eq + cs[N_EXPERTS:] - eq_f
        sel = gt | (eq & (eq_ex < need))
        slot = gt_ex + jnp.minimum(eq_ex, need)
        pos_ref[:, cols] = jnp.where(sel, slot, -1.0).astype(I32)
        start = (c_gt + jnp.minimum(c_eq, need)).astype(I32)
        offs = jnp.where(olane == i, start, offs)
        return (c_gt + cs[:N_EXPERTS, SEL_BLK - 1:SEL_BLK], c_eq + cs[N_EXPERTS:, SEL_BLK - 1:SEL_BLK], offs)

    zero = jnp.zeros((N_EXPERTS, 1), F32)
    _, _, offs = lax.fori_loop(0, nblk, blk, (zero, zero, jnp.zeros(off_ref.shape, I32)))
    off_ref[...] = offs


def _select(lg_t, cap):
    T = lg_t.shape[1]
    nblk = T // SEL_BLK
    owidth = -(-nblk // LANES) * LANES
    return pl.pallas_call(
        functools.partial(_select_kernel, cap=cap, T=T),
        out_shape=[jax.ShapeDtypeStruct((N_EXPERTS, T), F32),
                   jax.ShapeDtypeStruct((N_EXPERTS, T), I32),
                   jax.ShapeDtypeStruct((N_EXPERTS, owidth), I32)],
        scratch_shapes=[pltpu.VMEM((N_EXPERTS, T), F32)],
        compiler_params=pltpu.CompilerParams(vmem_limit_bytes=VMEM_LIMIT),
        name="select",
    )(lg_t)


SLOT_TILE = 256
COMPACT_TILE = 128
COMPACT_STATIC = 2
MOE_TOK = 512
IDX_BASE = 256
SLOT_SUBTILES = 4


def _slots_kernel(off_ref, pos_ref, idx_ref, digits_ref, *, n_tok_tiles):
    e = pl.program_id(0)
    g = pl.program_id(1)

    @pl.when(g == 0)
    def _():
        idx_ref[...] = jnp.zeros(idx_ref.shape, F32)

    for s in range(SLOT_SUBTILES):
        i = g * SLOT_SUBTILES + s
        n0 = off_ref[e * (n_tok_tiles + 1) + i]
        n1 = off_ref[e * (n_tok_tiles + 1) + i + 1]
        j0 = n0 // COMPACT_TILE
        n_tiles = jnp.where(n1 > n0, (n1 - 1) // COMPACT_TILE - j0 + 1, 0)
        pos_row = pos_ref[pl.ds(e, 1), s * MOE_TOK:(s + 1) * MOE_TOK]
        tok = i * MOE_TOK + lax.broadcasted_iota(I32, (MOE_TOK, LANES), 0)
        col = lax.broadcasted_iota(I32, (MOE_TOK, LANES), 1)
        hi = jnp.right_shift(tok, IDX_BASE.bit_length() - 1)
        lo = jnp.bitwise_and(tok, IDX_BASE - 1)
        digits_ref[...] = jnp.where(col == 0, hi, jnp.where(col == 1, lo, 0)).astype(F32).astype(BF16)

        def tile(j, live, pos_row=pos_row):
            sub = lax.broadcasted_iota(I32, (COMPACT_TILE, MOE_TOK), 0)
            onehot = jnp.where((pos_row - j * COMPACT_TILE == sub) & live, 1.0, 0.0).astype(BF16)
            rows = pl.ds(pl.multiple_of(j * COMPACT_TILE, COMPACT_TILE), COMPACT_TILE)
            idx_ref[0, rows, :] += jnp.dot(onehot, digits_ref[...], preferred_element_type=F32)

        last = idx_ref.shape[1] // COMPACT_TILE - 1
        for jj in range(COMPACT_STATIC):
            tile(jnp.minimum(j0 + jj, last), jj < n_tiles)

        def rest(jj, carry, j0=j0, tile=tile):
            tile(j0 + jj, True)
            return carry

        lax.fori_loop(COMPACT_STATIC, n_tiles, rest, 0)


def _slot_tokens(pos, off_flat, cap):
    T = pos.shape[1]
    nt = T // MOE_TOK
    assert T // IDX_BASE <= IDX_BASE and nt % SLOT_SUBTILES == 0
    grid_spec = pltpu.PrefetchScalarGridSpec(
        num_scalar_prefetch=1,
        grid=(N_EXPERTS, nt // SLOT_SUBTILES),
        in_specs=[pl.BlockSpec((N_EXPERTS, SLOT_SUBTILES * MOE_TOK), lambda e, g, off: (0, g))],
        out_specs=pl.BlockSpec((1, cap, LANES), lambda e, i, off: (e, 0, 0)),
        scratch_shapes=[pltpu.VMEM((MOE_TOK, LANES), BF16)])
    idx = pl.pallas_call(
        functools.partial(_slots_kernel, n_tok_tiles=nt),
        grid_spec=grid_spec,
        out_shape=jax.ShapeDtypeStruct((N_EXPERTS, cap, LANES), F32),
        compiler_params=_params(("arbitrary", "arbitrary")),
        name="slots",
    )(off_flat, pos)
    return (idx[:, :, 0] * IDX_BASE + idx[:, :, 1]).astype(I32).reshape(-1)


MOE_SLOT = 512
FF_BLK = 256
N_FF = D_FF_EXPERT // FF_BLK
ROWS_PER_FF = 48
GATHER_ROWS = ROWS_PER_FF * N_FF
assert D_FF_EXPERT % FF_BLK == 0 and GATHER_ROWS >= MOE_SLOT and GATHER_ROWS % 8 == 0


def _moe_kernel(tok_ref, x_hbm, wg_ref, wu_ref, wd_ref, out_ref, xg_ref, xb_ref, acc_ref, sem, *, n_slot_tiles):
    step = pl.program_id(0) * n_slot_tiles + pl.program_id(1)
    total = N_EXPERTS * n_slot_tiles
    cur = step % 2

    def row_copy(tile_idx, r, buf):
        slot = jnp.minimum(r, MOE_SLOT - 1)
        tok = tok_ref[tile_idx * MOE_SLOT + slot]
        return pltpu.make_async_copy(x_hbm.at[pl.ds(tok, 1)], xg_ref.at[buf, pl.ds(r, 1)], sem.at[buf])

    def wait_tile(buf):
        pltpu.make_async_copy(x_hbm.at[pl.ds(0, GATHER_ROWS)], xg_ref.at[buf], sem.at[buf]).wait()

    @pl.when(step == 0)
    def _():
        def first(r, carry):
            row_copy(0, r, 0).start()
            return carry

        lax.fori_loop(0, GATHER_ROWS, first, 0)

    wait_tile(cur)
    nxt = (step + 1) % total
    xb_ref[...] = xg_ref[cur, :MOE_SLOT, :].astype(BF16)
    acc_ref[...] = jnp.zeros(acc_ref.shape, F32)

    def ff(c, carry):
        for r in range(ROWS_PER_FF):
            row_copy(nxt, c * ROWS_PER_FF + r, 1 - cur).start(priority=r % 2)
        cols = pl.ds(pl.multiple_of(c * FF_BLK, FF_BLK), FF_BLK)
        xb = xb_ref[...]
        g = jnp.dot(xb, wg_ref[0, :, cols], preferred_element_type=F32)
        u = jnp.dot(xb, wu_ref[0, :, cols], preferred_element_type=F32)
        h = ((g / (1.0 + jnp.exp(-g))) * u).astype(BF16)
        acc_ref[...] += jnp.dot(h, wd_ref[0, cols, :], preferred_element_type=F32)
        return carry

    lax.fori_loop(0, N_FF, ff, 0)
    out_ref[0] = acc_ref[...].astype(out_ref.dtype)

    @pl.when(step == total - 1)
    def _():
        wait_tile(1 - cur)


def _moe(x1, tok_flat, wg, wu, wd, cap):
    assert cap % MOE_SLOT == 0
    nj = cap // MOE_SLOT
    grid_spec = pltpu.PrefetchScalarGridSpec(
        num_scalar_prefetch=1,
        grid=(N_EXPERTS, nj),
        in_specs=[pl.BlockSpec(memory_space=pl.ANY),
                  pl.BlockSpec((1, D_MODEL, D_FF_EXPERT), lambda e, j, tok: (e, 0, 0)),
                  pl.BlockSpec((1, D_MODEL, D_FF_EXPERT), lambda e, j, tok: (e, 0, 0)),
                  pl.BlockSpec((1, D_FF_EXPERT, D_MODEL), lambda e, j, tok: (e, 0, 0))],
        out_specs=pl.BlockSpec((1, MOE_SLOT, D_MODEL), lambda e, j, tok: (e, j, 0)),
        scratch_shapes=[pltpu.VMEM((2, GATHER_ROWS, D_MODEL), F32),
                        pltpu.VMEM((MOE_SLOT, D_MODEL), BF16),
                        pltpu.VMEM((MOE_SLOT, D_MODEL), F32),
                        pltpu.SemaphoreType.DMA((2,))])
    return pl.pallas_call(
        functools.partial(_moe_kernel, n_slot_tiles=nj),
        grid_spec=grid_spec,
        out_shape=jax.ShapeDtypeStruct((N_EXPERTS, cap, D_MODEL), BF16),
        compiler_params=_params(("arbitrary", "arbitrary")),
        name="moe",
    )(tok_flat, x1, wg, wu, wd)


CMB_TOK = 256
ROW_CHUNK = 16
STAGE_ROWS = N_EXPERTS * (CMB_TOK + 2 * ROW_CHUNK)
NO_ROW = -(2 ** 30)


def _combine_kernel(off_ref, x1_ref, gid_ref, gate_ref, g_ref, b_ref, eo_ref, o_ref, stage_ref, w_ref, sem,
                    *, cap, n_tok_tiles):
    i = pl.program_id(0)
    cur = i % 2

    def chunk_copy(src_row, dst_row, buf, chunks=1):
        rows = chunks * ROW_CHUNK
        return pltpu.make_async_copy(eo_ref.at[pl.ds(src_row, rows)],
                                     stage_ref.at[buf, pl.ds(dst_row, rows)], sem.at[buf])

    def segments(tile):
        base = jnp.int32(0)
        segs = []
        for e in range(N_EXPERTS):
            n0 = off_ref[e * (n_tok_tiles + 1) + tile]
            n1 = off_ref[e * (n_tok_tiles + 1) + tile + 1]
            nch = (n1 + ROW_CHUNK - 1) // ROW_CHUNK - n0 // ROW_CHUNK
            segs.append((base, nch, e * cap + (n0 // ROW_CHUNK) * ROW_CHUNK))
            base = base + nch * ROW_CHUNK
        return segs, base

    def issue(tile, buf):
        segs, _ = segments(tile)
        for seg_base, nch, seg_row in segs:
            def pair(c, carry, seg_base=seg_base, seg_row=seg_row):
                src = pl.multiple_of(seg_row + c * 2 * ROW_CHUNK, ROW_CHUNK)
                dst = pl.multiple_of(seg_base + c * 2 * ROW_CHUNK, ROW_CHUNK)
                chunk_copy(src, dst, buf, 2).start()
                return carry

            lax.fori_loop(0, nch // 2, pair, 0)

            @pl.when(nch % 2 == 1)
            def _(seg_base=seg_base, seg_row=seg_row, nch=nch):
                src = pl.multiple_of(seg_row + (nch - 1) * ROW_CHUNK, ROW_CHUNK)
                dst = pl.multiple_of(seg_base + (nch - 1) * ROW_CHUNK, ROW_CHUNK)
                chunk_copy(src, dst, buf).start()

    @pl.when(i == 0)
    def _():
        stage_ref[...] = jnp.zeros(stage_ref.shape, stage_ref.dtype)
        issue(0, 0)

    @pl.when(i + 1 < n_tok_tiles)
    def _():
        issue(i + 1, 1 - cur)

    segs, total = segments(i)

    def wait(c, carry):
        chunk_copy(0, 0, cur).wait()
        return carry

    lax.fori_loop(0, total // ROW_CHUNK, wait, 0)

    jsub = lax.broadcasted_iota(I32, (SLOT_TILE, CMB_TOK), 0)
    o_ref[...] = ALPHA * x1_ref[...]

    def kchunk(kc, carry):
        j0 = kc * SLOT_TILE
        w_ref[...] = jnp.zeros(w_ref.shape, F32)
        for e, (seg_base, nch, seg_row) in enumerate(segs):
            seg_end = seg_base + nch * ROW_CHUNK

            @pl.when((seg_base < j0 + SLOT_TILE) & (seg_end > j0))
            def _(e=e, seg_base=seg_base, seg_row=seg_row):
                row_id = jsub + (j0 + seg_row - seg_base)
                w_ref[...] = jnp.where(gid_ref[e:e + 1, :] == row_id, gate_ref[e:e + 1, :], w_ref[...])

        rows = stage_ref[cur, pl.ds(pl.multiple_of(j0, SLOT_TILE), SLOT_TILE), :]
        o_ref[...] += lax.dot_general(w_ref[...].astype(BF16), rows, (((0,), (0,)), ((), ())),
                                      preferred_element_type=F32)
        return carry

    lax.fori_loop(0, (total + SLOT_TILE - 1) // SLOT_TILE, kchunk, 0)
    o_ref[...] = _layer_norm(o_ref[...], g_ref[...], b_ref[...])


def _combine(x1, gid_tok, gate_tok, off_flat, eo_flat, g, b, cap):
    T = x1.shape[0]
    nt = T // CMB_TOK
    stage_rows = -(-STAGE_ROWS // SLOT_TILE) * SLOT_TILE
    row = lambda i, off: (i, 0)
    const = lambda i, off: (0, 0)
    grid_spec = pltpu.PrefetchScalarGridSpec(
        num_scalar_prefetch=1,
        grid=(nt,),
        in_specs=[pl.BlockSpec((CMB_TOK, D_MODEL), row),
                  pl.BlockSpec((N_EXPERTS, CMB_TOK), lambda i, off: (0, i)),
                  pl.BlockSpec((N_EXPERTS, CMB_TOK), lambda i, off: (0, i)),
                  pl.BlockSpec((1, D_MODEL), const),
                  pl.BlockSpec((1, D_MODEL), const),
                  pl.BlockSpec(memory_space=pl.ANY)],
        out_specs=pl.BlockSpec((CMB_TOK, D_MODEL), row),
        scratch_shapes=[pltpu.VMEM((2, stage_rows, D_MODEL), BF16),
                        pltpu.VMEM((SLOT_TILE, CMB_TOK), F32),
                        pltpu.SemaphoreType.DMA((2,))])
    return pl.pallas_call(
        functools.partial(_combine_kernel, cap=cap, n_tok_tiles=nt),
        grid_spec=grid_spec,
        out_shape=jax.ShapeDtypeStruct((T, D_MODEL), F32),
        compiler_params=_params(("arbitrary",)),
        name="combine",
    )(off_flat, x1, gid_tok, gate_tok, g.reshape(1, D_MODEL), b.reshape(1, D_MODEL), eo_flat)


def _tile_offsets(off256, cap, T, tok):
    step = tok // SEL_BLK
    o = off256[:, :T // SEL_BLK:step]
    o = jnp.concatenate([o, jnp.full((N_EXPERTS, 1), cap, I32)], axis=1)
    return o.reshape(-1)


def _trunk(x, w_in_p, conv_w, conv_b, dt_bias, a_log, d_skip, ssm_norm_w, w_out_b, ln1_g, ln1_b,
           wr_t, wg, wu, wd, ln2_g, ln2_b):
    B, seq, _ = x.shape
    T = B * seq
    cap = CAPACITY_FACTOR * T // N_EXPERTS
    x2 = x.reshape(T, D_MODEL)
    q, k, v, z, xbc, dt = _inproj(x2, w_in_p, seq)
    shp = (B, seq // RES, RES * ATTN_WIDTH)
    attn = _attention(q.reshape(shp), k.reshape(shp), v.reshape(shp)).reshape(T, ATTN_WIDTH)
    y = _ssd(xbc, dt, z, conv_w, conv_b, dt_bias, a_log, d_skip, ssm_norm_w, B, seq)
    x1, lg_t = _outproj(attn, y, x2, w_out_b, ln1_g, ln1_b, wr_t)
    aff_t, pos, off256 = _select(lg_t, cap)
    tok_flat = _slot_tokens(pos, _tile_offsets(off256, cap, T, MOE_TOK), cap)
    eo = _moe(x1, tok_flat, wg, wu, wd, cap)
    gid = jnp.where(pos >= 0, pos + jnp.arange(N_EXPERTS, dtype=I32)[:, None] * cap, NO_ROW)
    out = _combine(x1, gid, aff_t, _tile_offsets(off256, cap, T, CMB_TOK),
                   eo.reshape(N_EXPERTS * cap, D_MODEL), ln2_g, ln2_b, cap)
    return out.reshape(B, seq, D_MODEL)


def kernel(x_prompt, x_sample, w_in, conv_w, conv_b, dt_bias, a_log, d_skip, ssm_norm_w, w_out, ln1_g, ln1_b,
           w_router, w_gate, w_up, w_down, ln2_g, ln2_b):
    assert DEPTH == 1
    l = 0
    w_in_p = jnp.pad(w_in[l], ((0, 0), (0, IN_PAD - w_in.shape[-1]))).astype(BF16)
    args = (w_in_p, conv_w[l], conv_b[l], dt_bias[l], a_log[l], d_skip[l], ssm_norm_w[l],
            w_out[l].astype(BF16), ln1_g[l], ln1_b[l], w_router[l].T,
            w_gate[l].astype(BF16), w_up[l].astype(BF16), w_down[l].astype(BF16), ln2_g[l], ln2_b[l])
    return (_trunk(x_prompt, *args), _trunk(x_sample, *args))
```
